```python
import jax
import jax.numpy as jnp
from jax import lax
import numpy as np

D_MODEL = 1024
BATCH = 4
SEQ = 4096
DEPTH = 4

GRID_W = 64
CTX_LEN = 256

NA_HEAD_DIM = 64
NA_WIDTH = D_MODEL // 2
NA_HEADS = NA_WIDTH // NA_HEAD_DIM
NA_KH = 8
NA_KW = 16
HG_KEY_DIM = 128
HG_WIDTH = D_MODEL // 2
HG_HEADS = HG_WIDTH // HG_KEY_DIM
HG_CHUNK = 16
EVEN_SPLITS = (NA_WIDTH, NA_WIDTH, NA_WIDTH, HG_WIDTH, HG_WIDTH, HG_WIDTH, HG_WIDTH, HG_WIDTH)
EVEN_CUTS = tuple(sum(EVEN_SPLITS[:i + 1]) for i in range(len(EVEN_SPLITS) - 1))
EVEN_IN = sum(EVEN_SPLITS)
EVEN_MIX = NA_WIDTH + HG_WIDTH
RET_QK_DIM = 256
RET_HEADS = D_MODEL // RET_QK_DIM
RET_V_DIM = 2 * RET_QK_DIM
RET_QK_W = RET_HEADS * RET_QK_DIM
RET_V_W = RET_HEADS * RET_V_DIM
RET_CHUNK = 128
ODD_CUTS = (RET_QK_W, 2 * RET_QK_W, 2 * RET_QK_W + RET_V_W)
ODD_IN = 2 * RET_QK_W + 2 * RET_V_W
FFN_DIM = 2816
N_EXPERTS = 8
TOP_K = 2
EXPERT_DIM = 3584
ROPE_BASE = 10000.0
LN_EPS = 1e-5
NORM_EPS = 1e-6
N_EVEN = (DEPTH + 1) // 2
N_ODD = DEPTH // 2
ALPHA = (2 * DEPTH) ** 0.25
BETA = (8 * DEPTH) ** -0.25
F32 = jnp.float32

kernel_name = 'hybrid_natten_hgrn2_retention_moe_dit'


def layer_norm(x, g, b):
    xf = x.astype(F32)
    mu = jnp.mean(xf, -1, keepdims=True)
    var = jnp.mean(jnp.square(xf - mu), -1, keepdims=True)
    return ((xf - mu) * lax.rsqrt(var + LN_EPS) * g + b).astype(x.dtype)


def group_norm(x):
    xf = x.astype(F32)
    mu = jnp.mean(xf, -1, keepdims=True)
    var = jnp.mean(jnp.square(xf - mu), -1, keepdims=True)
    return ((xf - mu) * lax.rsqrt(var + LN_EPS)).astype(x.dtype)


def rms_norm(x, g):
    xf = x.astype(F32)
    return (xf * lax.rsqrt(jnp.mean(jnp.square(xf), -1, keepdims=True) + NORM_EPS) * g).astype(x.dtype)


def modulate(x, shift, scale):
    return x * (1.0 + scale) + shift


def heads(x, n):
    b, t, w = x.shape
    return x.reshape(b, t, n, w // n).transpose(0, 2, 1, 3)


def merge(x):
    b, n, t, d = x.shape
    return x.transpose(0, 2, 1, 3).reshape(b, t, n * d)


def rope_1d(x, pos):
    half = x.shape[-1] // 2
    inv = ROPE_BASE ** (-jnp.arange(half, dtype=F32) / half)
    ang = pos.astype(F32)[:, None] * inv
    cos, sin = jnp.cos(ang), jnp.sin(ang)
    x1, x2 = x[..., :half], x[..., half:]
    return jnp.concatenate([x1 * cos - x2 * sin, x2 * cos + x1 * sin], -1).astype(x.dtype)


def rope_2d(x, rows, cols):
    h = x.shape[-1] // 2
    return jnp.concatenate([rope_1d(x[..., :h], rows), rope_1d(x[..., h:], cols)], -1)


def swiglu(h, w1, w3, w2):
    return (jax.nn.silu(h @ w1) * (h @ w3)) @ w2


def moe_swiglu(h, rw, rb, w1, w3, w2):
    logits = (h @ rw + rb).astype(F32)
    top_val, top_idx = lax.top_k(logits, TOP_K)
    gate = jax.nn.softmax(top_val, axis=-1)
    combine = jnp.sum(jax.nn.one_hot(top_idx, N_EXPERTS, dtype=F32) * gate[..., None], axis=-2)
    out = jnp.zeros_like(h)
    for e in range(N_EXPERTS):
        out = out + combine[..., e:e + 1].astype(h.dtype) * swiglu(h, w1[e], w3[e], w2[e])
    return out


def full_attention(q, k, v):
    s = jnp.einsum('bhqd,bhkd->bhqk', q, k).astype(F32) * q.shape[-1] ** -0.5
    p = jax.nn.softmax(s, axis=-1)
    return jnp.einsum('bhqk,bhkd->bhqd', p, v).astype(q.dtype)


def neighbourhood_attention(q, k, v, kc, vc, rpb):
    b, h, n, dh = q.shape
    rows = n // GRID_W
    kh = min(NA_KH, rows)
    grid = lambda a: a.reshape(b, h, rows, GRID_W, dh)
    qg, kg, vg = grid(q) * dh ** -0.5, grid(k), grid(v)
    r = jnp.arange(rows)
    band = jnp.clip(r - kh // 2, 0, rows - kh)[:, None] + jnp.arange(kh)[None, :]
    k_band = kg[:, :, band]
    v_band = vg[:, :, band]
    col = jnp.arange(GRID_W)
    c_start = jnp.clip(col - NA_KW // 2, 0, GRID_W - NA_KW)
    col_in = (col[None, :] >= c_start[:, None]) & (col[None, :] < c_start[:, None] + NA_KW)
    dr = band - r[:, None] + NA_KH - 1
    dc = jnp.clip(col[None, :] - col[:, None], 1 - NA_KW, NA_KW - 1) + NA_KW - 1
    bias = rpb[:, dr[:, None, :, None], dc[None, :, None, :]]
    s_band = jnp.einsum('bhrcd,bhrkmd->bhrckm', qg, k_band).astype(F32) + bias.astype(F32)
    s_band = jnp.where(col_in[:, None, :], s_band, -jnp.inf)
    s_ctx = jnp.einsum('bhrcd,bhjd->bhrcj', qg, kc).astype(F32)
    nb = kh * GRID_W
    p = jax.nn.softmax(jnp.concatenate([s_band.reshape(b, h, rows, GRID_W, nb), s_ctx], -1), axis=-1)
    o = (jnp.einsum('bhrckm,bhrkmd->bhrcd', p[..., :nb].reshape(b, h, rows, GRID_W, kh, GRID_W), v_band)
         + jnp.einsum('bhrcj,bhjd->bhrcd', p[..., nb:], vc))
    return o.reshape(b, h, n, dh).astype(q.dtype)


def chunk_recurrence(q, k, v, log_f, s0, chunk, per_dim):
    b, h, t, dk = q.shape
    n = t // chunk
    blk = lambda a: a.reshape(b, h, n, chunk, a.shape[-1])
    qb, kb, vb = blk(q), blk(k), blk(v)
    cum = jnp.cumsum(blk(log_f.astype(F32)), axis=-2)
    cum_last = cum[..., -1:, :]
    lower = jnp.tril(jnp.ones((chunk, chunk), bool))
    diff = cum[..., :, None, :] - cum[..., None, :, :]
    if per_dim:
        dec = jnp.exp(jnp.where(lower[..., None], diff, -jnp.inf))
        att = jnp.einsum('bhntd,bhnsd,bhntsd->bhnts', qb, kb, dec)
    else:
        dec = jnp.exp(jnp.where(lower, diff[..., 0], -jnp.inf))
        att = jnp.einsum('bhntd,bhnsd->bhnts', qb, kb) * dec
    o_intra = jnp.einsum('bhnts,bhnsv->bhntv', att, vb)
    q_in = jnp.moveaxis(qb * jnp.exp(cum), 2, 0)
    u = jnp.einsum('bhncd,bhncv->nbhdv', kb * jnp.exp(cum_last - cum), vb)
    g = jnp.moveaxis(jnp.exp(cum_last[..., 0, :]), 2, 0)

    def step(s, inp):
        qi, gi, ui = inp
        return gi[..., None] * s + ui, jnp.einsum('bhcd,bhdv->bhcv', qi, s)

    s_final, o_inter = lax.scan(step, s0.astype(F32), (q_in, g, u))
    o = o_intra + jnp.moveaxis(o_inter, 0, 2)
    return o.reshape(b, h, t, v.shape[-1]).astype(v.dtype), s_final


def final_state(k, v, log_f):
    cum = jnp.cumsum(log_f.astype(F32), axis=-2)
    return jnp.einsum('bhtd,bhtv->bhdv', k * jnp.exp(cum[..., -1:, :] - cum), v)


def bidirectional_recurrence(q_x, v_x, k_x, lf_x, q_c, v_c, k_c, lf_c, chunk, per_dim, ctx_out):
    b, h, _, dk = k_c[0].shape
    dv = v_c.shape[-1]
    outs_x, outs_c = [], []
    for d in range(2):
        rev = (lambda a: jnp.flip(a, axis=2)) if d else (lambda a: a)
        if ctx_out:
            o_c, s_c = chunk_recurrence(rev(q_c), rev(k_c[d]), rev(v_c), rev(lf_c[d]),
                                        jnp.zeros((b, h, dk, dv), F32), chunk, per_dim)
            outs_c.append(rev(o_c))
        else:
            s_c = final_state(rev(k_c[d]), rev(v_c), rev(lf_c[d]))
        o_x, _ = chunk_recurrence(rev(q_x), rev(k_x[d]), rev(v_x), rev(lf_x[d]), s_c, chunk, per_dim)
        outs_x.append(rev(o_x))
    return outs_x[0] + outs_x[1], (outs_c[0] + outs_c[1] if ctx_out else None)


def hgrn_log_forget(z, lb):
    lb = lb.astype(F32)
    return jnp.logaddexp(jnp.log(lb), jnp.log1p(-lb) + jax.nn.log_sigmoid(z.astype(F32)))


def hgrn2_inputs(p, lb):
    q = heads(jax.nn.silu(p[0]), HG_HEADS)
    lf = [hgrn_log_forget(heads(p[1 + d], HG_HEADS), lb[d].reshape(HG_HEADS, 1, HG_KEY_DIM)) for d in range(2)]
    k = [-jnp.expm1(f) for f in lf]
    return q, heads(p[3], HG_HEADS), k, lf


def even_mixer(hx, hc, w_in, w_out, rpb, lb, norm_g, ctx_out):
    px = jnp.split(hx @ w_in, EVEN_CUTS, axis=-1)
    pc = jnp.split(hc @ w_in, EVEN_CUTS, axis=-1)
    na_q, na_k, na_v = (heads(p, NA_HEADS) for p in px[:3])
    nc_q, nc_k, nc_v = (heads(p, NA_HEADS) for p in pc[:3])
    a_x = neighbourhood_attention(na_q, na_k, na_v, nc_k, nc_v, rpb)
    qx, vx, kx, lfx = hgrn2_inputs(px[3:7], lb)
    qc, vc, kc, lfc = hgrn2_inputs(pc[3:7], lb)
    ox, oc = bidirectional_recurrence(qx, vx, kx, lfx, qc, vc, kc, lfc, HG_CHUNK, True, ctx_out)
    g_x = merge(rms_norm(ox, norm_g) * jax.nn.silu(heads(px[7], HG_HEADS)))
    yx = jnp.concatenate([merge(a_x), g_x], -1) @ w_out
    if not ctx_out:
        return yx, None
    a_c = full_attention(nc_q, nc_k, nc_v)
    g_c = merge(rms_norm(oc, norm_g) * jax.nn.silu(heads(pc[7], HG_HEADS)))
    return yx, jnp.concatenate([merge(a_c), g_c], -1) @ w_out


def odd_mixer(hx, hc, w_in, w_out, log_decay, ctx_out):
    b, n, _ = hx.shape
    l = hc.shape[1]
    qx, kx, vx, gx = jnp.split(hx @ w_in, ODD_CUTS, axis=-1)
    t = jnp.arange(n)
    rows, cols = t // GRID_W, t % GRID_W
    qx = rope_2d(heads(qx, RET_HEADS), rows, cols)
    kx = rope_2d(heads(kx, RET_HEADS), rows, cols) * RET_QK_DIM ** -0.5
    vx = heads(vx, RET_HEADS)
    if ctx_out:
        qc, kc, vc, gc = jnp.split(hc @ w_in, ODD_CUTS, axis=-1)
        qc = heads(qc, RET_HEADS)
    else:
        kc, vc = jnp.split(hc @ w_in[:, ODD_CUTS[0]:ODD_CUTS[2]], [RET_QK_W], axis=-1)
        qc = None
    kc = heads(kc, RET_HEADS) * RET_QK_DIM ** -0.5
    vc = heads(vc, RET_HEADS)
    decay = lambda tl: [jnp.broadcast_to(log_decay[d].astype(F32)[None, :, None, None], (b, RET_HEADS, tl, 1))
                        for d in range(2)]
    ox, oc = bidirectional_recurrence(qx, vx, [kx, kx], decay(n), qc, vc, [kc, kc], decay(l),
                                      RET_CHUNK, False, ctx_out)
    yx = (merge(group_norm(ox)) * jax.nn.silu(gx)) @ w_out
    if not ctx_out:
        return yx, None
    return yx, (merge(group_norm(oc)) * jax.nn.silu(gc)) @ w_out


def setup_inputs(seed: int = 0) -> dict:
    key = jax.random.key(seed)
    keys = iter(jax.random.split(key, 32))

    def nrm(shape, scale):
        return jax.random.normal(next(keys), shape, F32) * scale

    d = D_MODEL
    decay0 = jnp.log1p(-jnp.exp2(-5.0 - jnp.arange(RET_HEADS, dtype=F32)))
    return {
        'x': nrm((BATCH, SEQ, d), 1.0),
        'c': nrm((BATCH, d), 1.0),
        'ctx': nrm((BATCH, CTX_LEN, d), 1.0),
        'c_ctx': nrm((d,), 1.0),
        'ada_w': nrm((DEPTH, d, 6 * d), 0.5 * d ** -0.5),
        'ada_b': nrm((DEPTH, 6 * d), 0.02),
        'ln_g': 1.0 + nrm((DEPTH, 2, d), 0.02),
        'ln_b': nrm((DEPTH, 2, d), 0.02),
        'e_w_in': nrm((N_EVEN, d, EVEN_IN), d ** -0.5),
        'e_w_out': nrm((N_EVEN, EVEN_MIX, d), BETA * EVEN_MIX ** -0.5),
        'na_rpb': nrm((N_EVEN, NA_HEADS, 2 * NA_KH - 1, 2 * NA_KW - 1), 0.05),
        'hg_lb_logits': nrm((2, N_EVEN, HG_WIDTH), 0.5),
        'hg_norm_g': 1.0 + nrm((N_EVEN, HG_KEY_DIM), 0.02),
        'ffn_w1': nrm((N_EVEN, d, FFN_DIM), d ** -0.5),
        'ffn_w3': nrm((N_EVEN, d, FFN_DIM), d ** -0.5),
        'ffn_w2': nrm((N_EVEN, FFN_DIM, d), BETA * FFN_DIM ** -0.5),
        'o_w_in': nrm((N_ODD, d, ODD_IN), d ** -0.5),
        'o_w_out': nrm((N_ODD, RET_V_W, d), BETA * RET_V_W ** -0.5),
        'ret_log_decay': decay0 * jax.random.uniform(next(keys), (N_ODD, 2, RET_HEADS), F32, 0.9, 1.1),
        'router_w': nrm((N_ODD, d, N_EXPERTS), d ** -0.5),
        'router_b': nrm((N_ODD, N_EXPERTS), 0.01),
        'moe_w1': nrm((N_ODD, N_EXPERTS, d, EXPERT_DIM), d ** -0.5),
        'moe_w3': nrm((N_ODD, N_EXPERTS, d, EXPERT_DIM), d ** -0.5),
        'moe_w2': nrm((N_ODD, N_EXPERTS, EXPERT_DIM, d), BETA * EXPERT_DIM ** -0.5),
    }


def reference(x, c, ctx, c_ctx, ada_w, ada_b, ln_g, ln_b, e_w_in, e_w_out, na_rpb, hg_lb_logits, hg_norm_g,
              ffn_w1, ffn_w3, ffn_w2, o_w_in, o_w_out, ret_log_decay, router_w, router_b, moe_w1, moe_w3, moe_w2):
    lb_cum = jnp.cumsum(jax.nn.softmax(hg_lb_logits.astype(F32), axis=1), axis=1)
    lower_bounds = lb_cum - lb_cum[:, :1]
    s_x = jax.nn.silu(c)
    s_c = jax.nn.silu(c_ctx)
    for layer in range(DEPTH):
        j = layer // 2
        last = layer == DEPTH - 1
        mx = [m[:, None, :] for m in jnp.split(s_x @ ada_w[layer] + ada_b[layer], 6, axis=-1)]
        mc = jnp.split(s_c @ ada_w[layer] + ada_b[layer], 6, axis=-1)
        hx = modulate(x, mx[0], mx[1])
        hc = modulate(ctx, mc[0], mc[1])
        if layer % 2 == 0:
            yx, yc = even_mixer(hx, hc, e_w_in[j], e_w_out[j], na_rpb[j], lower_bounds[:, j], hg_norm_g[j], not last)
            ffn = lambda h: swiglu(h, ffn_w1[j], ffn_w3[j], ffn_w2[j])
        else:
            yx, yc = odd_mixer(hx, hc, o_w_in[j], o_w_out[j], ret_log_decay[j], not last)
            ffn = lambda h: moe_swiglu(h, router_w[j], router_b[j], moe_w1[j], moe_w3[j], moe_w2[j])
        x = layer_norm(ALPHA * x + mx[2] * yx, ln_g[layer, 0], ln_b[layer, 0])
        x = layer_norm(ALPHA * x + mx[5] * ffn(modulate(x, mx[3], mx[4])), ln_g[layer, 1], ln_b[layer, 1])
        if not last:
            ctx = layer_norm(ALPHA * ctx + mc[2] * yc, ln_g[layer, 0], ln_b[layer, 0])
            ctx = layer_norm(ALPHA * ctx + mc[5] * ffn(modulate(ctx, mc[3], mc[4])), ln_g[layer, 1], ln_b[layer, 1])
    return x
```

```python
import functools
import math

import jax
import jax.numpy as jnp
from jax import lax
from jax.experimental import pallas as pl
from jax.experimental.pallas import tpu as pltpu

F32 = jnp.float32
BF16 = jnp.bfloat16

GRID_W = 64
NA_HEAD_DIM = 64
NA_KH = 8
NA_KW = 16
HG_KEY_DIM = 128
RET_QK_DIM = 256
RET_V_DIM = 512
N_EXPERTS = 8
ROPE_BASE = 10000.0
LN_EPS = 1e-5
NORM_EPS = 1e-6
LANES = 128
CHUNK = 128
ROW_TILE = 512
VMEM_LIMIT = 48 * 1024 * 1024
MASK_VALUE = -1e30


def _cparams(*sem):
    return pltpu.CompilerParams(dimension_semantics=sem, vmem_limit_bytes=VMEM_LIMIT)


def _dot(a, b):
    return jnp.dot(a, b, preferred_element_type=F32)


def _dot_nt(a, b):
    return lax.dot_general(a, b, (((1,), (1,)), ((), ())), preferred_element_type=F32)


def _dot_tn(a, b):
    return lax.dot_general(a, b, (((0,), (0,)), ((), ())), preferred_element_type=F32)


def _silu(x):
    return x * jax.nn.sigmoid(x)


def _pick_tile(n, target):
    best = None
    for t in range(LANES, min(n, target) + 1, LANES):
        if n % t == 0:
            best = t
    assert best is not None, (n, target)
    return best


def _mm_kernel(a_ref, w_ref, *rest, silu_a, has_bias):
    o_ref = rest[-1]
    a = a_ref[...]
    if silu_a:
        a = _silu(a.astype(F32))
    acc = _dot(a.astype(BF16), w_ref[...].astype(BF16))
    if has_bias:
        acc = acc + rest[0][...]
    o_ref[...] = acc.astype(o_ref.dtype)


def _matmul(a, w, *, col0=0, n_cols=None, n_rows=None, tm=ROW_TILE, tn=512, out_dtype=F32, bias=None,
            silu_a=False):
    k = a.shape[1]
    n_rows = a.shape[0] if n_rows is None else n_rows
    n_cols = w.shape[1] - col0 if n_cols is None else n_cols
    tn = _pick_tile(math.gcd(n_cols, col0) if col0 else n_cols, tn)
    assert n_rows % tm == 0 and n_cols % tn == 0 and col0 % tn == 0
    cb = col0 // tn
    in_specs = [pl.BlockSpec((tm, k), lambda i, j: (i, 0)),
                pl.BlockSpec((k, tn), lambda i, j: (0, cb + j))]
    args = [a, w]
    if bias is not None:
        in_specs.append(pl.BlockSpec((1, tn), lambda i, j: (0, cb + j)))
        args.append(bias.reshape(1, -1))
    return pl.pallas_call(
        functools.partial(_mm_kernel, silu_a=silu_a, has_bias=bias is not None),
        grid=(n_rows // tm, n_cols // tn),
        in_specs=in_specs,
        out_specs=pl.BlockSpec((tm, tn), lambda i, j: (i, j)),
        out_shape=jax.ShapeDtypeStruct((n_rows, n_cols), out_dtype),
        compiler_params=_cparams("parallel", "parallel"),
        name="matmul",
    )(*args)


def _modulate_kernel(x_ref, mod_ref, h_ref):
    m = mod_ref[0]
    h_ref[...] = (x_ref[...] * (1.0 + m[1:2]) + m[0:1]).astype(h_ref.dtype)


def _mod_index(tiles_per_batch, n_batch):
    return lambda i: (jnp.minimum(i // tiles_per_batch, n_batch), 0, 0)


def _modulate(x, mod, *, rows_per_batch, tm=ROW_TILE):
    m, d = x.shape
    return pl.pallas_call(
        _modulate_kernel,
        grid=(m // tm,),
        in_specs=[pl.BlockSpec((tm, d), lambda i: (i, 0)),
                  pl.BlockSpec((1, 2, d), _mod_index(rows_per_batch // tm, mod.shape[0] - 1))],
        out_specs=pl.BlockSpec((tm, d), lambda i: (i, 0)),
        out_shape=jax.ShapeDtypeStruct((m, d), BF16),
        compiler_params=_cparams("parallel"),
        name="modulate",
    )(x, mod)


def _ln_kernel(*refs, alpha, n_y, router, n_experts):
    x_ref = refs[0]
    y_refs = refs[1:1 + n_y]
    pos = 1 + n_y
    if n_y == 2:
        yg_ref = refs[pos]
        pos += 1
    mod_ref, lnp_ref = refs[pos], refs[pos + 1]
    pos += 2
    if router:
        rw_ref, rb_ref = refs[pos], refs[pos + 1]
        pos += 2
    xo_ref, h_ref = refs[pos], refs[pos + 1]
    m = mod_ref[0]
    if n_y == 2:
        yg = yg_ref[...]
        y = yg[:, 0:1] * y_refs[0][...].astype(F32) + yg[:, 1:2] * y_refs[1][...].astype(F32)
    else:
        y = y_refs[0][...].astype(F32)
    z = alpha * x_ref[...] + m[0:1] * y
    mu = jnp.mean(z, axis=-1, keepdims=True)
    zc = z - mu
    var = jnp.mean(zc * zc, axis=-1, keepdims=True)
    xn = zc * lax.rsqrt(var + LN_EPS) * lnp_ref[0:1, :] + lnp_ref[1:2, :]
    xo_ref[...] = xn
    hf = xn * (1.0 + m[2:3]) + m[1:2]
    h_ref[...] = hf.astype(h_ref.dtype)
    if router:
        g_ref, i_ref = refs[pos + 2], refs[pos + 3]
        logits = jnp.dot(hf, rw_ref[...], preferred_element_type=F32, precision=lax.Precision.HIGHEST) + rb_ref[...]
        lane = lax.broadcasted_iota(jnp.int32, logits.shape, 1).astype(F32)
        neg = -jnp.inf
        lg = jnp.where(lane < n_experts, logits, neg)
        m1 = jnp.max(lg, axis=-1, keepdims=True)
        i1 = jnp.min(jnp.where(lg == m1, lane, float(LANES)), axis=-1, keepdims=True)
        lg2 = jnp.where(lane == i1, neg, lg)
        m2 = jnp.max(lg2, axis=-1, keepdims=True)
        i2 = jnp.min(jnp.where(lg2 == m2, lane, float(LANES)), axis=-1, keepdims=True)
        e2 = jnp.exp(m2 - m1)
        den = 1.0 + e2
        g_ref[...] = jnp.where(lane == 0, 1.0 / den, jnp.where(lane == 1, e2 / den, 0.0))
        i_ref[...] = jnp.where(lane == 0, i1, jnp.where(lane == 1, i2, 0.0)).astype(jnp.int32)


def _residual_ln(x, ys, mod, lnp, *, alpha, rows_per_batch, n_rows=None, y_gates=None, router=None, tm=ROW_TILE):
    d = x.shape[1]
    n_rows = x.shape[0] if n_rows is None else n_rows
    row = pl.BlockSpec((tm, d), lambda i: (i, 0))
    lane_row = pl.BlockSpec((tm, LANES), lambda i: (i, 0))
    in_specs = [row] + [row] * len(ys)
    args = [x, *ys]
    if len(ys) == 2:
        in_specs.append(lane_row)
        args.append(y_gates)
    in_specs += [pl.BlockSpec((1, 3, d), _mod_index(rows_per_batch // tm, mod.shape[0] - 1)),
                 pl.BlockSpec((2, d), lambda i: (0, 0))]
    args += [mod, lnp]
    out_specs = [row, row]
    out_shape = [jax.ShapeDtypeStruct((n_rows, d), F32), jax.ShapeDtypeStruct((n_rows, d), BF16)]
    if router is not None:
        in_specs += [pl.BlockSpec((d, LANES), lambda i: (0, 0)), pl.BlockSpec((1, LANES), lambda i: (0, 0))]
        args += list(router)
        out_specs += [lane_row, lane_row]
        out_shape += [jax.ShapeDtypeStruct((n_rows, LANES), F32), jax.ShapeDtypeStruct((n_rows, LANES), jnp.int32)]
    return pl.pallas_call(
        functools.partial(_ln_kernel, alpha=alpha, n_y=len(ys), router=router is not None, n_experts=N_EXPERTS),
        grid=(n_rows // tm,),
        in_specs=in_specs,
        out_specs=out_specs,
        out_shape=out_shape,
        compiler_params=_cparams("parallel"),
        name="residual_ln",
    )(*args)


def _gffn_kernel(te_ref, tv_ref, x_ref, w1_ref, w3_ref, w2_ref, o_ref):
    i, f = pl.program_id(0), pl.program_id(1)

    @pl.when(f == 0)
    def _():
        o_ref[...] = jnp.zeros_like(o_ref)

    @pl.when(tv_ref[i] != 0)
    def _():
        x = x_ref[...]
        h1 = _dot(x, w1_ref[0])
        h3 = _dot(x, w3_ref[0])
        a = (_silu(h1) * h3).astype(BF16)
        o_ref[...] += _dot(a, w2_ref[0])


def _grouped_swiglu(xs, w1, w3, w2, tile_expert, tile_valid, *, tm, tf):
    p, d = xs.shape
    _, _, f = w1.shape
    assert p % tm == 0 and f % tf == 0
    grid_spec = pltpu.PrefetchScalarGridSpec(
        num_scalar_prefetch=2,
        grid=(p // tm, f // tf),
        in_specs=[pl.BlockSpec((tm, d), lambda i, j, te, tv: (i, 0)),
                  pl.BlockSpec((1, d, tf), lambda i, j, te, tv: (te[i], 0, j * tv[i])),
                  pl.BlockSpec((1, d, tf), lambda i, j, te, tv: (te[i], 0, j * tv[i])),
                  pl.BlockSpec((1, tf, d), lambda i, j, te, tv: (te[i], j * tv[i], 0))],
        out_specs=pl.BlockSpec((tm, d), lambda i, j, te, tv: (i, 0)),
    )
    return pl.pallas_call(
        _gffn_kernel,
        grid_spec=grid_spec,
        out_shape=jax.ShapeDtypeStruct((p, d), F32),
        compiler_params=_cparams("parallel", "arbitrary"),
        name="grouped_swiglu",
    )(tile_expert, tile_valid, xs, w1, w3, w2)


def _route(idx, n_experts, tm):
    m = idx.shape[0]
    flat = idx.reshape(-1)
    n = flat.shape[0]
    p = -(-(n + n_experts * (tm - 1)) // tm) * tm
    order = jnp.argsort(flat, stable=True).astype(jnp.int32)
    sorted_e = flat[order]
    counts = jnp.sum(flat[:, None] == jnp.arange(n_experts, dtype=jnp.int32)[None, :], axis=0).astype(jnp.int32)
    padded = -(-counts // tm) * tm
    pad_end = jnp.cumsum(padded)
    pad_off = pad_end - padded
    off = jnp.cumsum(counts) - counts
    slot_sorted = pad_off[sorted_e] + jnp.arange(n, dtype=jnp.int32) - off[sorted_e]
    slot_token = jnp.zeros((p,), jnp.int32).at[slot_sorted].set(order // 2)
    assign_slot = jnp.zeros((n,), jnp.int32).at[order].set(slot_sorted).reshape(m, 2)
    tile_start = jnp.arange(p // tm, dtype=jnp.int32) * tm
    tile_valid = (tile_start < pad_end[-1]).astype(jnp.int32)
    tile_expert = jnp.minimum(jnp.searchsorted(pad_end, tile_start, side="right"), n_experts - 1).astype(jnp.int32)
    return slot_token, assign_slot, tile_expert, tile_valid


def _softmax_pv(parts):
    mx = functools.reduce(jnp.maximum, [jnp.max(s, axis=-1, keepdims=True) for s, _ in parts])
    ps = [jnp.exp(s - mx) for s, _ in parts]
    den = functools.reduce(jnp.add, [jnp.sum(p, axis=-1, keepdims=True) for p in ps])
    num = functools.reduce(jnp.add, [_dot(p.astype(BF16), v) for p, (_, v) in zip(ps, parts)])
    return num / den


def _na_kernel(*refs, rows, gw, ctx_out):
    if ctx_out:
        q_ref, k_ref, v_ref, kc_ref, vc_ref, qc_ref, bias_ref, o_ref, oc_ref = refs
    else:
        q_ref, k_ref, v_ref, kc_ref, vc_ref, bias_ref, o_ref = refs
    lane = lax.broadcasted_iota(jnp.int32, (1, LANES), 1)
    head_lanes = [lane < NA_HEAD_DIM, lane >= NA_HEAD_DIM]
    scale = NA_HEAD_DIM ** -0.5
    kc = kc_ref[...]
    vc = vc_ref[...]
    band = NA_KH * gw

    def body(r, carry):
        b0 = jnp.clip(r - NA_KH // 2, 0, rows - NA_KH)
        dr0 = b0 - r + NA_KH - 1
        q = q_ref[pl.ds(pl.multiple_of(r * gw, gw), gw), :]
        ks = pl.ds(pl.multiple_of(b0 * gw, gw), band)
        kb = k_ref[ks, :]
        vb = v_ref[ks, :]
        outs = []
        for h in range(2):
            qh = jnp.where(head_lanes[h], q, 0) * scale
            s = _dot_nt(qh, kb) + bias_ref[h, dr0]
            sc = _dot_nt(qh, kc)
            outs.append(_softmax_pv([(s, vb), (sc, vc)]))
        o_ref[pl.ds(pl.multiple_of(r * gw, gw), gw), :] = jnp.where(head_lanes[0], outs[0], outs[1]).astype(o_ref.dtype)
        return carry

    lax.fori_loop(0, rows, body, 0)
    if ctx_out:
        qc = qc_ref[...]
        outs = []
        for h in range(2):
            qh = jnp.where(head_lanes[h], qc, 0) * scale
            outs.append(_softmax_pv([(_dot_nt(qh, kc), vc)]))
        oc_ref[...] = jnp.where(head_lanes[0], outs[0], outs[1]).astype(oc_ref.dtype)


def _na_bias_table(rpb, gw):
    col = jnp.arange(gw)
    c_start = jnp.clip(col - NA_KW // 2, 0, gw - NA_KW)
    col_in = (col[None, :] >= c_start[:, None]) & (col[None, :] < c_start[:, None] + NA_KW)
    dc = jnp.clip(col[None, :] - col[:, None], 1 - NA_KW, NA_KW - 1) + NA_KW - 1
    t = jnp.where(col_in[None, None], rpb[:, :, dc].astype(F32), MASK_VALUE)
    win = jnp.stack([t[:, d:d + NA_KH] for d in range(NA_KH)], axis=1)
    return win.transpose(0, 1, 3, 2, 4).reshape(rpb.shape[0], NA_KH, gw, NA_KH * gw)


def _neighbourhood_attention(proj, rpb, *, n_batch, seq, ctx_len, ctx_out):
    width = proj.shape[1] // 3
    pairs = width // LANES
    rows = seq // GRID_W
    bias = _na_bias_table(rpb, GRID_W)
    cblk0 = n_batch * seq // ctx_len
    xspec = lambda g: pl.BlockSpec((seq, LANES), lambda b, p: (b, g * pairs + p))
    cspec = lambda g: pl.BlockSpec((ctx_len, LANES), lambda b, p: (cblk0 + b, g * pairs + p))
    in_specs = [xspec(0), xspec(1), xspec(2), cspec(1), cspec(2)]
    args = [proj, proj, proj, proj, proj]
    if ctx_out:
        in_specs.append(cspec(0))
        args.append(proj)
    in_specs.append(pl.BlockSpec((2, NA_KH, GRID_W, NA_KH * GRID_W), lambda b, p: (p, 0, 0, 0)))
    args.append(bias)
    out_specs = [pl.BlockSpec((seq, LANES), lambda b, p: (b, p))]
    out_shape = [jax.ShapeDtypeStruct((n_batch * seq, width), BF16)]
    if ctx_out:
        out_specs.append(pl.BlockSpec((ctx_len, LANES), lambda b, p: (b, p)))
        out_shape.append(jax.ShapeDtypeStruct((n_batch * ctx_len, width), BF16))
    return pl.pallas_call(
        functools.partial(_na_kernel, rows=rows, gw=GRID_W, ctx_out=ctx_out),
        grid=(n_batch, pairs),
        in_specs=in_specs,
        out_specs=out_specs,
        out_shape=out_shape,
        compiler_params=_cparams("parallel", "parallel"),
        name="neighbourhood_attention",
    )(*args)


def _hg_kernel(*refs, rev, n_chunks, n_blocks, epilogue):
    if epilogue:
        q_ref, f_ref, i_ref, lb_ref, s0_ref, g_ref, prev_ref, ng_ref, o_ref, sT_ref, st_scr = refs
    else:
        q_ref, f_ref, i_ref, lb_ref, s0_ref, o_ref, sT_ref, st_scr = refs
    blk = pl.program_id(2)

    @pl.when(blk == 0)
    def _():
        st_scr[...] = s0_ref[0, 0]

    log_lb = lb_ref[0, 0:1, :]
    log_1m_lb = lb_ref[0, 1:2, :]
    c = CHUNK
    t = lax.broadcasted_iota(jnp.int32, (c, c), 0)
    s = lax.broadcasted_iota(jnp.int32, (c, c), 1)

    def chunk(ci, carry):
        cc = (n_chunks - 1 - ci) if rev else ci
        sl = pl.ds(pl.multiple_of(cc * c, c), c)
        q = _silu(q_ref[sl, :])
        zf = f_ref[sl, :]
        log_sig = jnp.minimum(zf, 0.0) - jnp.log(1.0 + jnp.exp(-jnp.abs(zf)))
        bb = log_1m_lb + log_sig
        lf = jnp.maximum(log_lb, bb) + jnp.log(1.0 + jnp.exp(-jnp.abs(log_lb - bb)))
        k = 1.0 - jnp.exp(lf)
        v = i_ref[sl, :].astype(BF16)
        p, tot = lf, lf
        att = jnp.where(t == s, jnp.sum(q * k, axis=-1, keepdims=True), 0.0)
        m = 1
        while m < c:
            second = (t & m) != 0
            is_q = jnp.logical_not(second) if rev else second
            qh = jnp.where(is_q, q * jnp.exp(p), 0.0).astype(BF16)
            kh = jnp.where(is_q, 0.0, k * jnp.exp(tot - p)).astype(BF16)
            att = att + jnp.where((t ^ s) < 2 * m, _dot_nt(qh, kh), 0.0)
            t_up = pltpu.roll(tot, m, 0)
            t_dn = pltpu.roll(tot, c - m, 0)
            p = p + (jnp.where(second, 0.0, t_dn) if rev else jnp.where(second, t_up, 0.0))
            tot = tot + jnp.where(second, t_up, t_dn)
            m *= 2
        st = st_scr[...]
        o = _dot(att.astype(BF16), v) + _dot_nt((q * jnp.exp(p)).astype(BF16), st.astype(BF16))
        ku = (k * jnp.exp(tot - p)).astype(BF16)
        st_scr[...] = st * jnp.exp(tot[0:1, :]) + _dot_tn(v, ku)
        if epilogue:
            o = o + prev_ref[sl, :]
            o = o * lax.rsqrt(jnp.mean(o * o, axis=-1, keepdims=True) + NORM_EPS) * ng_ref[...]
            o = o * _silu(g_ref[sl, :])
        o_ref[sl, :] = o.astype(o_ref.dtype)
        return carry

    lax.fori_loop(0, n_chunks, chunk, 0)

    @pl.when(blk == n_blocks - 1)
    def _():
        sT_ref[0, 0] = st_scr[...]


def _hg_scan(proj, lb_tab, s0, *, rev, row0, n_batch, seq, block_rows, col_q, col_f, col_i, col_g=None, prev=None,
             norm_g=None):
    n_heads = lb_tab.shape[0]
    dk = HG_KEY_DIM
    n_blocks = seq // block_rows
    blk0 = row0 // block_rows
    epilogue = prev is not None

    def rows_of(b, i):
        return blk0 + b * n_blocks + ((n_blocks - 1 - i) if rev else i)

    def local_rows(b, i):
        return b * n_blocks + ((n_blocks - 1 - i) if rev else i)

    pspec = lambda col: pl.BlockSpec((block_rows, dk), lambda b, h, i: (rows_of(b, i), col + h))
    in_specs = [pspec(col_q), pspec(col_f), pspec(col_i),
                pl.BlockSpec((1, 2, dk), lambda b, h, i: (h, 0, 0)),
                pl.BlockSpec((1, 1, dk, dk), lambda b, h, i: (b, h, 0, 0))]
    args = [proj, proj, proj, lb_tab, s0]
    if epilogue:
        in_specs += [pspec(col_g),
                     pl.BlockSpec((block_rows, dk), lambda b, h, i: (local_rows(b, i), h)),
                     pl.BlockSpec((1, dk), lambda b, h, i: (0, 0))]
        args += [proj, prev, norm_g.reshape(1, dk)]
    return pl.pallas_call(
        functools.partial(_hg_kernel, rev=rev, n_chunks=block_rows // CHUNK, n_blocks=n_blocks, epilogue=epilogue),
        grid=(n_batch, n_heads, n_blocks),
        in_specs=in_specs,
        out_specs=[pl.BlockSpec((block_rows, dk), lambda b, h, i: (local_rows(b, i), h)),
                   pl.BlockSpec((1, 1, dk, dk), lambda b, h, i: (b, h, 0, 0))],
        out_shape=[jax.ShapeDtypeStruct((n_batch * seq, n_heads * dk), BF16 if epilogue else F32),
                   jax.ShapeDtypeStruct((n_batch, n_heads, dk, dk), F32)],
        scratch_shapes=[pltpu.VMEM((dk, dk), F32)],
        compiler_params=_cparams("parallel", "parallel", "arbitrary"),
        name="hgrn2_scan_rev" if rev else "hgrn2_scan_fwd",
    )(*args)


def _rope(x, cos, sin):
    half = LANES // 2
    swapped = jnp.concatenate([pltpu.roll(x[:, g * LANES:(g + 1) * LANES], half, 1) for g in range(x.shape[1] // LANES)],
                              axis=1)
    return x * cos + swapped * sin


def _ret_kernel(*refs, fwd, rope, n_chunks, n_blocks):
    refs = list(refs)
    q_ref, k_ref, v_ref = refs[:3]
    pos = 3
    if rope:
        cos_ref, sin_ref = refs[pos], refs[pos + 1]
        pos += 2
    lam_ref, s0_ref = refs[pos], refs[pos + 1]
    pos += 2
    if fwd:
        g_ref, prev_ref = refs[pos], refs[pos + 1]
        pos += 2
    o_ref, sT_ref, st_scr = refs[pos], refs[pos + 1], refs[pos + 2]
    blk = pl.program_id(2)

    @pl.when(blk == 0)
    def _():
        st_scr[...] = s0_ref[0, 0]

    c = CHUNK
    dk = q_ref.shape[1]
    lam_f = lam_ref[0, 0:1, :]
    lam_b = lam_ref[0, 1:2, :]
    lam = lam_f if fwd else lam_b
    ipos = lax.broadcasted_iota(jnp.int32, (c, dk), 0).astype(F32)
    if fwd:
        q_decay = jnp.exp((ipos + 1.0) * lam)
        k_decay = jnp.exp((c - 1.0 - ipos) * lam)
        dist = (lax.broadcasted_iota(jnp.int32, (c, c), 0) - lax.broadcasted_iota(jnp.int32, (c, c), 1)).astype(F32)
        decay = (jnp.where(dist >= 0, jnp.exp(jnp.maximum(dist, 0.0) * lam_f[:, :c]), 0.0)
                 + jnp.where(dist <= 0, jnp.exp(jnp.maximum(-dist, 0.0) * lam_b[:, :c]), 0.0))
    else:
        q_decay = jnp.exp((c - ipos) * lam)
        k_decay = jnp.exp(ipos * lam)
    chunk_decay = jnp.exp(float(c) * lam[:, 0:1])
    k_scale = dk ** -0.5

    def chunk(ci, carry):
        cc = ci if fwd else (n_chunks - 1 - ci)
        sl = pl.ds(pl.multiple_of(cc * c, c), c)
        q = q_ref[sl, :]
        k = k_ref[sl, :]
        if rope:
            q = _rope(q, cos_ref[sl, :], sin_ref[sl, :])
            k = _rope(k, cos_ref[sl, :], sin_ref[sl, :])
        k = k * k_scale
        v = v_ref[sl, :]
        st = st_scr[...]
        o = _dot((q * q_decay).astype(BF16), st.astype(BF16))
        st_scr[...] = st * chunk_decay + _dot_tn((k * k_decay).astype(BF16), v)
        if fwd:
            att = _dot_nt(q.astype(BF16), k.astype(BF16)) * decay
            o = o + _dot(att.astype(BF16), v) + prev_ref[sl, :]
            mu = jnp.mean(o, axis=-1, keepdims=True)
            oc = o - mu
            var = jnp.mean(oc * oc, axis=-1, keepdims=True)
            o = oc * lax.rsqrt(var + LN_EPS) * _silu(g_ref[sl, :])
        o_ref[sl, :] = o.astype(o_ref.dtype)
        return carry

    lax.fori_loop(0, n_chunks, chunk, 0)

    @pl.when(blk == n_blocks - 1)
    def _():
        sT_ref[0, 0] = st_scr[...]


def _ret_scan(qk, v, lam, s0, *, fwd, row0, n_batch, seq, block_rows, n_heads, rope=None, gate=None, prev=None):
    dk, dv = RET_QK_DIM, RET_V_DIM
    n_blocks = seq // block_rows
    blk0 = row0 // block_rows

    def blk_of(i):
        return i if fwd else (n_blocks - 1 - i)

    grow = lambda b, i: blk0 + b * n_blocks + blk_of(i)
    lrow = lambda b, i: b * n_blocks + blk_of(i)
    in_specs = [pl.BlockSpec((block_rows, dk), lambda b, h, i: (grow(b, i), h)),
                pl.BlockSpec((block_rows, dk), lambda b, h, i: (grow(b, i), n_heads + h)),
                pl.BlockSpec((block_rows, dv), lambda b, h, i: (grow(b, i), h))]
    args = [qk, qk, v]
    if rope is not None:
        in_specs += [pl.BlockSpec((block_rows, dk), lambda b, h, i: (blk_of(i), 0))] * 2
        args += list(rope)
    in_specs += [pl.BlockSpec((1, 2, dk), lambda b, h, i: (h, 0, 0)),
                 pl.BlockSpec((1, 1, dk, dv), lambda b, h, i: (b, h, 0, 0))]
    args += [lam, s0]
    if fwd:
        in_specs += [pl.BlockSpec((block_rows, dv), lambda b, h, i: (grow(b, i), h)),
                     pl.BlockSpec((block_rows, dv), lambda b, h, i: (lrow(b, i), h))]
        args += [gate, prev]
    return pl.pallas_call(
        functools.partial(_ret_kernel, fwd=fwd, rope=rope is not None, n_chunks=block_rows // CHUNK, n_blocks=n_blocks),
        grid=(n_batch, n_heads, n_blocks),
        in_specs=in_specs,
        out_specs=[pl.BlockSpec((block_rows, dv), lambda b, h, i: (lrow(b, i), h)),
                   pl.BlockSpec((1, 1, dk, dv), lambda b, h, i: (b, h, 0, 0))],
        out_shape=[jax.ShapeDtypeStruct((n_batch * seq, n_heads * dv), BF16 if fwd else F32),
                   jax.ShapeDtypeStruct((n_batch, n_heads, dk, dv), F32)],
        scratch_shapes=[pltpu.VMEM((dk, dv), F32)],
        compiler_params=_cparams("parallel", "parallel", "arbitrary"),
        name="retention_fwd" if fwd else "retention_rev",
    )(*args)


def _rope_tables(seq, gw):
    half = LANES // 2
    inv = ROPE_BASE ** (-jnp.arange(half, dtype=F32) / half)
    t = jnp.arange(seq)
    ang_r = (t // gw).astype(F32)[:, None] * inv
    ang_c = (t % gw).astype(F32)[:, None] * inv
    cos = jnp.concatenate([jnp.cos(ang_r)] * 2 + [jnp.cos(ang_c)] * 2, axis=-1)
    sin = jnp.concatenate([-jnp.sin(ang_r), jnp.sin(ang_r), -jnp.sin(ang_c), jnp.sin(ang_c)], axis=-1)
    return cos, sin


def _even_mixer(h, w_in, w_out, rpb, lb, norm_g, *, n_batch, seq, ctx_len, ctx_out):
    n_x = n_batch * seq
    na_w = rpb.shape[0] * NA_HEAD_DIM
    n_heads = lb.shape[1] // HG_KEY_DIM
    w_in = w_in.astype(BF16)
    proj_na = _matmul(h, w_in, col0=0, n_cols=3 * na_w, out_dtype=BF16)
    proj_hg = _matmul(h, w_in, col0=3 * na_w, out_dtype=F32)
    na = _neighbourhood_attention(proj_na, rpb, n_batch=n_batch, seq=seq, ctx_len=ctx_len, ctx_out=ctx_out)
    lb_tab = jnp.stack([jnp.log(lb), jnp.log1p(-lb)], axis=1).reshape(2, 2, n_heads, HG_KEY_DIM).transpose(0, 2, 1, 3)
    zeros = jnp.zeros((n_batch, n_heads, HG_KEY_DIM, HG_KEY_DIM), F32)
    cols = dict(col_q=0, col_i=3 * n_heads)
    ctx_kw = dict(row0=n_x, n_batch=n_batch, seq=ctx_len, block_rows=ctx_len)
    x_kw = dict(row0=0, n_batch=n_batch, seq=seq, block_rows=ROW_TILE)
    oc_f, sc_f = _hg_scan(proj_hg, lb_tab[0], zeros, rev=False, col_f=n_heads, **cols, **ctx_kw)
    epi = dict(col_g=4 * n_heads, norm_g=norm_g)
    gc, sc_b = _hg_scan(proj_hg, lb_tab[1], zeros, rev=True, col_f=2 * n_heads, prev=oc_f, **epi, **cols, **ctx_kw)
    ox_f, _ = _hg_scan(proj_hg, lb_tab[0], sc_f, rev=False, col_f=n_heads, **cols, **x_kw)
    gx, _ = _hg_scan(proj_hg, lb_tab[1], sc_b, rev=True, col_f=2 * n_heads, prev=ox_f, **epi, **cols, **x_kw)
    if ctx_out:
        mix = jnp.concatenate([jnp.concatenate([na[0], gx], axis=1), jnp.concatenate([na[1], gc], axis=1)], axis=0)
    else:
        mix = jnp.concatenate([na[0], gx], axis=1)
    return _matmul(mix, w_out.astype(BF16), out_dtype=F32)


def _odd_mixer(h, w_in, w_out, log_decay, rope, *, n_batch, seq, ctx_len, ctx_out):
    n_x = n_batch * seq
    n_heads = log_decay.shape[1]
    qk_w = 2 * n_heads * RET_QK_DIM
    v_w = n_heads * RET_V_DIM
    w_in = w_in.astype(BF16)
    qk = _matmul(h, w_in, col0=0, n_cols=qk_w, out_dtype=F32)
    v = _matmul(h, w_in, col0=qk_w, n_cols=v_w, out_dtype=BF16)
    g = _matmul(h, w_in, col0=qk_w + v_w, n_cols=v_w, out_dtype=F32)
    lam = jnp.broadcast_to(log_decay.astype(F32).T[:, :, None], (n_heads, 2, RET_QK_DIM))
    zeros = jnp.zeros((n_batch, n_heads, RET_QK_DIM, RET_V_DIM), F32)
    ctx_kw = dict(row0=n_x, n_batch=n_batch, seq=ctx_len, block_rows=ctx_len, n_heads=n_heads)
    x_kw = dict(row0=0, n_batch=n_batch, seq=seq, block_rows=ROW_TILE, n_heads=n_heads, rope=rope)
    oc_b, sc_b = _ret_scan(qk, v, lam, zeros, fwd=False, **ctx_kw)
    yc, sc_f = _ret_scan(qk, v, lam, zeros, fwd=True, gate=g, prev=oc_b, **ctx_kw)
    ox_b, _ = _ret_scan(qk, v, lam, sc_b, fwd=False, **x_kw)
    yx, _ = _ret_scan(qk, v, lam, sc_f, fwd=True, gate=g, prev=ox_b, **x_kw)
    mix = jnp.concatenate([yx, yc], axis=0) if ctx_out else yx
    return _matmul(mix, w_out.astype(BF16), out_dtype=F32)


def kernel(x, c, ctx, c_ctx, ada_w, ada_b, ln_g, ln_b, e_w_in, e_w_out, na_rpb, hg_lb_logits, hg_norm_g, ffn_w1, ffn_w3,
           ffn_w2, o_w_in, o_w_out, ret_log_decay, router_w, router_b, moe_w1, moe_w3, moe_w2):
    n_batch, seq, d = x.shape
    ctx_len = ctx.shape[1]
    depth = ada_w.shape[0]
    n_x = n_batch * seq
    alpha = (2 * depth) ** 0.25
    dims = dict(n_batch=n_batch, seq=seq, ctx_len=ctx_len)

    lb_cum = jnp.cumsum(jax.nn.softmax(hg_lb_logits.astype(F32), axis=1), axis=1)
    lower_bounds = lb_cum - lb_cum[:, :1]
    rope = _rope_tables(seq, GRID_W)

    cond = jnp.concatenate([c, c_ctx[None, :], jnp.zeros((8 - n_batch - 1, d), F32)], axis=0)
    mods = [_matmul(cond, ada_w[l], bias=ada_b[l], silu_a=True, tm=8, tn=1024).reshape(8, 6, d)[:n_batch + 1]
            for l in range(depth)]

    tok = jnp.concatenate([x.reshape(n_x, d), ctx.reshape(n_batch * ctx_len, d)], axis=0)
    h = _modulate(tok, mods[0][:, 0:2], rows_per_batch=seq)
    for layer in range(depth):
        j = layer // 2
        last = layer == depth - 1
        mod = mods[layer]
        n_rows = n_x if last else tok.shape[0]
        if layer % 2 == 0:
            y = _even_mixer(h, e_w_in[j], e_w_out[j], na_rpb[j], lower_bounds[:, j], hg_norm_g[j], ctx_out=not last, **dims)
            router = None
        else:
            y = _odd_mixer(h, o_w_in[j], o_w_out[j], ret_log_decay[j], rope, ctx_out=not last, **dims)
            rw = jnp.zeros((d, LANES), F32).at[:, :N_EXPERTS].set(router_w[j])
            rb = jnp.zeros((1, LANES), F32).at[0, :N_EXPERTS].set(router_b[j])
            router = (rw, rb)
        lnp = lambda i: jnp.stack([ln_g[layer, i], ln_b[layer, i]])
        res = _residual_ln(tok, [y], mod[:, 2:5], lnp(0), alpha=alpha, rows_per_batch=seq, n_rows=n_rows, router=router)
        tok, h = res[0], res[1]
        nxt = mods[layer + 1][:, 0:2] if not last else jnp.stack([jnp.zeros_like(mod[:, 0])] * 2, axis=1)
        mod2 = jnp.concatenate([mod[:, 5:6], nxt], axis=1)
        if layer % 2 == 0:
            ones = jnp.ones((n_rows // ROW_TILE,), jnp.int32)
            y = _grouped_swiglu(h, ffn_w1[j:j + 1].astype(BF16), ffn_w3[j:j + 1].astype(BF16),
                                ffn_w2[j:j + 1].astype(BF16), 0 * ones, ones, tm=ROW_TILE,
                                tf=_pick_tile(ffn_w1.shape[2], 1408))
            res = _residual_ln(tok, [y], mod2, lnp(1), alpha=alpha, rows_per_batch=seq)
        else:
            gates, idx = res[2], res[3]
            slot_token, assign_slot, tile_expert, tile_valid = _route(idx[:, :2], N_EXPERTS, ROW_TILE)
            xs = jnp.take(h, slot_token, axis=0)
            ys = _grouped_swiglu(xs, moe_w1[j].astype(BF16), moe_w3[j].astype(BF16), moe_w2[j].astype(BF16),
                                 tile_expert, tile_valid, tm=ROW_TILE, tf=_pick_tile(moe_w1.shape[3], 512))
            y1 = jnp.take(ys, assign_slot[:, 0], axis=0)
            y2 = jnp.take(ys, assign_slot[:, 1], axis=0)
            res = _residual_ln(tok, [y1, y2], mod2, lnp(1), alpha=alpha, rows_per_batch=seq, y_gates=gates)
        tok, h = res[0], res[1]
    return tok[:n_x].reshape(n_batch, seq, d)
```

```python
import functools
import math

import jax
import jax.numpy as jnp
from jax import lax
from jax.experimental import pallas as pl
from jax.experimental.pallas import tpu as pltpu

F32 = jnp.float32
BF16 = jnp.bfloat16

GRID_W = 64
NA_HEAD_DIM = 64
NA_KH = 8
NA_KW = 16
HG_KEY_DIM = 128
RET_QK_DIM = 256
RET_V_DIM = 512
N_EXPERTS = 8
ROPE_BASE = 10000.0
LN_EPS = 1e-5
NORM_EPS = 1e-6
LANES = 128
HG_CHUNK = 128
RET_CHUNK = 256
ROW_TILE = 512
PROJ_TILE = 1024
SCAN_BLOCK = 1024
VMEM_LIMIT = 48 * 1024 * 1024
MASK_VALUE = -1e30


def _cparams(*sem):
    return pltpu.CompilerParams(dimension_semantics=sem, vmem_limit_bytes=VMEM_LIMIT)


def _dot(a, b):
    return jnp.dot(a, b, preferred_element_type=F32)


def _dot_nt(a, b):
    return lax.dot_general(a, b, (((1,), (1,)), ((), ())), preferred_element_type=F32)


def _dot_tn(a, b):
    return lax.dot_general(a, b, (((0,), (0,)), ((), ())), preferred_element_type=F32)


def _silu(x):
    return x * jax.nn.sigmoid(x)


def _pick_tile(n, target, unit=LANES):
    best = None
    for t in range(unit, min(n, target) + 1, unit):
        if n % t == 0:
            best = t
    assert best is not None, (n, target, unit)
    return best


def _rope(x, cos, sin):
    half = LANES // 2
    swapped = jnp.concatenate([pltpu.roll(x[:, g * LANES:(g + 1) * LANES], half, 1) for g in range(x.shape[1] // LANES)],
                              axis=1)
    return x * cos + swapped * sin


def _mm_kernel(a_ref, w_ref, *rest, silu_a, epilogue):
    a = a_ref[...]
    if silu_a:
        a = _silu(a.astype(F32))
    acc = _dot(a.astype(BF16), w_ref[...].astype(BF16))
    if epilogue == "bias":
        b_ref, o_ref = rest
        o_ref[...] = (acc + b_ref[...]).astype(o_ref.dtype)
    elif epilogue == "silu":
        (o_ref,) = rest
        o_ref[...] = _silu(acc).astype(o_ref.dtype)
    elif epilogue == "log_forget":
        lb_ref, o_ref = rest
        log_sig = jnp.minimum(acc, 0.0) - jnp.log(1.0 + jnp.exp(-jnp.abs(acc)))
        log_lb = lb_ref[0:1, :]
        bb = lb_ref[1:2, :] + log_sig
        o_ref[...] = jnp.maximum(log_lb, bb) + jnp.log(1.0 + jnp.exp(-jnp.abs(log_lb - bb)))
    elif epilogue in ("rope", "ret_k"):
        cos_ref, sin_ref = rest[0], rest[1]
        reps = acc.shape[1] // cos_ref.shape[1]
        cos = jnp.concatenate([cos_ref[...]] * reps, axis=1)
        sin = jnp.concatenate([sin_ref[...]] * reps, axis=1)
        r = _rope(acc, cos, sin)
        if epilogue == "rope":
            rest[2][...] = r.astype(rest[2].dtype)
        else:
            lam_ref, k_ref, kf_ref, kb_ref = rest[2:]
            k = r * (RET_QK_DIM ** -0.5)
            ipos = (lax.broadcasted_iota(jnp.int32, k.shape, 0) & (RET_CHUNK - 1)).astype(F32)
            k_ref[...] = k.astype(k_ref.dtype)
            kf_ref[...] = (k * jnp.exp((RET_CHUNK - 1.0 - ipos) * lam_ref[0:1, :])).astype(kf_ref.dtype)
            kb_ref[...] = (k * jnp.exp(ipos * lam_ref[1:2, :])).astype(kb_ref.dtype)
    else:
        (o_ref,) = rest
        o_ref[...] = acc.astype(o_ref.dtype)


def _matmul(a, w, *, col0=0, n_cols=None, n_rows=None, tm=PROJ_TILE, tn=512, out_dtype=F32, silu_a=False, epilogue=None,
            bias=None, col_rows=None, rope=None, n_out=1):
    k = a.shape[1]
    n_rows = a.shape[0] if n_rows is None else n_rows
    n_cols = w.shape[1] - col0 if n_cols is None else n_cols
    tn = _pick_tile(math.gcd(n_cols, col0) if col0 else n_cols, tn)
    assert n_rows % tm == 0 and n_cols % tn == 0 and col0 % tn == 0
    cb = col0 // tn
    in_specs = [pl.BlockSpec((tm, k), lambda j, i: (i, 0)),
                pl.BlockSpec((k, tn), lambda j, i: (0, cb + j))]
    args = [a, w]
    if rope is not None:
        cos, sin, block_fn = rope
        in_specs += [pl.BlockSpec((tm, cos.shape[1]), lambda j, i: (block_fn(i), 0))] * 2
        args += [cos, sin]
    if bias is not None:
        col_rows = bias.reshape(1, -1)
    if col_rows is not None:
        in_specs.append(pl.BlockSpec((col_rows.shape[0], tn), lambda j, i: (0, j)))
        args.append(col_rows)
    out_spec = pl.BlockSpec((tm, tn), lambda j, i: (i, j))
    out_sds = jax.ShapeDtypeStruct((n_rows, n_cols), out_dtype)
    return pl.pallas_call(
        functools.partial(_mm_kernel, silu_a=silu_a, epilogue="bias" if bias is not None else epilogue),
        grid=(n_cols // tn, n_rows // tm),
        in_specs=in_specs,
        out_specs=out_spec if n_out == 1 else [out_spec] * n_out,
        out_shape=out_sds if n_out == 1 else [out_sds] * n_out,
        compiler_params=_cparams("parallel", "parallel"),
        name="matmul" if epilogue is None else "matmul_" + epilogue,
    )(*args)


def _modulate_kernel(x_ref, mod_ref, h_ref):
    m = mod_ref[0]
    h_ref[...] = (x_ref[...] * (1.0 + m[1:2]) + m[0:1]).astype(h_ref.dtype)


def _mod_index(tiles_per_batch, n_batch):
    return lambda i: (jnp.minimum(i // tiles_per_batch, n_batch), 0, 0)


def _modulate(x, mod, *, rows_per_batch, tm=ROW_TILE):
    m, d = x.shape
    return pl.pallas_call(
        _modulate_kernel,
        grid=(m // tm,),
        in_specs=[pl.BlockSpec((tm, d), lambda i: (i, 0)),
                  pl.BlockSpec((1, 2, d), _mod_index(rows_per_batch // tm, mod.shape[0] - 1))],
        out_specs=pl.BlockSpec((tm, d), lambda i: (i, 0)),
        out_shape=jax.ShapeDtypeStruct((m, d), BF16),
        compiler_params=_cparams("parallel"),
        name="modulate",
    )(x, mod)


def _ln_kernel(*refs, alpha, two_y, router, n_experts):
    x_ref, y_ref = refs[0], refs[1]
    pos = 2
    if two_y:
        yg_ref = refs[pos]
        pos += 1
    mod_ref, lnp_ref = refs[pos], refs[pos + 1]
    pos += 2
    if router:
        rw_ref, rb_ref = refs[pos], refs[pos + 1]
        pos += 2
    xo_ref, h_ref = refs[pos], refs[pos + 1]
    m = mod_ref[0]
    d = x_ref.shape[1]
    if two_y:
        yg = yg_ref[...]
        y = yg[:, 0:1] * y_ref[:, :d].astype(F32) + yg[:, 1:2] * y_ref[:, d:].astype(F32)
    else:
        y = y_ref[...].astype(F32)
    z = alpha * x_ref[...] + m[0:1] * y
    mu = jnp.mean(z, axis=-1, keepdims=True)
    zc = z - mu
    var = jnp.mean(zc * zc, axis=-1, keepdims=True)
    xn = zc * lax.rsqrt(var + LN_EPS) * lnp_ref[0:1, :] + lnp_ref[1:2, :]
    xo_ref[...] = xn
    hf = xn * (1.0 + m[2:3]) + m[1:2]
    h_ref[...] = hf.astype(h_ref.dtype)
    if router:
        g_ref, i_ref = refs[pos + 2], refs[pos + 3]
        logits = jnp.dot(hf, rw_ref[...], preferred_element_type=F32, precision=lax.Precision.HIGHEST) + rb_ref[...]
        lane = lax.broadcasted_iota(jnp.int32, logits.shape, 1).astype(F32)
        neg = -jnp.inf
        lg = jnp.where(lane < n_experts, logits, neg)
        m1 = jnp.max(lg, axis=-1, keepdims=True)
        i1 = jnp.min(jnp.where(lg == m1, lane, float(LANES)), axis=-1, keepdims=True)
        lg2 = jnp.where(lane == i1, neg, lg)
        m2 = jnp.max(lg2, axis=-1, keepdims=True)
        i2 = jnp.min(jnp.where(lg2 == m2, lane, float(LANES)), axis=-1, keepdims=True)
        e2 = jnp.exp(m2 - m1)
        den = 1.0 + e2
        g_ref[...] = jnp.where(lane == 0, 1.0 / den, jnp.where(lane == 1, e2 / den, 0.0))
        i_ref[...] = jnp.where(lane == 0, i1, jnp.where(lane == 1, i2, 0.0)).astype(jnp.int32)


def _residual_ln(x, y, mod, lnp, *, alpha, rows_per_batch, n_rows=None, y_gates=None, router=None, tm=ROW_TILE):
    d = x.shape[1]
    n_rows = x.shape[0] if n_rows is None else n_rows
    row = pl.BlockSpec((tm, d), lambda i: (i, 0))
    lane_row = pl.BlockSpec((tm, LANES), lambda i: (i, 0))
    two_y = y_gates is not None
    in_specs = [row, pl.BlockSpec((tm, y.shape[1]), lambda i: (i, 0))]
    args = [x, y]
    if two_y:
        in_specs.append(lane_row)
        args.append(y_gates)
    in_specs += [pl.BlockSpec((1, 3, d), _mod_index(rows_per_batch // tm, mod.shape[0] - 1)),
                 pl.BlockSpec((2, d), lambda i: (0, 0))]
    args += [mod, lnp]
    out_specs = [row, row]
    out_shape = [jax.ShapeDtypeStruct((n_rows, d), F32), jax.ShapeDtypeStruct((n_rows, d), BF16 if router is None else F32)]
    if router is not None:
        in_specs += [pl.BlockSpec((d, LANES), lambda i: (0, 0)), pl.BlockSpec((1, LANES), lambda i: (0, 0))]
        args += list(router)
        out_specs += [lane_row, lane_row]
        out_shape += [jax.ShapeDtypeStruct((n_rows, LANES), F32), jax.ShapeDtypeStruct((n_rows, LANES), jnp.int32)]
    return pl.pallas_call(
        functools.partial(_ln_kernel, alpha=alpha, two_y=two_y, router=router is not None, n_experts=N_EXPERTS),
        grid=(n_rows // tm,),
        in_specs=in_specs,
        out_specs=out_specs,
        out_shape=out_shape,
        compiler_params=_cparams("parallel"),
        name="residual_ln",
    )(*args)


def _gffn_kernel(te_ref, tv_ref, x_ref, w1_ref, w3_ref, w2_ref, o_ref):
    i, f = pl.program_id(0), pl.program_id(1)

    @pl.when(f == 0)
    def _():
        o_ref[...] = jnp.zeros_like(o_ref)

    @pl.when(tv_ref[i] != 0)
    def _():
        x = x_ref[...].astype(BF16)
        h1 = _dot(x, w1_ref[0])
        h3 = _dot(x, w3_ref[0])
        a = (_silu(h1) * h3).astype(BF16)
        o_ref[...] += _dot(a, w2_ref[0])


def _grouped_swiglu(xs, w1, w3, w2, tile_expert, tile_valid, *, tm, tf):
    p, d = xs.shape
    _, _, f = w1.shape
    assert p % tm == 0 and f % tf == 0
    grid_spec = pltpu.PrefetchScalarGridSpec(
        num_scalar_prefetch=2,
        grid=(p // tm, f // tf),
        in_specs=[pl.BlockSpec((tm, d), lambda i, j, te, tv: (i, 0)),
                  pl.BlockSpec((1, d, tf), lambda i, j, te, tv: (te[i], 0, j * tv[i])),
                  pl.BlockSpec((1, d, tf), lambda i, j, te, tv: (te[i], 0, j * tv[i])),
                  pl.BlockSpec((1, tf, d), lambda i, j, te, tv: (te[i], j * tv[i], 0))],
        out_specs=pl.BlockSpec((tm, d), lambda i, j, te, tv: (i, 0)),
    )
    return pl.pallas_call(
        _gffn_kernel,
        grid_spec=grid_spec,
        out_shape=jax.ShapeDtypeStruct((p, d), F32),
        compiler_params=_cparams("parallel", "arbitrary"),
        name="grouped_swiglu",
    )(tile_expert, tile_valid, xs, w1, w3, w2)


def _route(idx, n_experts, tm):
    m = idx.shape[0]
    flat = idx.reshape(-1)
    n = flat.shape[0]
    p = -(-(n + n_experts * (tm - 1)) // tm) * tm
    onehot = (flat[:, None] == jnp.arange(n_experts, dtype=jnp.int32)[None, :]).astype(jnp.int32)
    csum = jnp.cumsum(onehot, axis=0)
    counts = csum[-1]
    padded = -(-counts // tm) * tm
    pad_end = jnp.cumsum(padded)
    pad_off = pad_end - padded
    rank = jnp.sum(csum * onehot, axis=1) - 1
    assign_slot = (pad_off[flat] + rank).reshape(m, 2)
    slot = jnp.arange(p, dtype=jnp.int32)
    slot_expert = jnp.minimum(jnp.sum(pad_end[None, :] <= slot[:, None], axis=1), n_experts - 1).astype(jnp.int32)
    slot_rank = slot - pad_off[slot_expert]
    csum_t = csum.T
    assign_of_slot = jax.vmap(lambda col, r: jnp.searchsorted(col, r, side="left"))(
        csum_t, (slot_rank + 1).reshape(-1)[None, :].repeat(n_experts, 0))
    src = jnp.take_along_axis(assign_of_slot, slot_expert[None, :], axis=0)[0]
    slot_token = jnp.where(slot_rank < counts[slot_expert], jnp.minimum(src, n - 1) // 2, 0).astype(jnp.int32)
    tile_start = slot[::tm]
    tile_valid = (tile_start < pad_end[-1]).astype(jnp.int32)
    tile_expert = slot_expert[::tm]
    return slot_token, assign_slot, tile_expert, tile_valid


def _softmax_pv(parts):
    mx = functools.reduce(jnp.maximum, [jnp.max(s, axis=-1, keepdims=True) for s, _ in parts])
    ps = [jnp.exp(s - mx) for s, _ in parts]
    den = functools.reduce(jnp.add, [jnp.sum(p, axis=-1, keepdims=True) for p in ps])
    num = functools.reduce(jnp.add, [_dot(p.astype(BF16), v) for p, (_, v) in zip(ps, parts)])
    return num / den


def _na_kernel(*refs, rows, gw, ctx_out):
    if ctx_out:
        q_ref, k_ref, v_ref, kc_ref, vc_ref, qc_ref, bias_ref, o_ref, oc_ref = refs
    else:
        q_ref, k_ref, v_ref, kc_ref, vc_ref, bias_ref, o_ref = refs
    lane = lax.broadcasted_iota(jnp.int32, (1, LANES), 1)
    head_lanes = [lane < NA_HEAD_DIM, lane >= NA_HEAD_DIM]
    scale = NA_HEAD_DIM ** -0.5
    kc = kc_ref[...]
    vc = vc_ref[...]
    band = NA_KH * gw

    def body(r, carry):
        b0 = jnp.clip(r - NA_KH // 2, 0, rows - NA_KH)
        dr0 = b0 - r + NA_KH - 1
        q = q_ref[pl.ds(pl.multiple_of(r * gw, gw), gw), :]
        ks = pl.ds(pl.multiple_of(b0 * gw, gw), band)
        kb = k_ref[ks, :]
        vb = v_ref[ks, :]
        outs = []
        for h in range(2):
            qh = jnp.where(head_lanes[h], q, 0) * scale
            s = _dot_nt(qh, kb) + bias_ref[h, dr0]
            sc = _dot_nt(qh, kc)
            outs.append(_softmax_pv([(s, vb), (sc, vc)]))
        o_ref[pl.ds(pl.multiple_of(r * gw, gw), gw), :] = jnp.where(head_lanes[0], outs[0], outs[1]).astype(o_ref.dtype)
        return carry

    lax.fori_loop(0, rows, body, 0, unroll=2)
    if ctx_out:
        qc = qc_ref[...]
        outs = []
        for h in range(2):
            qh = jnp.where(head_lanes[h], qc, 0) * scale
            outs.append(_softmax_pv([(_dot_nt(qh, kc), vc)]))
        oc_ref[...] = jnp.where(head_lanes[0], outs[0], outs[1]).astype(oc_ref.dtype)


def _na_bias_table(rpb, gw):
    col = jnp.arange(gw)
    c_start = jnp.clip(col - NA_KW // 2, 0, gw - NA_KW)
    col_in = (col[None, :] >= c_start[:, None]) & (col[None, :] < c_start[:, None] + NA_KW)
    dc = jnp.clip(col[None, :] - col[:, None], 1 - NA_KW, NA_KW - 1) + NA_KW - 1
    t = jnp.where(col_in[None, None], rpb[:, :, dc].astype(F32), MASK_VALUE)
    win = jnp.stack([t[:, d:d + NA_KH] for d in range(NA_KH)], axis=1)
    return win.transpose(0, 1, 3, 2, 4).reshape(rpb.shape[0], NA_KH, gw, NA_KH * gw)


def _neighbourhood_attention(proj, rpb, *, n_batch, seq, ctx_len, ctx_out):
    width = rpb.shape[0] * NA_HEAD_DIM
    pairs = width // LANES
    rows = seq // GRID_W
    bias = _na_bias_table(rpb, GRID_W)
    cblk0 = n_batch * seq // ctx_len
    xspec = lambda g: pl.BlockSpec((seq, LANES), lambda b, p: (b, g * pairs + p))
    cspec = lambda g: pl.BlockSpec((ctx_len, LANES), lambda b, p: (cblk0 + b, g * pairs + p))
    in_specs = [xspec(0), xspec(1), xspec(2), cspec(1), cspec(2)]
    args = [proj, proj, proj, proj, proj]
    if ctx_out:
        in_specs.append(cspec(0))
        args.append(proj)
    in_specs.append(pl.BlockSpec((2, NA_KH, GRID_W, NA_KH * GRID_W), lambda b, p: (p, 0, 0, 0)))
    args.append(bias)
    out_specs = [pl.BlockSpec((seq, LANES), lambda b, p: (b, p))]
    out_shape = [jax.ShapeDtypeStruct((n_batch * seq, width), BF16)]
    if ctx_out:
        out_specs.append(pl.BlockSpec((ctx_len, LANES), lambda b, p: (b, p)))
        out_shape.append(jax.ShapeDtypeStruct((n_batch * ctx_len, width), BF16))
    return pl.pallas_call(
        functools.partial(_na_kernel, rows=rows, gw=GRID_W, ctx_out=ctx_out),
        grid=(n_batch, pairs),
        in_specs=in_specs,
        out_specs=out_specs,
        out_shape=out_shape,
        compiler_params=_cparams("parallel", "parallel"),
        name="neighbourhood_attention",
    )(*args)


def _hg_kernel(*refs, rev, n_chunks, n_blocks, epilogue):
    if epilogue:
        q_ref, f_ref, i_ref, s0_ref, g_ref, prev_ref, ng_ref, o_ref, sT_ref, st_scr = refs
    else:
        q_ref, f_ref, i_ref, s0_ref, o_ref, sT_ref, st_scr = refs
    blk = pl.program_id(2)

    @pl.when(blk == 0)
    def _():
        st_scr[...] = s0_ref[0, 0]

    c = HG_CHUNK
    t = lax.broadcasted_iota(jnp.int32, (c, c), 0)
    s = lax.broadcasted_iota(jnp.int32, (c, c), 1)

    def chunk(ci, carry):
        cc = (n_chunks - 1 - ci) if rev else ci
        sl = pl.ds(pl.multiple_of(cc * c, c), c)
        q = q_ref[sl, :].astype(F32)
        lf = f_ref[sl, :]
        k = 1.0 - jnp.exp(lf)
        v = i_ref[sl, :]
        p, tot = lf, lf
        att = jnp.where(t == s, jnp.sum(q * k, axis=-1, keepdims=True), 0.0)
        m = 1
        while m < c:
            second = (t & m) != 0
            is_q = jnp.logical_not(second) if rev else second
            z = (jnp.where(is_q, q, k) * jnp.exp(jnp.where(is_q, p, tot - p))).astype(BF16)
            pair = jnp.logical_and((t ^ s) < 2 * m, ((s & m) == 0) if not rev else ((s & m) != 0))
            att = att + jnp.where(jnp.logical_and(pair, is_q), _dot_nt(z, z), 0.0)
            t_up = pltpu.roll(tot, m, 0)
            t_dn = pltpu.roll(tot, c - m, 0)
            p = p + (jnp.where(second, 0.0, t_dn) if rev else jnp.where(second, t_up, 0.0))
            tot = tot + jnp.where(second, t_up, t_dn)
            m *= 2
        st = st_scr[...]
        o = _dot(att.astype(BF16), v) + _dot_nt((q * jnp.exp(p)).astype(BF16), st.astype(BF16))
        ku = (k * jnp.exp(tot - p)).astype(BF16)
        st_scr[...] = st * jnp.exp(tot[0:1, :]) + _dot_tn(v, ku)
        if epilogue:
            o = o + prev_ref[sl, :]
            o = o * lax.rsqrt(jnp.mean(o * o, axis=-1, keepdims=True) + NORM_EPS) * ng_ref[...]
            o = o * g_ref[sl, :].astype(F32)
        o_ref[sl, :] = o.astype(o_ref.dtype)
        return carry

    lax.fori_loop(0, n_chunks, chunk, 0)

    @pl.when(blk == n_blocks - 1)
    def _():
        sT_ref[0, 0] = st_scr[...]


def _hg_scan(qg, lf, proj, s0, *, rev, row0, n_batch, seq, block_rows, n_heads, col_f, col_i, prev=None, norm_g=None):
    dk = HG_KEY_DIM
    n_blocks = seq // block_rows
    blk0 = row0 // block_rows
    epilogue = prev is not None

    def local_rows(b, i):
        return b * n_blocks + ((n_blocks - 1 - i) if rev else i)

    pspec = lambda col: pl.BlockSpec((block_rows, dk), lambda b, h, i: (blk0 + local_rows(b, i), col + h))
    state_spec = pl.BlockSpec((1, 1, dk, dk), lambda b, h, i: (b, h, 0, 0))
    local_spec = pl.BlockSpec((block_rows, dk), lambda b, h, i: (local_rows(b, i), h))
    in_specs = [pspec(0), pspec(col_f), pspec(col_i), state_spec]
    args = [qg, lf, proj, s0]
    if epilogue:
        in_specs += [pspec(n_heads), local_spec, pl.BlockSpec((1, dk), lambda b, h, i: (0, 0))]
        args += [qg, prev, norm_g.reshape(1, dk)]
    return pl.pallas_call(
        functools.partial(_hg_kernel, rev=rev, n_chunks=block_rows // HG_CHUNK, n_blocks=n_blocks, epilogue=epilogue),
        grid=(n_batch, n_heads, n_blocks),
        in_specs=in_specs,
        out_specs=[local_spec, state_spec],
        out_shape=[jax.ShapeDtypeStruct((n_batch * seq, n_heads * dk), BF16 if epilogue else F32),
                   jax.ShapeDtypeStruct((n_batch, n_heads, dk, dk), F32)],
        scratch_shapes=[pltpu.VMEM((dk, dk), F32)],
        compiler_params=_cparams("parallel", "parallel", "arbitrary"),
        name="hgrn2_scan_rev" if rev else "hgrn2_scan_fwd",
    )(*args)


def _ret_kernel(*refs, fwd, n_chunks, n_blocks):
    if fwd:
        q_ref, k_ref, kd_ref, v_ref, lam_ref, s0_ref, g_ref, prev_ref, o_ref, sT_ref, st_scr, qd_scr, dm_scr = refs
    else:
        q_ref, kd_ref, v_ref, lam_ref, s0_ref, o_ref, sT_ref, st_scr, qd_scr = refs
    blk = pl.program_id(2)
    c = RET_CHUNK
    lam_f = lam_ref[0, 0:1, :]
    lam_b = lam_ref[0, 1:2, :]
    lam = lam_f if fwd else lam_b

    @pl.when(blk == 0)
    def _():
        st_scr[...] = s0_ref[0, 0]
        ipos = lax.broadcasted_iota(jnp.int32, (c, LANES), 0).astype(F32)
        steps = (ipos + 1.0) if fwd else (c - ipos)
        qd_scr[...] = jnp.exp(steps * lam[:, :LANES])
        if fwd:
            dist = (lax.broadcasted_iota(jnp.int32, (c, c), 0) - lax.broadcasted_iota(jnp.int32, (c, c), 1)).astype(F32)
            dm_scr[...] = (jnp.where(dist >= 0, jnp.exp(jnp.maximum(dist, 0.0) * lam_f[:, :c]), 0.0)
                           + jnp.where(dist <= 0, jnp.exp(jnp.maximum(-dist, 0.0) * lam_b[:, :c]), 0.0))

    chunk_decay = jnp.exp(float(c) * lam[:, 0:1])

    def chunk(ci, carry):
        cc = ci if fwd else (n_chunks - 1 - ci)
        sl = pl.ds(pl.multiple_of(cc * c, c), c)
        q = q_ref[sl, :]
        v = v_ref[sl, :]
        st = st_scr[...]
        o = _dot(q, st.astype(BF16)) * qd_scr[:, 0:1]
        st_scr[...] = st * chunk_decay + _dot_tn(kd_ref[sl, :], v)
        if fwd:
            att = _dot_nt(q, k_ref[sl, :]) * dm_scr[...]
            o = o + _dot(att.astype(BF16), v) + prev_ref[sl, :].astype(F32)
            mu = jnp.mean(o, axis=-1, keepdims=True)
            oc = o - mu
            var = jnp.mean(oc * oc, axis=-1, keepdims=True)
            o = oc * lax.rsqrt(var + LN_EPS) * g_ref[sl, :].astype(F32)
        o_ref[sl, :] = o.astype(o_ref.dtype)
        return carry

    lax.fori_loop(0, n_chunks, chunk, 0)

    @pl.when(blk == n_blocks - 1)
    def _():
        sT_ref[0, 0] = st_scr[...]


def _ret_scan(q, k, kd, v, lam, s0, *, fwd, row0, n_batch, seq, block_rows, n_heads, gate=None, prev=None):
    dk, dv = RET_QK_DIM, RET_V_DIM
    n_blocks = seq // block_rows
    blk0 = row0 // block_rows
    assert RET_CHUNK <= dk

    def lrow(b, i):
        return b * n_blocks + (i if fwd else (n_blocks - 1 - i))

    kspec = pl.BlockSpec((block_rows, dk), lambda b, h, i: (blk0 + lrow(b, i), h))
    vspec = pl.BlockSpec((block_rows, dv), lambda b, h, i: (blk0 + lrow(b, i), h))
    local_spec = pl.BlockSpec((block_rows, dv), lambda b, h, i: (lrow(b, i), h))
    state_spec = pl.BlockSpec((1, 1, dk, dv), lambda b, h, i: (b, h, 0, 0))
    lam_spec = pl.BlockSpec((1, 2, dk), lambda b, h, i: (h, 0, 0))
    scratch = [pltpu.VMEM((dk, dv), F32), pltpu.VMEM((RET_CHUNK, LANES), F32)]
    if fwd:
        in_specs = [kspec, kspec, kspec, vspec, lam_spec, state_spec, vspec, local_spec]
        args = [q, k, kd, v, lam, s0, gate, prev]
        scratch.append(pltpu.VMEM((RET_CHUNK, RET_CHUNK), F32))
    else:
        in_specs = [kspec, kspec, vspec, lam_spec, state_spec]
        args = [q, kd, v, lam, s0]
    return pl.pallas_call(
        functools.partial(_ret_kernel, fwd=fwd, n_chunks=block_rows // RET_CHUNK, n_blocks=n_blocks),
        grid=(n_batch, n_heads, n_blocks),
        in_specs=in_specs,
        out_specs=[local_spec, state_spec],
        out_shape=[jax.ShapeDtypeStruct((n_batch * seq, n_heads * dv), BF16),
                   jax.ShapeDtypeStruct((n_batch, n_heads, dk, dv), F32)],
        scratch_shapes=scratch,
        compiler_params=_cparams("parallel", "parallel", "arbitrary"),
        name="retention_fwd" if fwd else "retention_rev",
    )(*args)


def _rope_tables(seq, gw, pad_rows):
    half = LANES // 2
    inv = ROPE_BASE ** (-jnp.arange(half, dtype=F32) / half)
    t = jnp.arange(seq)
    ang_r = (t // gw).astype(F32)[:, None] * inv
    ang_c = (t % gw).astype(F32)[:, None] * inv
    cos = jnp.concatenate([jnp.cos(ang_r)] * 2 + [jnp.cos(ang_c)] * 2, axis=-1)
    sin = jnp.concatenate([-jnp.sin(ang_r), jnp.sin(ang_r), -jnp.sin(ang_c), jnp.sin(ang_c)], axis=-1)
    cos = jnp.concatenate([cos, jnp.ones((pad_rows, cos.shape[1]), F32)], axis=0)
    sin = jnp.concatenate([sin, jnp.zeros((pad_rows, sin.shape[1]), F32)], axis=0)
    return cos, sin


def _even_mixer(h, w_in, w_out, rpb, lb, norm_g, *, n_batch, seq, ctx_len, ctx_out, tm):
    n_x = n_batch * seq
    na_w = rpb.shape[0] * NA_HEAD_DIM
    hg_w = lb.shape[1]
    n_heads = hg_w // HG_KEY_DIM
    cuts = [0, 3 * na_w] + [3 * na_w + i * hg_w for i in range(1, 6)]
    seg = lambda i: w_in[:, cuts[i]:cuts[i + 1]]
    w_plain = jnp.concatenate([seg(0), seg(4)], axis=1).astype(BF16)
    w_silu = jnp.concatenate([seg(1), seg(5)], axis=1).astype(BF16)
    w_lf = jnp.concatenate([seg(2), seg(3)], axis=1).astype(BF16)
    proj = _matmul(h, w_plain, tm=tm, out_dtype=BF16)
    qg = _matmul(h, w_silu, tm=tm, out_dtype=BF16, epilogue="silu")
    lb_rows = jnp.stack([jnp.log(lb).reshape(-1), jnp.log1p(-lb).reshape(-1)])
    lf = _matmul(h, w_lf, tm=tm, out_dtype=F32, epilogue="log_forget", col_rows=lb_rows)
    na = _neighbourhood_attention(proj, rpb, n_batch=n_batch, seq=seq, ctx_len=ctx_len, ctx_out=ctx_out)
    zeros = jnp.zeros((n_batch, n_heads, HG_KEY_DIM, HG_KEY_DIM), F32)
    col_i = 3 * na_w // HG_KEY_DIM
    ctx_kw = dict(row0=n_x, n_batch=n_batch, seq=ctx_len, block_rows=ctx_len, n_heads=n_heads, col_i=col_i)
    x_kw = dict(row0=0, n_batch=n_batch, seq=seq, block_rows=min(SCAN_BLOCK, seq), n_heads=n_heads, col_i=col_i)
    oc_f, sc_f = _hg_scan(qg, lf, proj, zeros, rev=False, col_f=0, **ctx_kw)
    gc, sc_b = _hg_scan(qg, lf, proj, zeros, rev=True, col_f=n_heads, prev=oc_f, norm_g=norm_g, **ctx_kw)
    ox_f, _ = _hg_scan(qg, lf, proj, sc_f, rev=False, col_f=0, **x_kw)
    gx, _ = _hg_scan(qg, lf, proj, sc_b, rev=True, col_f=n_heads, prev=ox_f, norm_g=norm_g, **x_kw)
    if ctx_out:
        mix = jnp.concatenate([jnp.concatenate([na[0], gx], axis=1), jnp.concatenate([na[1], gc], axis=1)], axis=0)
    else:
        mix = jnp.concatenate([na[0], gx], axis=1)
    return _matmul(mix, w_out.astype(BF16), tm=tm, out_dtype=F32)


def _odd_mixer(h, w_in, w_out, log_decay, rope, *, n_batch, seq, ctx_len, ctx_out, tm):
    n_x = n_batch * seq
    n_heads = log_decay.shape[1]
    qk_w = n_heads * RET_QK_DIM
    v_w = n_heads * RET_V_DIM
    w_in = w_in.astype(BF16)
    lam = jnp.broadcast_to(log_decay.astype(F32).T[:, :, None], (n_heads, 2, RET_QK_DIM))
    lam_rows = lam.transpose(1, 0, 2).reshape(2, qk_w)
    q = _matmul(h, w_in, col0=0, n_cols=qk_w, tm=tm, out_dtype=BF16, epilogue="rope", rope=rope)
    k, kf, kb = _matmul(h, w_in, col0=qk_w, n_cols=qk_w, tm=tm, out_dtype=BF16, epilogue="ret_k", rope=rope,
                        col_rows=lam_rows, n_out=3)
    v = _matmul(h, w_in, col0=2 * qk_w, n_cols=v_w, tm=tm, out_dtype=BF16)
    g = _matmul(h, w_in, col0=2 * qk_w + v_w, n_cols=v_w, tm=tm, out_dtype=BF16, epilogue="silu")
    zeros = jnp.zeros((n_batch, n_heads, RET_QK_DIM, RET_V_DIM), F32)
    ctx_kw = dict(row0=n_x, n_batch=n_batch, seq=ctx_len, block_rows=ctx_len, n_heads=n_heads)
    x_kw = dict(row0=0, n_batch=n_batch, seq=seq, block_rows=min(SCAN_BLOCK, seq), n_heads=n_heads)
    oc_b, sc_b = _ret_scan(q, k, kb, v, lam, zeros, fwd=False, **ctx_kw)
    yc, sc_f = _ret_scan(q, k, kf, v, lam, zeros, fwd=True, gate=g, prev=oc_b, **ctx_kw)
    ox_b, _ = _ret_scan(q, k, kb, v, lam, sc_b, fwd=False, **x_kw)
    yx, _ = _ret_scan(q, k, kf, v, lam, sc_f, fwd=True, gate=g, prev=ox_b, **x_kw)
    mix = jnp.concatenate([yx, yc], axis=0) if ctx_out else yx
    return _matmul(mix, w_out.astype(BF16), tm=tm, out_dtype=F32)


def kernel(x, c, ctx, c_ctx, ada_w, ada_b, ln_g, ln_b, e_w_in, e_w_out, na_rpb, hg_lb_logits, hg_norm_g, ffn_w1, ffn_w3,
           ffn_w2, o_w_in, o_w_out, ret_log_decay, router_w, router_b, moe_w1, moe_w3, moe_w2):
    n_batch, seq, d = x.shape
    ctx_len = ctx.shape[1]
    depth = ada_w.shape[0]
    n_x = n_batch * seq
    n_all = n_x + n_batch * ctx_len
    alpha = (2 * depth) ** 0.25
    tm = _pick_tile(math.gcd(math.gcd(n_x, n_all), seq), PROJ_TILE, unit=RET_CHUNK)
    dims = dict(n_batch=n_batch, seq=seq, ctx_len=ctx_len, tm=tm)

    lb_cum = jnp.cumsum(jax.nn.softmax(hg_lb_logits.astype(F32), axis=1), axis=1)
    lower_bounds = lb_cum - lb_cum[:, :1]
    cos, sin = _rope_tables(seq, GRID_W, tm)
    tiles_per_seq = seq // tm
    rope = (cos, sin, lambda i: jnp.where(i < n_x // tm, i % tiles_per_seq, tiles_per_seq))

    cond = jnp.concatenate([c, c_ctx[None, :], jnp.zeros((8 - n_batch - 1, d), F32)], axis=0)
    mods = [_matmul(cond, ada_w[l], bias=ada_b[l], silu_a=True, tm=8, tn=1024).reshape(8, 6, d)[:n_batch + 1]
            for l in range(depth)]

    tok = jnp.concatenate([x.reshape(n_x, d), ctx.reshape(n_batch * ctx_len, d)], axis=0)
    h = _modulate(tok, mods[0][:, 0:2], rows_per_batch=seq)
    for layer in range(depth):
        j = layer // 2
        last = layer == depth - 1
        mod = mods[layer]
        n_rows = n_x if last else n_all
        if layer % 2 == 0:
            y = _even_mixer(h, e_w_in[j], e_w_out[j], na_rpb[j], lower_bounds[:, j], hg_norm_g[j], ctx_out=not last, **dims)
            router = None
        else:
            y = _odd_mixer(h, o_w_in[j], o_w_out[j], ret_log_decay[j], rope, ctx_out=not last, **dims)
            rw = jnp.zeros((d, LANES), F32).at[:, :N_EXPERTS].set(router_w[j])
            rb = jnp.zeros((1, LANES), F32).at[0, :N_EXPERTS].set(router_b[j])
            router = (rw, rb)
        lnp = lambda i: jnp.stack([ln_g[layer, i], ln_b[layer, i]])
        res = _residual_ln(tok, y, mod[:, 2:5], lnp(0), alpha=alpha, rows_per_batch=seq, n_rows=n_rows, router=router)
        tok, h = res[0], res[1]
        nxt = mods[layer + 1][:, 0:2] if not last else jnp.stack([jnp.zeros_like(mod[:, 0])] * 2, axis=1)
        mod2 = jnp.concatenate([mod[:, 5:6], nxt], axis=1)
        if layer % 2 == 0:
            ones = jnp.ones((n_rows // ROW_TILE,), jnp.int32)
            y = _grouped_swiglu(h, ffn_w1[j:j + 1].astype(BF16), ffn_w3[j:j + 1].astype(BF16),
                                ffn_w2[j:j + 1].astype(BF16), 0 * ones, ones, tm=ROW_TILE,
                                tf=_pick_tile(ffn_w1.shape[2], 1408))
            res = _residual_ln(tok, y, mod2, lnp(1), alpha=alpha, rows_per_batch=seq)
        else:
            gates, idx = res[2], res[3]
            slot_token, assign_slot, tile_expert, tile_valid = _route(idx[:, :2], N_EXPERTS, ROW_TILE)
            xs = jnp.take(h, slot_token, axis=0, mode="clip")
            ys = _grouped_swiglu(xs, moe_w1[j].astype(BF16), moe_w3[j].astype(BF16), moe_w2[j].astype(BF16),
                                 tile_expert, tile_valid, tm=ROW_TILE, tf=_pick_tile(moe_w1.shape[3], 512))
            y12 = jnp.take(ys, assign_slot.reshape(-1), axis=0, mode="clip").reshape(n_rows, 2 * d)
            res = _residual_ln(tok, y12, mod2, lnp(1), alpha=alpha, rows_per_batch=seq, y_gates=gates)
        tok, h = res[0], res[1]
    return tok[:n_x].reshape(n_batch, seq, d)
```

```python
import functools
import math

import jax
import jax.numpy as jnp
from jax import lax
from jax.experimental import pallas as pl
from jax.experimental.pallas import tpu as pltpu

F32 = jnp.float32
BF16 = jnp.bfloat16

GRID_W = 64
NA_HEAD_DIM = 64
NA_KH = 8
NA_KW = 16
HG_KEY_DIM = 128
RET_QK_DIM = 256
RET_V_DIM = 512
N_EXPERTS = 8
ROPE_BASE = 10000.0
LN_EPS = 1e-5
NORM_EPS = 1e-6
LOG2_E = 1.4426950408889634
LANES = 128
HG_CHUNK = 128
RET_CHUNK = 256
ROW_TILE = 512
PROJ_TILE = 1024
SCAN_BLOCK = 1024
VMEM_LIMIT = 48 * 1024 * 1024
MASK_VALUE = -1e30


def _cparams(*sem):
    return pltpu.CompilerParams(dimension_semantics=sem, vmem_limit_bytes=VMEM_LIMIT)


def _dot(a, b):
    return jnp.dot(a, b, preferred_element_type=F32)


def _dot_nt(a, b):
    return lax.dot_general(a, b, (((1,), (1,)), ((), ())), preferred_element_type=F32)


def _dot_tn(a, b):
    return lax.dot_general(a, b, (((0,), (0,)), ((), ())), preferred_element_type=F32)


def _silu(x):
    return x * jax.nn.sigmoid(x)


def _pick_tile(n, target, unit=LANES):
    best = None
    for t in range(unit, min(n, target) + 1, unit):
        if n % t == 0:
            best = t
    assert best is not None, (n, target, unit)
    return best


def _rope(x, cos, sin):
    half = LANES // 2
    swapped = jnp.concatenate([pltpu.roll(x[:, g * LANES:(g + 1) * LANES], half, 1) for g in range(x.shape[1] // LANES)],
                              axis=1)
    return x * cos + swapped * sin


def _mm_kernel(a_ref, w_ref, *rest, silu_a, epilogue):
    a = a_ref[...]
    if silu_a:
        a = _silu(a.astype(F32))
    acc = _dot(a.astype(BF16), w_ref[...].astype(BF16))
    if epilogue == "bias":
        b_ref, o_ref = rest
        o_ref[...] = (acc + b_ref[...]).astype(o_ref.dtype)
    elif epilogue == "silu":
        (o_ref,) = rest
        o_ref[...] = _silu(acc).astype(o_ref.dtype)
    elif epilogue == "log_forget":
        lb_ref, o_ref = rest
        log_sig = jnp.minimum(acc, 0.0) - jnp.log(1.0 + jnp.exp(-jnp.abs(acc)))
        log_lb = lb_ref[0:1, :]
        bb = lb_ref[1:2, :] + log_sig
        o_ref[...] = jnp.maximum(log_lb, bb) + jnp.log(1.0 + jnp.exp(-jnp.abs(log_lb - bb)))
    elif epilogue in ("rope", "ret_k"):
        cos_ref, sin_ref = rest[0], rest[1]
        reps = acc.shape[1] // cos_ref.shape[1]
        cos = jnp.concatenate([cos_ref[...]] * reps, axis=1)
        sin = jnp.concatenate([sin_ref[...]] * reps, axis=1)
        r = _rope(acc, cos, sin)
        if epilogue == "rope":
            rest[2][...] = r.astype(rest[2].dtype)
        else:
            lam_ref, k_ref, kf_ref, kb_ref = rest[2:]
            k = r * (RET_QK_DIM ** -0.5)
            ipos = (lax.broadcasted_iota(jnp.int32, k.shape, 0) & (RET_CHUNK - 1)).astype(F32)
            k_ref[...] = k.astype(k_ref.dtype)
            kf_ref[...] = (k * jnp.exp((RET_CHUNK - 1.0 - ipos) * lam_ref[0:1, :])).astype(kf_ref.dtype)
            kb_ref[...] = (k * jnp.exp(ipos * lam_ref[1:2, :])).astype(kb_ref.dtype)
    else:
        (o_ref,) = rest
        o_ref[...] = acc.astype(o_ref.dtype)


def _matmul(a, w, *, col0=0, n_cols=None, n_rows=None, tm=PROJ_TILE, tn=512, out_dtype=F32, silu_a=False, epilogue=None,
            bias=None, col_rows=None, rope=None, n_out=1):
    k = a.shape[1]
    n_rows = a.shape[0] if n_rows is None else n_rows
    n_cols = w.shape[1] - col0 if n_cols is None else n_cols
    tn = _pick_tile(math.gcd(n_cols, col0) if col0 else n_cols, tn)
    assert n_rows % tm == 0 and n_cols % tn == 0 and col0 % tn == 0
    cb = col0 // tn
    in_specs = [pl.BlockSpec((tm, k), lambda j, i: (i, 0)),
                pl.BlockSpec((k, tn), lambda j, i: (0, cb + j))]
    args = [a, w]
    if rope is not None:
        cos, sin, block_fn = rope
        in_specs += [pl.BlockSpec((tm, cos.shape[1]), lambda j, i: (block_fn(i), 0))] * 2
        args += [cos, sin]
    if bias is not None:
        col_rows = bias.reshape(1, -1)
    if col_rows is not None:
        in_specs.append(pl.BlockSpec((col_rows.shape[0], tn), lambda j, i: (0, j)))
        args.append(col_rows)
    out_spec = pl.BlockSpec((tm, tn), lambda j, i: (i, j))
    out_sds = jax.ShapeDtypeStruct((n_rows, n_cols), out_dtype)
    return pl.pallas_call(
        functools.partial(_mm_kernel, silu_a=silu_a, epilogue="bias" if bias is not None else epilogue),
        grid=(n_cols // tn, n_rows // tm),
        in_specs=in_specs,
        out_specs=out_spec if n_out == 1 else [out_spec] * n_out,
        out_shape=out_sds if n_out == 1 else [out_sds] * n_out,
        compiler_params=_cparams("parallel", "parallel"),
        name="matmul" if epilogue is None else "matmul_" + epilogue,
    )(*args)


def _mm_parts_kernel(*refs, n_parts, n_x_tiles, has_ctx):
    w_ref, o_ref = refs[-2], refs[-1]
    is_ctx = pl.program_id(1) >= n_x_tiles
    acc = None
    k0 = 0
    for p in range(n_parts):
        if has_ctx:
            a = jnp.where(is_ctx, refs[2 * p + 1][...], refs[2 * p][...])
        else:
            a = refs[p][...]
        k1 = k0 + a.shape[1]
        part = _dot(a, w_ref[k0:k1, :])
        acc = part if acc is None else acc + part
        k0 = k1
    o_ref[...] = acc.astype(o_ref.dtype)


def _matmul_parts(parts_x, parts_c, w, *, tm, tn=512, out_dtype=F32):
    n_x = parts_x[0].shape[0]
    has_ctx = parts_c is not None
    n_c = parts_c[0].shape[0] if has_ctx else 0
    n_cols = w.shape[1]
    tn = _pick_tile(n_cols, tn)
    assert n_x % tm == 0 and n_c % tm == 0
    n_x_tiles, n_c_tiles = n_x // tm, n_c // tm
    in_specs, args = [], []
    for p, ax in enumerate(parts_x):
        in_specs.append(pl.BlockSpec((tm, ax.shape[1]), lambda j, i: (jnp.minimum(i, n_x_tiles - 1), 0)))
        args.append(ax)
        if has_ctx:
            in_specs.append(pl.BlockSpec((tm, ax.shape[1]), lambda j, i: (jnp.maximum(i - n_x_tiles, 0), 0)))
            args.append(parts_c[p])
    in_specs.append(pl.BlockSpec((w.shape[0], tn), lambda j, i: (0, j)))
    args.append(w)
    return pl.pallas_call(
        functools.partial(_mm_parts_kernel, n_parts=len(parts_x), n_x_tiles=n_x_tiles, has_ctx=has_ctx),
        grid=(n_cols // tn, n_x_tiles + n_c_tiles),
        in_specs=in_specs,
        out_specs=pl.BlockSpec((tm, tn), lambda j, i: (i, j)),
        out_shape=jax.ShapeDtypeStruct((n_x + n_c, n_cols), out_dtype),
        compiler_params=_cparams("parallel", "parallel"),
        name="matmul_parts",
    )(*args)


def _modulate_kernel(x_ref, mod_ref, h_ref):
    m = mod_ref[0]
    h_ref[...] = (x_ref[...] * (1.0 + m[1:2]) + m[0:1]).astype(h_ref.dtype)


def _mod_index(tiles_per_batch, n_batch):
    return lambda i: (jnp.minimum(i // tiles_per_batch, n_batch), 0, 0)


def _modulate(x, mod, *, rows_per_batch, tm=ROW_TILE):
    m, d = x.shape
    return pl.pallas_call(
        _modulate_kernel,
        grid=(m // tm,),
        in_specs=[pl.BlockSpec((tm, d), lambda i: (i, 0)),
                  pl.BlockSpec((1, 2, d), _mod_index(rows_per_batch // tm, mod.shape[0] - 1))],
        out_specs=pl.BlockSpec((tm, d), lambda i: (i, 0)),
        out_shape=jax.ShapeDtypeStruct((m, d), BF16),
        compiler_params=_cparams("parallel"),
        name="modulate",
    )(x, mod)


def _ln_kernel(*refs, alpha, two_y, router, n_experts):
    x_ref, y_ref = refs[0], refs[1]
    pos = 2
    if two_y:
        y2_ref, yg_ref = refs[pos], refs[pos + 1]
        pos += 2
    mod_ref, lnp_ref = refs[pos], refs[pos + 1]
    pos += 2
    if router:
        rw_ref, rb_ref = refs[pos], refs[pos + 1]
        pos += 2
    xo_ref, h_ref = refs[pos], refs[pos + 1]
    m = mod_ref[0]
    d = x_ref.shape[1]
    if two_y:
        yg = yg_ref[...]
        y = yg[:, 0:1] * y_ref[...].astype(F32) + yg[:, 1:2] * y2_ref[...].astype(F32)
    else:
        y = y_ref[...].astype(F32)
    z = alpha * x_ref[...] + m[0:1] * y
    mu = jnp.mean(z, axis=-1, keepdims=True)
    zc = z - mu
    var = jnp.mean(zc * zc, axis=-1, keepdims=True)
    xn = zc * lax.rsqrt(var + LN_EPS) * lnp_ref[0:1, :] + lnp_ref[1:2, :]
    xo_ref[...] = xn
    hf = xn * (1.0 + m[2:3]) + m[1:2]
    h_ref[...] = hf.astype(h_ref.dtype)
    if router:
        g_ref, i_ref = refs[pos + 2], refs[pos + 3]
        logits = jnp.dot(hf, rw_ref[...], preferred_element_type=F32, precision=lax.Precision.HIGHEST) + rb_ref[...]
        lane = lax.broadcasted_iota(jnp.int32, logits.shape, 1).astype(F32)
        neg = -jnp.inf
        lg = jnp.where(lane < n_experts, logits, neg)
        m1 = jnp.max(lg, axis=-1, keepdims=True)
        i1 = jnp.min(jnp.where(lg == m1, lane, float(LANES)), axis=-1, keepdims=True)
        lg2 = jnp.where(lane == i1, neg, lg)
        m2 = jnp.max(lg2, axis=-1, keepdims=True)
        i2 = jnp.min(jnp.where(lg2 == m2, lane, float(LANES)), axis=-1, keepdims=True)
        e2 = jnp.exp(m2 - m1)
        den = 1.0 + e2
        g_ref[...] = jnp.where(lane == 0, 1.0 / den, jnp.where(lane == 1, e2 / den, 0.0))
        i_ref[...] = jnp.where(lane == 0, i1, jnp.where(lane == 1, i2, 0.0)).astype(jnp.int32)


def _residual_ln(x, y, mod, lnp, *, alpha, rows_per_batch, n_rows=None, y_gates=None, router=None, tm=ROW_TILE):
    d = x.shape[1]
    n_rows = x.shape[0] if n_rows is None else n_rows
    row = pl.BlockSpec((tm, d), lambda i: (i, 0))
    lane_row = pl.BlockSpec((tm, LANES), lambda i: (i, 0))
    two_y = y_gates is not None
    in_specs = [row, row]
    args = [x, y]
    if two_y:
        second_half = n_rows // tm
        in_specs += [pl.BlockSpec((tm, d), lambda i: (second_half + i, 0)), lane_row]
        args += [y, y_gates]
    in_specs += [pl.BlockSpec((1, 3, d), _mod_index(rows_per_batch // tm, mod.shape[0] - 1)),
                 pl.BlockSpec((2, d), lambda i: (0, 0))]
    args += [mod, lnp]
    out_specs = [row, row]
    out_shape = [jax.ShapeDtypeStruct((n_rows, d), F32), jax.ShapeDtypeStruct((n_rows, d), BF16 if router is None else F32)]
    if router is not None:
        in_specs += [pl.BlockSpec((d, LANES), lambda i: (0, 0)), pl.BlockSpec((1, LANES), lambda i: (0, 0))]
        args += list(router)
        out_specs += [lane_row, lane_row]
        out_shape += [jax.ShapeDtypeStruct((n_rows, LANES), F32), jax.ShapeDtypeStruct((n_rows, LANES), jnp.int32)]
    return pl.pallas_call(
        functools.partial(_ln_kernel, alpha=alpha, two_y=two_y, router=router is not None, n_experts=N_EXPERTS),
        grid=(n_rows // tm,),
        in_specs=in_specs,
        out_specs=out_specs,
        out_shape=out_shape,
        compiler_params=_cparams("parallel"),
        name="residual_ln",
    )(*args)


def _gffn_kernel(te_ref, tv_ref, x_ref, w1_ref, w3_ref, w2_ref, o_ref):
    i, f = pl.program_id(0), pl.program_id(1)

    @pl.when(f == 0)
    def _():
        o_ref[...] = jnp.zeros_like(o_ref)

    @pl.when(tv_ref[i] != 0)
    def _():
        x = x_ref[...].astype(BF16)
        h1 = _dot(x, w1_ref[0, 0])
        h3 = _dot(x, w3_ref[0, 0])
        a = (_silu(h1) * h3).astype(BF16)
        o_ref[...] += _dot(a, w2_ref[0, 0])


def _grouped_swiglu(xs, w1, w3, w2, layer, tile_expert, tile_valid, *, tm, tf):
    p, d = xs.shape
    f = w1.shape[3]
    assert p % tm == 0 and f % tf == 0
    grid_spec = pltpu.PrefetchScalarGridSpec(
        num_scalar_prefetch=2,
        grid=(p // tm, f // tf),
        in_specs=[pl.BlockSpec((tm, d), lambda i, j, te, tv: (i, 0)),
                  pl.BlockSpec((1, 1, d, tf), lambda i, j, te, tv: (layer, te[i], 0, j * tv[i])),
                  pl.BlockSpec((1, 1, d, tf), lambda i, j, te, tv: (layer, te[i], 0, j * tv[i])),
                  pl.BlockSpec((1, 1, tf, d), lambda i, j, te, tv: (layer, te[i], j * tv[i], 0))],
        out_specs=pl.BlockSpec((tm, d), lambda i, j, te, tv: (i, 0)),
    )
    return pl.pallas_call(
        _gffn_kernel,
        grid_spec=grid_spec,
        out_shape=jax.ShapeDtypeStruct((p, d), F32),
        compiler_params=_cparams("parallel", "arbitrary"),
        name="grouped_swiglu",
    )(tile_expert, tile_valid, xs, w1, w3, w2)


def _route(idx, n_experts, tm):
    m = idx.shape[0]
    flat = idx.reshape(-1)
    n = flat.shape[0]
    p = -(-(n + n_experts * (tm - 1)) // tm) * tm
    onehot = (flat[:, None] == jnp.arange(n_experts, dtype=jnp.int32)[None, :]).astype(jnp.int32)
    csum = jnp.cumsum(onehot, axis=0)
    counts = csum[-1]
    padded = -(-counts // tm) * tm
    pad_end = jnp.cumsum(padded)
    pad_off = pad_end - padded
    rank = jnp.sum(csum * onehot, axis=1) - 1
    assign_slot = (pad_off[flat] + rank).reshape(m, 2)
    slot = jnp.arange(p, dtype=jnp.int32)
    slot_expert = jnp.minimum(jnp.sum(pad_end[None, :] <= slot[:, None], axis=1), n_experts - 1).astype(jnp.int32)
    slot_rank = slot - pad_off[slot_expert]
    order = jnp.argsort(flat, stable=True).astype(jnp.int32)
    off = jnp.cumsum(counts) - counts
    src = order[jnp.minimum(off[slot_expert] + slot_rank, n - 1)]
    slot_token = jnp.where(slot_rank < counts[slot_expert], src // 2, slot % m).astype(jnp.int32)
    tile_start = slot[::tm]
    tile_valid = (tile_start < pad_end[-1]).astype(jnp.int32)
    tile_expert = slot_expert[::tm]
    return slot_token, assign_slot, tile_expert, tile_valid


def _softmax_pv(parts):
    mx = functools.reduce(jnp.maximum, [jnp.max(s, axis=-1, keepdims=True) for s, _ in parts])
    ps = [jnp.exp(s - mx) for s, _ in parts]
    den = functools.reduce(jnp.add, [jnp.sum(p, axis=-1, keepdims=True) for p in ps])
    num = functools.reduce(jnp.add, [_dot(p.astype(BF16), v) for p, (_, v) in zip(ps, parts)])
    return num / den


def _na_kernel(*refs, rows, gw, ctx_out):
    if ctx_out:
        q_ref, k_ref, v_ref, kc_ref, vc_ref, qc_ref, bias_ref, o_ref, oc_ref, s_scr, p_scr, l_scr = refs
    else:
        q_ref, k_ref, v_ref, kc_ref, vc_ref, bias_ref, o_ref, s_scr, p_scr, l_scr = refs
    lane = lax.broadcasted_iota(jnp.int32, (1, LANES), 1)
    head_lanes = [lane < NA_HEAD_DIM, lane >= NA_HEAD_DIM]
    scale = NA_HEAD_DIM ** -0.5
    band = NA_KH * gw

    def band_rows(r):
        b0 = jnp.clip(r - NA_KH // 2, 0, rows - NA_KH)
        return b0, pl.ds(pl.multiple_of(b0 * gw, gw), band)

    def scores(r, par):
        r = jnp.minimum(r, rows - 1)
        b0, ks = band_rows(r)
        dr0 = b0 - r + NA_KH - 1
        q = q_ref[pl.ds(pl.multiple_of(r * gw, gw), gw), :]
        kb = k_ref[ks, :]
        for h in range(2):
            qh = jnp.where(head_lanes[h], q, 0) * scale
            s_scr[par, h, :, :band] = _dot_nt(qh, kb) + bias_ref[h, dr0]
            s_scr[par, h, :, band:] = _dot_nt(qh, kc_ref[...])

    def softmax(par):
        for h in range(2):
            s = s_scr[par, h]
            e = jnp.exp(s - jnp.max(s, axis=-1, keepdims=True))
            l_scr[par, h] = jnp.broadcast_to(jnp.sum(e, axis=-1, keepdims=True), (gw, LANES))
            p_scr[par, h] = e.astype(BF16)

    def weighted_values(r, par):
        _, ks = band_rows(r)
        vb = v_ref[ks, :]
        outs = []
        for h in range(2):
            num = _dot(p_scr[par, h, :, :band], vb) + _dot(p_scr[par, h, :, band:], vc_ref[...])
            outs.append(num / l_scr[par, h])
        o_ref[pl.ds(pl.multiple_of(r * gw, gw), gw), :] = jnp.where(head_lanes[0], outs[0], outs[1]).astype(o_ref.dtype)

    scores(0, 0)
    scores(1, 1)
    softmax(0)

    def body(i, carry):
        r = 2 * i
        weighted_values(r, 0)
        softmax(1)
        scores(r + 2, 0)
        weighted_values(r + 1, 1)
        softmax(0)
        scores(r + 3, 1)
        return carry

    lax.fori_loop(0, rows // 2, body, 0)
    if ctx_out:
        qc = qc_ref[...]
        kc = kc_ref[...]
        vc = vc_ref[...]
        outs = []
        for h in range(2):
            qh = jnp.where(head_lanes[h], qc, 0) * scale
            outs.append(_softmax_pv([(_dot_nt(qh, kc), vc)]))
        oc_ref[...] = jnp.where(head_lanes[0], outs[0], outs[1]).astype(oc_ref.dtype)


def _na_bias_table(rpb, gw):
    col = jnp.arange(gw)
    c_start = jnp.clip(col - NA_KW // 2, 0, gw - NA_KW)
    col_in = (col[None, :] >= c_start[:, None]) & (col[None, :] < c_start[:, None] + NA_KW)
    dc = jnp.clip(col[None, :] - col[:, None], 1 - NA_KW, NA_KW - 1) + NA_KW - 1
    t = jnp.where(col_in[None, None], rpb[:, :, dc].astype(F32), MASK_VALUE)
    win = jnp.stack([t[:, d:d + NA_KH] for d in range(NA_KH)], axis=1)
    return win.transpose(0, 1, 3, 2, 4).reshape(rpb.shape[0], NA_KH, gw, NA_KH * gw)


def _neighbourhood_attention(proj, rpb, *, n_batch, seq, ctx_len, ctx_out):
    width = rpb.shape[0] * NA_HEAD_DIM
    pairs = width // LANES
    rows = seq // GRID_W
    bias = _na_bias_table(rpb, GRID_W)
    n_keys = NA_KH * GRID_W + ctx_len
    cblk0 = n_batch * seq // ctx_len
    xspec = lambda g: pl.BlockSpec((seq, LANES), lambda b, p: (b, g * pairs + p))
    cspec = lambda g: pl.BlockSpec((ctx_len, LANES), lambda b, p: (cblk0 + b, g * pairs + p))
    in_specs = [xspec(0), xspec(1), xspec(2), cspec(1), cspec(2)]
    args = [proj, proj, proj, proj, proj]
    if ctx_out:
        in_specs.append(cspec(0))
        args.append(proj)
    in_specs.append(pl.BlockSpec((2, NA_KH, GRID_W, NA_KH * GRID_W), lambda b, p: (p, 0, 0, 0)))
    args.append(bias)
    out_specs = [pl.BlockSpec((seq, LANES), lambda b, p: (b, p))]
    out_shape = [jax.ShapeDtypeStruct((n_batch * seq, width), BF16)]
    if ctx_out:
        out_specs.append(pl.BlockSpec((ctx_len, LANES), lambda b, p: (b, p)))
        out_shape.append(jax.ShapeDtypeStruct((n_batch * ctx_len, width), BF16))
    return pl.pallas_call(
        functools.partial(_na_kernel, rows=rows, gw=GRID_W, ctx_out=ctx_out),
        grid=(n_batch, pairs),
        in_specs=in_specs,
        out_specs=out_specs,
        out_shape=out_shape,
        scratch_shapes=[pltpu.VMEM((2, 2, GRID_W, n_keys), F32),
                        pltpu.VMEM((2, 2, GRID_W, n_keys), BF16),
                        pltpu.VMEM((2, 2, GRID_W, LANES), F32)],
        compiler_params=_cparams("parallel", "parallel"),
        name="neighbourhood_attention",
    )(*args)


def _hg_kernel(*refs, rev, n_chunks, n_blocks, epilogue):
    if epilogue:
        q_ref, f_ref, i_ref, s0_ref, pm_ref, rm_ref, g_ref, prev_ref, ng_ref, o_ref, sT_ref, st_scr = refs
    else:
        q_ref, f_ref, i_ref, s0_ref, pm_ref, rm_ref, o_ref, sT_ref, st_scr = refs
    blk = pl.program_id(2)

    @pl.when(blk == 0)
    def _():
        st_scr[...] = s0_ref[0, 0]

    c = HG_CHUNK

    def chunk(ci, carry):
        cc = (n_chunks - 1 - ci) if rev else ci
        sl = pl.ds(pl.multiple_of(cc * c, c), c)
        q = q_ref[sl, :].astype(F32)
        lf = f_ref[sl, :] * LOG2_E
        k = 1.0 - jnp.exp2(lf)
        v = i_ref[sl, :]
        p, tot = lf, lf
        att = pm_ref[0] * jnp.sum(q * k, axis=-1, keepdims=True)
        for lvl in range(1, pm_ref.shape[0]):
            m = 1 << (lvl - 1)
            second = rm_ref[lvl - 1] != 0.0
            is_q = jnp.logical_not(second) if rev else second
            z = (jnp.where(is_q, q, k) * jnp.exp2(jnp.where(is_q, p, tot - p))).astype(BF16)
            att = att + pm_ref[lvl] * _dot_nt(z, z)
            t_up = pltpu.roll(tot, m, 0)
            t_dn = pltpu.roll(tot, c - m, 0)
            p = p + (jnp.where(second, 0.0, t_dn) if rev else jnp.where(second, t_up, 0.0))
            tot = tot + jnp.where(second, t_up, t_dn)
        st = st_scr[...]
        o = _dot(att.astype(BF16), v) + _dot_nt((q * jnp.exp2(p)).astype(BF16), st.astype(BF16))
        ku = (k * jnp.exp2(tot - p)).astype(BF16)
        st_scr[...] = st * jnp.exp2(tot[0:1, :]) + _dot_tn(v, ku)
        if epilogue:
            o = o + prev_ref[sl, :]
            o = o * lax.rsqrt(jnp.mean(o * o, axis=-1, keepdims=True) + NORM_EPS) * ng_ref[...]
            o = o * g_ref[sl, :].astype(F32)
        o_ref[sl, :] = o.astype(o_ref.dtype)
        return carry

    lax.fori_loop(0, n_chunks, chunk, 0)

    @pl.when(blk == n_blocks - 1)
    def _():
        sT_ref[0, 0] = st_scr[...]


def _hg_level_masks(rev):
    c = HG_CHUNK
    assert c == HG_KEY_DIM
    t = jnp.arange(c)[:, None]
    s = jnp.arange(c)[None, :]
    pair, row = [t == s], []
    m = 1
    while m < c:
        t_second = (t & m) != 0
        s_second = (s & m) != 0
        same = (t ^ s) < 2 * m
        pair.append(same & (~t_second & s_second if rev else t_second & ~s_second))
        row.append(jnp.broadcast_to(t_second, (c, c)))
        m *= 2
    return jnp.stack(pair).astype(F32), jnp.stack(row).astype(F32)


def _hg_scan(qg, lf, proj, s0, *, rev, row0, n_batch, seq, block_rows, n_heads, col_f, col_i, prev=None, norm_g=None):
    dk = HG_KEY_DIM
    n_blocks = seq // block_rows
    blk0 = row0 // block_rows
    epilogue = prev is not None

    def local_rows(b, i):
        return b * n_blocks + ((n_blocks - 1 - i) if rev else i)

    pspec = lambda col: pl.BlockSpec((block_rows, dk), lambda b, h, i: (blk0 + local_rows(b, i), col + h))
    state_spec = pl.BlockSpec((1, 1, dk, dk), lambda b, h, i: (b, h, 0, 0))
    local_spec = pl.BlockSpec((block_rows, dk), lambda b, h, i: (local_rows(b, i), h))
    pair_mask, row_mask = _hg_level_masks(rev)
    const_spec = lambda a: pl.BlockSpec(a.shape, lambda b, h, i: (0, 0, 0))
    in_specs = [pspec(0), pspec(col_f), pspec(col_i), state_spec, const_spec(pair_mask), const_spec(row_mask)]
    args = [qg, lf, proj, s0, pair_mask, row_mask]
    if epilogue:
        in_specs += [pspec(n_heads), local_spec, pl.BlockSpec((1, dk), lambda b, h, i: (0, 0))]
        args += [qg, prev, norm_g.reshape(1, dk)]
    return pl.pallas_call(
        functools.partial(_hg_kernel, rev=rev, n_chunks=block_rows // HG_CHUNK, n_blocks=n_blocks, epilogue=epilogue),
        grid=(n_batch, n_heads, n_blocks),
        in_specs=in_specs,
        out_specs=[local_spec, state_spec],
        out_shape=[jax.ShapeDtypeStruct((n_batch * seq, n_heads * dk), BF16 if epilogue else F32),
                   jax.ShapeDtypeStruct((n_batch, n_heads, dk, dk), F32)],
        scratch_shapes=[pltpu.VMEM((dk, dk), F32)],
        compiler_params=_cparams("parallel", "parallel", "arbitrary"),
        name="hgrn2_scan_rev" if rev else "hgrn2_scan_fwd",
    )(*args)


def _ret_kernel(*refs, fwd, n_chunks, n_blocks):
    if fwd:
        q_ref, k_ref, kd_ref, v_ref, lam_ref, s0_ref, g_ref, prev_ref, o_ref, sT_ref, st_scr, qd_scr, dm_scr = refs
    else:
        q_ref, kd_ref, v_ref, lam_ref, s0_ref, o_ref, sT_ref, st_scr, qd_scr = refs
    blk = pl.program_id(2)
    c = RET_CHUNK
    lam_f = lam_ref[0, 0:1, :]
    lam_b = lam_ref[0, 1:2, :]
    lam = lam_f if fwd else lam_b

    @pl.when(blk == 0)
    def _():
        st_scr[...] = s0_ref[0, 0]
        ipos = lax.broadcasted_iota(jnp.int32, (c, LANES), 0).astype(F32)
        steps = (ipos + 1.0) if fwd else (c - ipos)
        qd_scr[...] = jnp.exp(steps * lam[:, :LANES])
        if fwd:
            dist = (lax.broadcasted_iota(jnp.int32, (c, c), 0) - lax.broadcasted_iota(jnp.int32, (c, c), 1)).astype(F32)
            dm_scr[...] = (jnp.where(dist >= 0, jnp.exp(jnp.maximum(dist, 0.0) * lam_f[:, :c]), 0.0)
                           + jnp.where(dist <= 0, jnp.exp(jnp.maximum(-dist, 0.0) * lam_b[:, :c]), 0.0))

    chunk_decay = jnp.exp(float(c) * lam[:, 0:1])

    def chunk(ci, carry):
        cc = ci if fwd else (n_chunks - 1 - ci)
        sl = pl.ds(pl.multiple_of(cc * c, c), c)
        q = q_ref[sl, :]
        v = v_ref[sl, :]
        st = st_scr[...]
        o = _dot(q, st.astype(BF16)) * qd_scr[:, 0:1]
        st_scr[...] = st * chunk_decay + _dot_tn(kd_ref[sl, :], v)
        if fwd:
            att = _dot_nt(q, k_ref[sl, :]) * dm_scr[...]
            o = o + _dot(att.astype(BF16), v) + prev_ref[sl, :].astype(F32)
            mu = jnp.mean(o, axis=-1, keepdims=True)
            oc = o - mu
            var = jnp.mean(oc * oc, axis=-1, keepdims=True)
            o = oc * lax.rsqrt(var + LN_EPS) * g_ref[sl, :].astype(F32)
        o_ref[sl, :] = o.astype(o_ref.dtype)
        return carry

    lax.fori_loop(0, n_chunks, chunk, 0)

    @pl.when(blk == n_blocks - 1)
    def _():
        sT_ref[0, 0] = st_scr[...]


def _ret_scan(q, k, kd, v, lam, s0, *, fwd, row0, n_batch, seq, block_rows, n_heads, gate=None, prev=None):
    dk, dv = RET_QK_DIM, RET_V_DIM
    n_blocks = seq // block_rows
    blk0 = row0 // block_rows
    assert RET_CHUNK <= dk

    def lrow(b, i):
        return b * n_blocks + (i if fwd else (n_blocks - 1 - i))

    kspec = pl.BlockSpec((block_rows, dk), lambda b, h, i: (blk0 + lrow(b, i), h))
    vspec = pl.BlockSpec((block_rows, dv), lambda b, h, i: (blk0 + lrow(b, i), h))
    local_spec = pl.BlockSpec((block_rows, dv), lambda b, h, i: (lrow(b, i), h))
    state_spec = pl.BlockSpec((1, 1, dk, dv), lambda b, h, i: (b, h, 0, 0))
    lam_spec = pl.BlockSpec((1, 2, dk), lambda b, h, i: (h, 0, 0))
    scratch = [pltpu.VMEM((dk, dv), F32), pltpu.VMEM((RET_CHUNK, LANES), F32)]
    if fwd:
        in_specs = [kspec, kspec, kspec, vspec, lam_spec, state_spec, vspec, local_spec]
        args = [q, k, kd, v, lam, s0, gate, prev]
        scratch.append(pltpu.VMEM((RET_CHUNK, RET_CHUNK), F32))
    else:
        in_specs = [kspec, kspec, vspec, lam_spec, state_spec]
        args = [q, kd, v, lam, s0]
    return pl.pallas_call(
        functools.partial(_ret_kernel, fwd=fwd, n_chunks=block_rows // RET_CHUNK, n_blocks=n_blocks),
        grid=(n_batch, n_heads, n_blocks),
        in_specs=in_specs,
        out_specs=[local_spec, state_spec],
        out_shape=[jax.ShapeDtypeStruct((n_batch * seq, n_heads * dv), BF16),
                   jax.ShapeDtypeStruct((n_batch, n_heads, dk, dv), F32)],
        scratch_shapes=scratch,
        compiler_params=_cparams("parallel", "parallel", "arbitrary"),
        name="retention_fwd" if fwd else "retention_rev",
    )(*args)


def _rope_tables(seq, gw, pad_rows):
    half = LANES // 2
    inv = ROPE_BASE ** (-jnp.arange(half, dtype=F32) / half)
    t = jnp.arange(seq)
    ang_r = (t // gw).astype(F32)[:, None] * inv
    ang_c = (t % gw).astype(F32)[:, None] * inv
    cos = jnp.concatenate([jnp.cos(ang_r)] * 2 + [jnp.cos(ang_c)] * 2, axis=-1)
    sin = jnp.concatenate([-jnp.sin(ang_r), jnp.sin(ang_r), -jnp.sin(ang_c), jnp.sin(ang_c)], axis=-1)
    cos = jnp.concatenate([cos, jnp.ones((pad_rows, cos.shape[1]), F32)], axis=0)
    sin = jnp.concatenate([sin, jnp.zeros((pad_rows, sin.shape[1]), F32)], axis=0)
    return cos, sin


def _even_mixer(h, w_in, w_out, rpb, lb, norm_g, *, n_batch, seq, ctx_len, ctx_out, tm):
    n_x = n_batch * seq
    na_w = rpb.shape[0] * NA_HEAD_DIM
    hg_w = lb.shape[1]
    n_heads = hg_w // HG_KEY_DIM
    cuts = [0, 3 * na_w] + [3 * na_w + i * hg_w for i in range(1, 6)]
    seg = lambda i: w_in[:, cuts[i]:cuts[i + 1]]
    w_plain = jnp.concatenate([seg(0), seg(4)], axis=1).astype(BF16)
    w_silu = jnp.concatenate([seg(1), seg(5)], axis=1).astype(BF16)
    w_lf = jnp.concatenate([seg(2), seg(3)], axis=1).astype(BF16)
    proj = _matmul(h, w_plain, tm=tm, out_dtype=BF16)
    qg = _matmul(h, w_silu, tm=tm, out_dtype=BF16, epilogue="silu")
    lb_rows = jnp.stack([jnp.log(lb).reshape(-1), jnp.log1p(-lb).reshape(-1)])
    lf = _matmul(h, w_lf, tm=tm, out_dtype=F32, epilogue="log_forget", col_rows=lb_rows)
    na = _neighbourhood_attention(proj, rpb, n_batch=n_batch, seq=seq, ctx_len=ctx_len, ctx_out=ctx_out)
    zeros = jnp.zeros((n_batch, n_heads, HG_KEY_DIM, HG_KEY_DIM), F32)
    col_i = 3 * na_w // HG_KEY_DIM
    ctx_kw = dict(row0=n_x, n_batch=n_batch, seq=ctx_len, block_rows=ctx_len, n_heads=n_heads, col_i=col_i)
    x_kw = dict(row0=0, n_batch=n_batch, seq=seq, block_rows=min(SCAN_BLOCK, seq), n_heads=n_heads, col_i=col_i)
    oc_f, sc_f = _hg_scan(qg, lf, proj, zeros, rev=False, col_f=0, **ctx_kw)
    gc, sc_b = _hg_scan(qg, lf, proj, zeros, rev=True, col_f=n_heads, prev=oc_f, norm_g=norm_g, **ctx_kw)
    ox_f, _ = _hg_scan(qg, lf, proj, sc_f, rev=False, col_f=0, **x_kw)
    gx, _ = _hg_scan(qg, lf, proj, sc_b, rev=True, col_f=n_heads, prev=ox_f, norm_g=norm_g, **x_kw)
    return _matmul_parts([na[0], gx], [na[1], gc] if ctx_out else None, w_out.astype(BF16), tm=tm)


def _odd_mixer(h, w_in, w_out, log_decay, rope, *, n_batch, seq, ctx_len, ctx_out, tm):
    n_x = n_batch * seq
    n_heads = log_decay.shape[1]
    qk_w = n_heads * RET_QK_DIM
    v_w = n_heads * RET_V_DIM
    w_in = w_in.astype(BF16)
    lam = jnp.broadcast_to(log_decay.astype(F32).T[:, :, None], (n_heads, 2, RET_QK_DIM))
    lam_rows = lam.transpose(1, 0, 2).reshape(2, qk_w)
    q = _matmul(h, w_in, col0=0, n_cols=qk_w, tm=tm, out_dtype=BF16, epilogue="rope", rope=rope)
    k, kf, kb = _matmul(h, w_in, col0=qk_w, n_cols=qk_w, tm=tm, out_dtype=BF16, epilogue="ret_k", rope=rope,
                        col_rows=lam_rows, n_out=3)
    v = _matmul(h, w_in, col0=2 * qk_w, n_cols=v_w, tm=tm, out_dtype=BF16)
    g = _matmul(h, w_in, col0=2 * qk_w + v_w, n_cols=v_w, tm=tm, out_dtype=BF16, epilogue="silu")
    zeros = jnp.zeros((n_batch, n_heads, RET_QK_DIM, RET_V_DIM), F32)
    ctx_kw = dict(row0=n_x, n_batch=n_batch, seq=ctx_len, block_rows=ctx_len, n_heads=n_heads)
    x_kw = dict(row0=0, n_batch=n_batch, seq=seq, block_rows=min(SCAN_BLOCK, seq), n_heads=n_heads)
    oc_b, sc_b = _ret_scan(q, k, kb, v, lam, zeros, fwd=False, **ctx_kw)
    yc, sc_f = _ret_scan(q, k, kf, v, lam, zeros, fwd=True, gate=g, prev=oc_b, **ctx_kw)
    ox_b, _ = _ret_scan(q, k, kb, v, lam, sc_b, fwd=False, **x_kw)
    yx, _ = _ret_scan(q, k, kf, v, lam, sc_f, fwd=True, gate=g, prev=ox_b, **x_kw)
    return _matmul_parts([yx], [yc] if ctx_out else None, w_out.astype(BF16), tm=tm)


def kernel(x, c, ctx, c_ctx, ada_w, ada_b, ln_g, ln_b, e_w_in, e_w_out, na_rpb, hg_lb_logits, hg_norm_g, ffn_w1, ffn_w3,
           ffn_w2, o_w_in, o_w_out, ret_log_decay, router_w, router_b, moe_w1, moe_w3, moe_w2):
    n_batch, seq, d = x.shape
    ctx_len = ctx.shape[1]
    depth = ada_w.shape[0]
    n_x = n_batch * seq
    n_all = n_x + n_batch * ctx_len
    alpha = (2 * depth) ** 0.25
    tm = _pick_tile(math.gcd(math.gcd(n_x, n_all), seq), PROJ_TILE, unit=RET_CHUNK)
    dims = dict(n_batch=n_batch, seq=seq, ctx_len=ctx_len, tm=tm)

    lb_cum = jnp.cumsum(jax.nn.softmax(hg_lb_logits.astype(F32), axis=1), axis=1)
    lower_bounds = lb_cum - lb_cum[:, :1]
    cos, sin = _rope_tables(seq, GRID_W, tm)
    tiles_per_seq = seq // tm
    rope = (cos, sin, lambda i: jnp.where(i < n_x // tm, i % tiles_per_seq, tiles_per_seq))

    cond = jnp.concatenate([c, c_ctx[None, :], jnp.zeros((8 - n_batch - 1, d), F32)], axis=0)
    mods = [_matmul(cond, ada_w[l], bias=ada_b[l], silu_a=True, tm=8, tn=1024).reshape(8, 6, d)[:n_batch + 1]
            for l in range(depth)]

    ffn_w = [w.astype(BF16)[:, None] for w in (ffn_w1, ffn_w3, ffn_w2)]
    moe_w = [w.astype(BF16) for w in (moe_w1, moe_w3, moe_w2)]

    tok = jnp.concatenate([x.reshape(n_x, d), ctx.reshape(n_batch * ctx_len, d)], axis=0)
    h = _modulate(tok, mods[0][:, 0:2], rows_per_batch=seq)
    for layer in range(depth):
        j = layer // 2
        last = layer == depth - 1
        mod = mods[layer]
        n_rows = n_x if last else n_all
        if layer % 2 == 0:
            y = _even_mixer(h, e_w_in[j], e_w_out[j], na_rpb[j], lower_bounds[:, j], hg_norm_g[j], ctx_out=not last, **dims)
            router = None
        else:
            y = _odd_mixer(h, o_w_in[j], o_w_out[j], ret_log_decay[j], rope, ctx_out=not last, **dims)
            rw = jnp.zeros((d, LANES), F32).at[:, :N_EXPERTS].set(router_w[j])
            rb = jnp.zeros((1, LANES), F32).at[0, :N_EXPERTS].set(router_b[j])
            router = (rw, rb)
        lnp = lambda i: jnp.stack([ln_g[layer, i], ln_b[layer, i]])
        res = _residual_ln(tok, y, mod[:, 2:5], lnp(0), alpha=alpha, rows_per_batch=seq, n_rows=n_rows, router=router)
        tok, h = res[0], res[1]
        nxt = mods[layer + 1][:, 0:2] if not last else jnp.stack([jnp.zeros_like(mod[:, 0])] * 2, axis=1)
        mod2 = jnp.concatenate([mod[:, 5:6], nxt], axis=1)
        if layer % 2 == 0:
            ones = jnp.ones((n_rows // ROW_TILE,), jnp.int32)
            y = _grouped_swiglu(h, *ffn_w, j, 0 * ones, ones, tm=ROW_TILE, tf=_pick_tile(ffn_w1.shape[2], 1408))
            res = _residual_ln(tok, y, mod2, lnp(1), alpha=alpha, rows_per_batch=seq)
        else:
            gates, idx = res[2], res[3]
            slot_token, assign_slot, tile_expert, tile_valid = _route(idx[:, :2], N_EXPERTS, ROW_TILE)
            xs = jnp.take(h, slot_token, axis=0, mode="clip")
            ys = _grouped_swiglu(xs, *moe_w, j, tile_expert, tile_valid, tm=ROW_TILE,
                                 tf=_pick_tile(moe_w1.shape[3], 896))
            y12 = jnp.take(ys, assign_slot.T.reshape(-1), axis=0, mode="clip")
            res = _residual_ln(tok, y12, mod2, lnp(1), alpha=alpha, rows_per_batch=seq, y_gates=gates)
        tok, h = res[0], res[1]
    return tok[:n_x].reshape(n_batch, seq, d)
```

```python
import functools
import math

import jax
import jax.numpy as jnp
from jax import lax
from jax.experimental import pallas as pl
from jax.experimental.pallas import tpu as pltpu

F32 = jnp.float32
BF16 = jnp.bfloat16

GRID_W = 64
NA_HEAD_DIM = 64
NA_KH = 8
NA_KW = 16
HG_KEY_DIM = 128
RET_QK_DIM = 256
RET_V_DIM = 512
N_EXPERTS = 8
ROPE_BASE = 10000.0
LN_EPS = 1e-5
NORM_EPS = 1e-6
LOG2_E = 1.4426950408889634
LANES = 128
HG_CHUNK = 128
RET_CHUNK = 256
ROW_TILE = 512
EVEN_PROJ_TILE = 512
ODD_PROJ_TILE = 512
SCAN_BLOCK = 1024
VMEM_LIMIT = 48 * 1024 * 1024
MASK_VALUE = -1e30


def _cparams(*sem):
    return pltpu.CompilerParams(dimension_semantics=sem, vmem_limit_bytes=VMEM_LIMIT)


def _dot(a, b):
    return jnp.dot(a, b, preferred_element_type=F32)


def _dot_nt(a, b):
    return lax.dot_general(a, b, (((1,), (1,)), ((), ())), preferred_element_type=F32)


def _dot_tn(a, b):
    return lax.dot_general(a, b, (((0,), (0,)), ((), ())), preferred_element_type=F32)


def _dot_3pass(a, b):
    a_hi = a.astype(BF16)
    b_hi = b.astype(BF16)
    a_lo = (a - a_hi.astype(F32)).astype(BF16)
    b_lo = (b - b_hi.astype(F32)).astype(BF16)
    return _dot(a_hi, b_hi) + _dot(a_lo, b_hi) + _dot(a_hi, b_lo)


def _silu(x):
    return x * jax.nn.sigmoid(x)


def _pick_tile(n, target, unit=LANES):
    best = None
    for t in range(unit, min(n, target) + 1, unit):
        if n % t == 0:
            best = t
    assert best is not None, (n, target, unit)
    return best


def _rope(x, cos, sin):
    half = LANES // 2
    swapped = jnp.concatenate([pltpu.roll(x[:, g * LANES:(g + 1) * LANES], half, 1) for g in range(x.shape[1] // LANES)],
                              axis=1)
    return x * cos + swapped * sin


def _mm_kernel(a_ref, w_ref, b_ref, o_ref):
    a = _silu(a_ref[...])
    o_ref[...] = _dot(a.astype(BF16), w_ref[...].astype(BF16)) + b_ref[...]


def _silu_matmul(a, w, bias, *, tn):
    m, k = a.shape
    n = w.shape[1]
    assert n % tn == 0
    return pl.pallas_call(
        _mm_kernel,
        grid=(n // tn,),
        in_specs=[pl.BlockSpec((m, k), lambda j: (0, 0)),
                  pl.BlockSpec((k, tn), lambda j: (0, j)),
                  pl.BlockSpec((1, tn), lambda j: (0, j))],
        out_specs=pl.BlockSpec((m, tn), lambda j: (0, j)),
        out_shape=jax.ShapeDtypeStruct((m, n), F32),
        compiler_params=_cparams("parallel"),
        name="silu_matmul",
    )(a, w, bias.reshape(1, n))


def _proj_kernel(*refs, groups, sub, has_rope):
    a_ref, w_ref = refs[0], refs[1]
    pos = 2
    if has_rope:
        cos_ref, sin_ref = refs[2], refs[3]
        pos = 4
    params = []
    for _, epilogue, _ in groups:
        params.append(refs[pos] if epilogue in ("log_forget", "ret_k") else None)
        pos += epilogue in ("log_forget", "ret_k")
    outs = refs[pos:]
    a = a_ref[...]
    c0 = 0
    oi = 0
    for (n_cols, epilogue, _), pr in zip(groups, params):
        for s0 in range(0, n_cols, sub):
            cols = slice(s0, s0 + sub)
            acc = _dot(a, w_ref[:, c0 + s0:c0 + s0 + sub])
            if epilogue is None:
                outs[oi][:, cols] = acc.astype(outs[oi].dtype)
            elif epilogue == "silu":
                outs[oi][:, cols] = _silu(acc).astype(outs[oi].dtype)
            elif epilogue == "log_forget":
                log_sig = jnp.minimum(acc, 0.0) - jnp.log(1.0 + jnp.exp(-jnp.abs(acc)))
                log_lb = pr[0:1, cols]
                bb = pr[1:2, cols] + log_sig
                outs[oi][:, cols] = jnp.maximum(log_lb, bb) + jnp.log(1.0 + jnp.exp(-jnp.abs(log_lb - bb)))
            else:
                reps = sub // cos_ref.shape[1]
                r = _rope(acc, jnp.concatenate([cos_ref[...]] * reps, axis=1), jnp.concatenate([sin_ref[...]] * reps, axis=1))
                if epilogue == "rope":
                    outs[oi][:, cols] = r.astype(outs[oi].dtype)
                else:
                    k = r * (RET_QK_DIM ** -0.5)
                    ipos = (lax.broadcasted_iota(jnp.int32, k.shape, 0) & (RET_CHUNK - 1)).astype(F32)
                    outs[oi][:, cols] = k.astype(BF16)
                    outs[oi + 1][:, cols] = (k * jnp.exp((RET_CHUNK - 1.0 - ipos) * pr[0:1, cols])).astype(BF16)
                    outs[oi + 2][:, cols] = (k * jnp.exp(ipos * pr[1:2, cols])).astype(BF16)
        c0 += n_cols
        oi += 3 if epilogue == "ret_k" else 1


def _project(a, w, groups, *, tm, params=(), rope=None, sub=512):
    n_rows, k = a.shape
    assert n_rows % tm == 0 and sum(g[0] for g in groups) == w.shape[1] and all(g[0] % sub == 0 for g in groups)
    in_specs = [pl.BlockSpec((tm, k), lambda i: (i, 0)), pl.BlockSpec(w.shape, lambda i: (0, 0))]
    args = [a, w]
    if rope is not None:
        cos, sin, block_fn = rope
        in_specs += [pl.BlockSpec((tm, cos.shape[1]), lambda i: (block_fn(i), 0))] * 2
        args += [cos, sin]
    for pr in params:
        in_specs.append(pl.BlockSpec(pr.shape, lambda i: (0, 0)))
        args.append(pr)
    out_specs, out_shape = [], []
    for n_cols, epilogue, dtype in groups:
        for _ in range(3 if epilogue == "ret_k" else 1):
            out_specs.append(pl.BlockSpec((tm, n_cols), lambda i: (i, 0)))
            out_shape.append(jax.ShapeDtypeStruct((n_rows, n_cols), dtype))
    return pl.pallas_call(
        functools.partial(_proj_kernel, groups=tuple(groups), sub=sub, has_rope=rope is not None),
        grid=(n_rows // tm,),
        in_specs=in_specs,
        out_specs=out_specs,
        out_shape=out_shape,
        compiler_params=_cparams("parallel"),
        name="project",
    )(*args)


def _mix_out_kernel(*refs, n_parts, n_x_tiles, has_ctx, tail):
    n_in = n_parts * (2 if has_ctx else 1)
    w_ref = refs[n_in]
    is_ctx = pl.program_id(0) >= n_x_tiles
    acc = None
    k0 = 0
    for p in range(n_parts):
        if has_ctx:
            a = jnp.where(is_ctx, refs[2 * p + 1][...], refs[2 * p][...])
        else:
            a = refs[p][...]
        k1 = k0 + a.shape[1]
        part = _dot(a, w_ref[k0:k1, :])
        acc = part if acc is None else acc + part
        k0 = k1
    tail(acc, *refs[n_in + 1:])


def _mix_out(parts_x, parts_c, w, x, mod, lnp, router, *, alpha, rows_per_batch, tm=ROW_TILE):
    n_x = parts_x[0].shape[0]
    has_ctx = parts_c is not None
    n_c = parts_c[0].shape[0] if has_ctx else 0
    assert n_x % tm == 0 and n_c % tm == 0
    n_x_tiles, n_c_tiles = n_x // tm, n_c // tm
    in_specs, args = [], []
    for p, ax in enumerate(parts_x):
        in_specs.append(pl.BlockSpec((tm, ax.shape[1]), lambda i: (jnp.minimum(i, n_x_tiles - 1), 0)))
        args.append(ax)
        if has_ctx:
            in_specs.append(pl.BlockSpec((tm, ax.shape[1]), lambda i: (jnp.maximum(i - n_x_tiles, 0), 0)))
            args.append(parts_c[p])
    in_specs.append(pl.BlockSpec(w.shape, lambda i: (0, 0)))
    args.append(w)
    tail, t_in, t_args, out_specs, out_shape = _ln_tail_specs(
        x, mod, lnp, router, alpha=alpha, tm=tm, rows_per_batch=rows_per_batch, n_rows=n_x + n_c, row_of=lambda i: i)
    return pl.pallas_call(
        functools.partial(_mix_out_kernel, n_parts=len(parts_x), n_x_tiles=n_x_tiles, has_ctx=has_ctx, tail=tail),
        grid=(n_x_tiles + n_c_tiles,),
        in_specs=in_specs + t_in,
        out_specs=out_specs,
        out_shape=out_shape,
        compiler_params=_cparams("parallel"),
        name="mix_out",
    )(*args, *t_args)


def _modulate_kernel(x_ref, mod_ref, h_ref):
    m = mod_ref[0]
    h_ref[...] = (x_ref[...] * (1.0 + m[1:2]) + m[0:1]).astype(h_ref.dtype)


def _mod_index(tiles_per_batch, n_batch):
    return lambda i: (jnp.minimum(i // tiles_per_batch, n_batch), 0, 0)


def _modulate(x, mod, *, rows_per_batch, tm=ROW_TILE):
    m, d = x.shape
    return pl.pallas_call(
        _modulate_kernel,
        grid=(m // tm,),
        in_specs=[pl.BlockSpec((tm, d), lambda i: (i, 0)),
                  pl.BlockSpec((1, 2, d), _mod_index(rows_per_batch // tm, mod.shape[0] - 1))],
        out_specs=pl.BlockSpec((tm, d), lambda i: (i, 0)),
        out_shape=jax.ShapeDtypeStruct((m, d), BF16),
        compiler_params=_cparams("parallel"),
        name="modulate",
    )(x, mod)


def _ln_tail(y, x_ref, mod_ref, lnp_ref, *rest, alpha, router, n_experts):
    if router:
        rw_ref, rb_ref, xo_ref, h_ref, g_ref, i_ref = rest
    else:
        xo_ref, h_ref = rest
    m = mod_ref[0]
    z = alpha * x_ref[...] + m[0:1] * y
    mu = jnp.mean(z, axis=-1, keepdims=True)
    zc = z - mu
    var = jnp.mean(zc * zc, axis=-1, keepdims=True)
    xn = zc * lax.rsqrt(var + LN_EPS) * lnp_ref[0:1, :] + lnp_ref[1:2, :]
    xo_ref[...] = xn
    hf = xn * (1.0 + m[2:3]) + m[1:2]
    h_ref[...] = hf.astype(h_ref.dtype)
    if router:
        logits = _dot_3pass(hf, rw_ref[...]) + rb_ref[...]
        lane = lax.broadcasted_iota(jnp.int32, logits.shape, 1).astype(F32)
        neg = -jnp.inf
        lg = jnp.where(lane < n_experts, logits, neg)
        m1 = jnp.max(lg, axis=-1, keepdims=True)
        i1 = jnp.min(jnp.where(lg == m1, lane, float(LANES)), axis=-1, keepdims=True)
        lg2 = jnp.where(lane == i1, neg, lg)
        m2 = jnp.max(lg2, axis=-1, keepdims=True)
        i2 = jnp.min(jnp.where(lg2 == m2, lane, float(LANES)), axis=-1, keepdims=True)
        e2 = jnp.exp(m2 - m1)
        den = 1.0 + e2
        g_ref[...] = jnp.where(lane == 0, 1.0 / den, jnp.where(lane == 1, e2 / den, 0.0))
        i_ref[...] = jnp.where(lane == 0, i1, jnp.where(lane == 1, i2, 0.0)).astype(jnp.int32)


def _ln_tail_specs(x, mod, lnp, router, *, alpha, tm, rows_per_batch, n_rows, row_of):
    d = x.shape[1]
    tiles_per_batch, n_batch = rows_per_batch // tm, mod.shape[0] - 1
    row = pl.BlockSpec((tm, d), lambda *g: (row_of(*g), 0))
    lane_row = pl.BlockSpec((tm, LANES), lambda *g: (row_of(*g), 0))
    const = lambda shape: pl.BlockSpec(shape, lambda *g: (0,) * len(shape))
    in_specs = [row, pl.BlockSpec((1, 3, d), lambda *g: (jnp.minimum(row_of(*g) // tiles_per_batch, n_batch), 0, 0)),
                const((2, d))]
    args = [x, mod, lnp]
    out_specs = [row, row]
    out_shape = [jax.ShapeDtypeStruct((n_rows, d), F32), jax.ShapeDtypeStruct((n_rows, d), BF16 if router is None else F32)]
    if router is not None:
        in_specs += [const((d, LANES)), const((1, LANES))]
        args += list(router)
        out_specs += [lane_row, lane_row]
        out_shape += [jax.ShapeDtypeStruct((n_rows, LANES), F32), jax.ShapeDtypeStruct((n_rows, LANES), jnp.int32)]
    tail = functools.partial(_ln_tail, alpha=alpha, router=router is not None, n_experts=N_EXPERTS)
    return tail, in_specs, args, out_specs, out_shape


def _ln_kernel(*refs, two_y, tail):
    if two_y:
        y_ref, y2_ref, yg_ref = refs[:3]
        yg = yg_ref[...]
        y = yg[:, 0:1] * y_ref[...].astype(F32) + yg[:, 1:2] * y2_ref[...].astype(F32)
        refs = refs[3:]
    else:
        y = refs[0][...].astype(F32)
        refs = refs[1:]
    tail(y, *refs)


def _residual_ln(x, y, mod, lnp, *, alpha, rows_per_batch, n_rows=None, y_gates=None, router=None, tm=ROW_TILE):
    d = x.shape[1]
    n_rows = x.shape[0] if n_rows is None else n_rows
    row = pl.BlockSpec((tm, d), lambda i: (i, 0))
    two_y = y_gates is not None
    in_specs, args = [row], [y]
    if two_y:
        second_half = n_rows // tm
        in_specs += [pl.BlockSpec((tm, d), lambda i: (second_half + i, 0)), pl.BlockSpec((tm, LANES), lambda i: (i, 0))]
        args += [y, y_gates]
    tail, t_in, t_args, out_specs, out_shape = _ln_tail_specs(
        x, mod, lnp, router, alpha=alpha, tm=tm, rows_per_batch=rows_per_batch, n_rows=n_rows, row_of=lambda i: i)
    return pl.pallas_call(
        functools.partial(_ln_kernel, two_y=two_y, tail=tail),
        grid=(n_rows // tm,),
        in_specs=in_specs + t_in,
        out_specs=out_specs,
        out_shape=out_shape,
        compiler_params=_cparams("parallel"),
        name="residual_ln",
    )(*args, *t_args)


def _gffn_kernel(te_ref, tv_ref, x_ref, w1_ref, w3_ref, w2_ref, *rest, n_f, tail):
    acc_scr = rest[-1]
    i, f = pl.program_id(0), pl.program_id(1)

    @pl.when(f == 0)
    def _():
        acc_scr[...] = jnp.zeros_like(acc_scr)

    @pl.when(tv_ref[i] != 0)
    def _():
        x = x_ref[...].astype(BF16)
        h1 = _dot(x, w1_ref[0, 0])
        h3 = _dot(x, w3_ref[0, 0])
        a = (_silu(h1) * h3).astype(BF16)
        acc_scr[...] += _dot(a, w2_ref[0, 0])

    @pl.when(f == n_f - 1)
    def _():
        if tail is None:
            rest[0][...] = acc_scr[...]
        else:
            tail(acc_scr[...], *rest[:-1])


def _grouped_swiglu(xs, w1, w3, w2, layer, tile_expert, tile_valid, *, tm, tf, ln=None):
    p, d = xs.shape
    f = w1.shape[3]
    assert p % tm == 0 and f % tf == 0
    in_specs = [pl.BlockSpec((tm, d), lambda i, j, te, tv: (i, 0)),
                pl.BlockSpec((1, 1, d, tf), lambda i, j, te, tv: (layer, te[i], 0, j * tv[i])),
                pl.BlockSpec((1, 1, d, tf), lambda i, j, te, tv: (layer, te[i], 0, j * tv[i])),
                pl.BlockSpec((1, 1, tf, d), lambda i, j, te, tv: (layer, te[i], j * tv[i], 0))]
    args = [xs, w1, w3, w2]
    if ln is None:
        tail = None
        out_specs = pl.BlockSpec((tm, d), lambda i, j, te, tv: (i, 0))
        out_shape = jax.ShapeDtypeStruct((p, d), F32)
    else:
        tail, t_in, t_args, out_specs, out_shape = _ln_tail_specs(
            ln["x"], ln["mod"], ln["lnp"], None, alpha=ln["alpha"], tm=tm, rows_per_batch=ln["rows_per_batch"],
            n_rows=p, row_of=lambda i, j, te, tv: i)
        in_specs += t_in
        args += t_args
    grid_spec = pltpu.PrefetchScalarGridSpec(
        num_scalar_prefetch=2,
        grid=(p // tm, f // tf),
        in_specs=in_specs,
        out_specs=out_specs,
        scratch_shapes=[pltpu.VMEM((tm, d), F32)],
    )
    return pl.pallas_call(
        functools.partial(_gffn_kernel, n_f=f // tf, tail=tail),
        grid_spec=grid_spec,
        out_shape=out_shape,
        compiler_params=_cparams("parallel", "arbitrary"),
        name="grouped_swiglu",
    )(tile_expert, tile_valid, *args)


def _route(idx, n_experts, tm):
    m = idx.shape[0]
    flat = idx.reshape(-1)
    n = flat.shape[0]
    p = -(-(n + n_experts * (tm - 1)) // tm) * tm
    onehot = (flat[:, None] == jnp.arange(n_experts, dtype=jnp.int32)[None, :]).astype(jnp.int32)
    csum = jnp.cumsum(onehot, axis=0)
    counts = csum[-1]
    padded = -(-counts // tm) * tm
    pad_end = jnp.cumsum(padded)
    pad_off = pad_end - padded
    rank = jnp.sum(csum * onehot, axis=1) - 1
    assign_slot = (pad_off[flat] + rank).reshape(m, 2)
    slot = jnp.arange(p, dtype=jnp.int32)
    slot_expert = jnp.minimum(jnp.sum(pad_end[None, :] <= slot[:, None], axis=1), n_experts - 1).astype(jnp.int32)
    slot_rank = slot - pad_off[slot_expert]
    order = jnp.argsort(flat, stable=True).astype(jnp.int32)
    off = jnp.cumsum(counts) - counts
    src = order[jnp.minimum(off[slot_expert] + slot_rank, n - 1)]
    slot_token = jnp.where(slot_rank < counts[slot_expert], src // 2, slot % m).astype(jnp.int32)
    tile_start = slot[::tm]
    tile_valid = (tile_start < pad_end[-1]).astype(jnp.int32)
    tile_expert = slot_expert[::tm]
    return slot_token, assign_slot, tile_expert, tile_valid


def _softmax_pv(parts):
    mx = functools.reduce(jnp.maximum, [jnp.max(s, axis=-1, keepdims=True) for s, _ in parts])
    ps = [jnp.exp(s - mx) for s, _ in parts]
    den = functools.reduce(jnp.add, [jnp.sum(p, axis=-1, keepdims=True) for p in ps])
    num = functools.reduce(jnp.add, [_dot(p.astype(BF16), v) for p, (_, v) in zip(ps, parts)])
    return num / den


def _na_kernel(*refs, rows, gw, ctx_out):
    if ctx_out:
        q_ref, k_ref, v_ref, kc_ref, vc_ref, qc_ref, bias_ref, o_ref, oc_ref, s_scr, p_scr, l_scr = refs
    else:
        q_ref, k_ref, v_ref, kc_ref, vc_ref, bias_ref, o_ref, s_scr, p_scr, l_scr = refs
    lane = lax.broadcasted_iota(jnp.int32, (1, LANES), 1)
    head_lanes = [lane < NA_HEAD_DIM, lane >= NA_HEAD_DIM]
    scale = NA_HEAD_DIM ** -0.5
    band = NA_KH * gw

    def band_rows(r):
        b0 = jnp.clip(r - NA_KH // 2, 0, rows - NA_KH)
        return b0, pl.ds(pl.multiple_of(b0 * gw, gw), band)

    def scores(r, par):
        r = jnp.minimum(r, rows - 1)
        b0, ks = band_rows(r)
        dr0 = b0 - r + NA_KH - 1
        q = q_ref[pl.ds(pl.multiple_of(r * gw, gw), gw), :]
        kb = k_ref[ks, :]
        for h in range(2):
            qh = jnp.where(head_lanes[h], q, 0) * scale
            s_scr[par, h, :, :band] = _dot_nt(qh, kb) + bias_ref[h, dr0]
            s_scr[par, h, :, band:] = _dot_nt(qh, kc_ref[...])

    def softmax(par):
        for h in range(2):
            s = s_scr[par, h]
            e = jnp.exp(s - jnp.max(s, axis=-1, keepdims=True))
            l_scr[par, h] = jnp.broadcast_to(jnp.sum(e, axis=-1, keepdims=True), (gw, LANES))
            p_scr[par, h] = e.astype(BF16)

    def weighted_values(r, par):
        _, ks = band_rows(r)
        vb = v_ref[ks, :]
        outs = []
        for h in range(2):
            num = _dot(p_scr[par, h, :, :band], vb) + _dot(p_scr[par, h, :, band:], vc_ref[...])
            outs.append(num / l_scr[par, h])
        o_ref[pl.ds(pl.multiple_of(r * gw, gw), gw), :] = jnp.where(head_lanes[0], outs[0], outs[1]).astype(o_ref.dtype)

    scores(0, 0)
    scores(1, 1)
    softmax(0)

    def body(i, carry):
        r = 2 * i
        weighted_values(r, 0)
        softmax(1)
        scores(r + 2, 0)
        weighted_values(r + 1, 1)
        softmax(0)
        scores(r + 3, 1)
        return carry

    lax.fori_loop(0, rows // 2, body, 0)
    if ctx_out:
        qc = qc_ref[...]
        kc = kc_ref[...]
        vc = vc_ref[...]
        outs = []
        for h in range(2):
            qh = jnp.where(head_lanes[h], qc, 0) * scale
            outs.append(_softmax_pv([(_dot_nt(qh, kc), vc)]))
        oc_ref[...] = jnp.where(head_lanes[0], outs[0], outs[1]).astype(oc_ref.dtype)


def _na_bias_table(rpb, gw):
    col = jnp.arange(gw)
    c_start = jnp.clip(col - NA_KW // 2, 0, gw - NA_KW)
    col_in = (col[None, :] >= c_start[:, None]) & (col[None, :] < c_start[:, None] + NA_KW)
    dc = jnp.clip(col[None, :] - col[:, None], 1 - NA_KW, NA_KW - 1) + NA_KW - 1
    t = jnp.where(col_in[None, None], rpb[:, :, dc].astype(F32), MASK_VALUE)
    win = jnp.stack([t[:, d:d + NA_KH] for d in range(NA_KH)], axis=1)
    return win.transpose(0, 1, 3, 2, 4).reshape(rpb.shape[0], NA_KH, gw, NA_KH * gw)


def _neighbourhood_attention(proj, rpb, *, n_batch, seq, ctx_len, ctx_out):
    width = rpb.shape[0] * NA_HEAD_DIM
    pairs = width // LANES
    rows = seq // GRID_W
    bias = _na_bias_table(rpb, GRID_W)
    n_keys = NA_KH * GRID_W + ctx_len
    cblk0 = n_batch * seq // ctx_len
    xspec = lambda g: pl.BlockSpec((seq, LANES), lambda b, p: (b, g * pairs + p))
    cspec = lambda g: pl.BlockSpec((ctx_len, LANES), lambda b, p: (cblk0 + b, g * pairs + p))
    in_specs = [xspec(0), xspec(1), xspec(2), cspec(1), cspec(2)]
    args = [proj, proj, proj, proj, proj]
    if ctx_out:
        in_specs.append(cspec(0))
        args.append(proj)
    in_specs.append(pl.BlockSpec((2, NA_KH, GRID_W, NA_KH * GRID_W), lambda b, p: (p, 0, 0, 0)))
    args.append(bias)
    out_specs = [pl.BlockSpec((seq, LANES), lambda b, p: (b, p))]
    out_shape = [jax.ShapeDtypeStruct((n_batch * seq, width), BF16)]
    if ctx_out:
        out_specs.append(pl.BlockSpec((ctx_len, LANES), lambda b, p: (b, p)))
        out_shape.append(jax.ShapeDtypeStruct((n_batch * ctx_len, width), BF16))
    return pl.pallas_call(
        functools.partial(_na_kernel, rows=rows, gw=GRID_W, ctx_out=ctx_out),
        grid=(n_batch, pairs),
        in_specs=in_specs,
        out_specs=out_specs,
        out_shape=out_shape,
        scratch_shapes=[pltpu.VMEM((2, 2, GRID_W, n_keys), F32),
                        pltpu.VMEM((2, 2, GRID_W, n_keys), BF16),
                        pltpu.VMEM((2, 2, GRID_W, LANES), F32)],
        compiler_params=_cparams("parallel", "parallel"),
        name="neighbourhood_attention",
    )(*args)


def _hg_kernel(*refs, rev, n_chunks, n_blocks, epilogue):
    if epilogue:
        q_ref, f_ref, i_ref, s0_ref, pm_ref, rm_ref, g_ref, prev_ref, ng_ref, o_ref, sT_ref, st_scr = refs
    else:
        q_ref, f_ref, i_ref, s0_ref, pm_ref, rm_ref, o_ref, sT_ref, st_scr = refs
    blk = pl.program_id(2)

    @pl.when(blk == 0)
    def _():
        st_scr[...] = s0_ref[0, 0]

    c = HG_CHUNK

    def chunk(ci, carry):
        cc = (n_chunks - 1 - ci) if rev else ci
        sl = pl.ds(pl.multiple_of(cc * c, c), c)
        q = q_ref[sl, :].astype(F32)
        lf = f_ref[sl, :] * LOG2_E
        k = 1.0 - jnp.exp2(lf)
        v = i_ref[sl, :]
        p, tot = lf, lf
        att = pm_ref[0] * jnp.sum(q * k, axis=-1, keepdims=True)
        for lvl in range(1, pm_ref.shape[0]):
            m = 1 << (lvl - 1)
            second = rm_ref[lvl - 1] != 0.0
            is_q = jnp.logical_not(second) if rev else second
            z = (jnp.where(is_q, q, k) * jnp.exp2(jnp.where(is_q, p, tot - p))).astype(BF16)
            att = att + pm_ref[lvl] * _dot_nt(z, z)
            t_up = pltpu.roll(tot, m, 0)
            t_dn = pltpu.roll(tot, c - m, 0)
            p = p + (jnp.where(second, 0.0, t_dn) if rev else jnp.where(second, t_up, 0.0))
            tot = tot + jnp.where(second, t_up, t_dn)
        st = st_scr[...]
        o = _dot(att.astype(BF16), v) + _dot_nt((q * jnp.exp2(p)).astype(BF16), st.astype(BF16))
        ku = (k * jnp.exp2(tot - p)).astype(BF16)
        st_scr[...] = st * jnp.exp2(tot[0:1, :]) + _dot_tn(v, ku)
        if epilogue:
            o = o + prev_ref[sl, :]
            o = o * lax.rsqrt(jnp.mean(o * o, axis=-1, keepdims=True) + NORM_EPS) * ng_ref[...]
            o = o * g_ref[sl, :].astype(F32)
        o_ref[sl, :] = o.astype(o_ref.dtype)
        return carry

    lax.fori_loop(0, n_chunks, chunk, 0)

    @pl.when(blk == n_blocks - 1)
    def _():
        sT_ref[0, 0] = st_scr[...]


def _hg_level_masks(rev):
    c = HG_CHUNK
    assert c == HG_KEY_DIM
    t = jnp.arange(c)[:, None]
    s = jnp.arange(c)[None, :]
    pair, row = [t == s], []
    m = 1
    while m < c:
        t_second = (t & m) != 0
        s_second = (s & m) != 0
        same = (t ^ s) < 2 * m
        pair.append(same & (~t_second & s_second if rev else t_second & ~s_second))
        row.append(jnp.broadcast_to(t_second, (c, c)))
        m *= 2
    return jnp.stack(pair).astype(F32), jnp.stack(row).astype(F32)


def _hg_scan(qg, lf, proj, s0, *, rev, row0, n_batch, seq, block_rows, n_heads, col_f, col_i, prev=None, norm_g=None):
    dk = HG_KEY_DIM
    n_blocks = seq // block_rows
    blk0 = row0 // block_rows
    epilogue = prev is not None

    def local_rows(b, i):
        return b * n_blocks + ((n_blocks - 1 - i) if rev else i)

    pspec = lambda col: pl.BlockSpec((block_rows, dk), lambda b, h, i: (blk0 + local_rows(b, i), col + h))
    state_spec = pl.BlockSpec((1, 1, dk, dk), lambda b, h, i: (b, h, 0, 0))
    local_spec = pl.BlockSpec((block_rows, dk), lambda b, h, i: (local_rows(b, i), h))
    pair_mask, row_mask = _hg_level_masks(rev)
    const_spec = lambda a: pl.BlockSpec(a.shape, lambda b, h, i: (0, 0, 0))
    in_specs = [pspec(0), pspec(col_f), pspec(col_i), state_spec, const_spec(pair_mask), const_spec(row_mask)]
    args = [qg, lf, proj, s0, pair_mask, row_mask]
    if epilogue:
        in_specs += [pspec(n_heads), local_spec, pl.BlockSpec((1, dk), lambda b, h, i: (0, 0))]
        args += [qg, prev, norm_g.reshape(1, dk)]
    return pl.pallas_call(
        functools.partial(_hg_kernel, rev=rev, n_chunks=block_rows // HG_CHUNK, n_blocks=n_blocks, epilogue=epilogue),
        grid=(n_batch, n_heads, n_blocks),
        in_specs=in_specs,
        out_specs=[local_spec, state_spec],
        out_shape=[jax.ShapeDtypeStruct((n_batch * seq, n_heads * dk), BF16 if epilogue else F32),
                   jax.ShapeDtypeStruct((n_batch, n_heads, dk, dk), F32)],
        scratch_shapes=[pltpu.VMEM((dk, dk), F32)],
        compiler_params=_cparams("parallel", "parallel", "arbitrary"),
        name="hgrn2_scan_rev" if rev else "hgrn2_scan_fwd",
    )(*args)


def _ret_kernel(*refs, fwd, n_chunks, n_blocks):
    if fwd:
        q_ref, k_ref, kd_ref, v_ref, lam_ref, s0_ref, g_ref, prev_ref, o_ref, sT_ref, st_scr, qd_scr, dm_scr = refs
    else:
        q_ref, kd_ref, v_ref, lam_ref, s0_ref, o_ref, sT_ref, st_scr, qd_scr = refs
    blk = pl.program_id(2)
    c = RET_CHUNK
    lam_f = lam_ref[0, 0:1, :]
    lam_b = lam_ref[0, 1:2, :]
    lam = lam_f if fwd else lam_b

    @pl.when(blk == 0)
    def _():
        st_scr[...] = s0_ref[0, 0]
        ipos = lax.broadcasted_iota(jnp.int32, (c, LANES), 0).astype(F32)
        steps = (ipos + 1.0) if fwd else (c - ipos)
        qd_scr[...] = jnp.exp(steps * lam[:, :LANES])
        if fwd:
            dist = (lax.broadcasted_iota(jnp.int32, (c, c), 0) - lax.broadcasted_iota(jnp.int32, (c, c), 1)).astype(F32)
            dm_scr[...] = (jnp.where(dist >= 0, jnp.exp(jnp.maximum(dist, 0.0) * lam_f[:, :c]), 0.0)
                           + jnp.where(dist <= 0, jnp.exp(jnp.maximum(-dist, 0.0) * lam_b[:, :c]), 0.0))

    chunk_decay = jnp.exp(float(c) * lam[:, 0:1])

    def chunk(ci, carry):
        cc = ci if fwd else (n_chunks - 1 - ci)
        sl = pl.ds(pl.multiple_of(cc * c, c), c)
        q = q_ref[sl, :]
        v = v_ref[sl, :]
        st = st_scr[...]
        o = _dot(q, st.astype(BF16)) * qd_scr[:, 0:1]
        st_scr[...] = st * chunk_decay + _dot_tn(kd_ref[sl, :], v)
        if fwd:
            att = _dot_nt(q, k_ref[sl, :]) * dm_scr[...]
            o = o + _dot(att.astype(BF16), v) + prev_ref[sl, :].astype(F32)
            mu = jnp.mean(o, axis=-1, keepdims=True)
            oc = o - mu
            var = jnp.mean(oc * oc, axis=-1, keepdims=True)
            o = oc * lax.rsqrt(var + LN_EPS) * g_ref[sl, :].astype(F32)
        o_ref[sl, :] = o.astype(o_ref.dtype)
        return carry

    lax.fori_loop(0, n_chunks, chunk, 0)

    @pl.when(blk == n_blocks - 1)
    def _():
        sT_ref[0, 0] = st_scr[...]


def _ret_scan(q, k, kd, v, lam, s0, *, fwd, row0, n_batch, seq, block_rows, n_heads, gate=None, prev=None):
    dk, dv = RET_QK_DIM, RET_V_DIM
    n_blocks = seq // block_rows
    blk0 = row0 // block_rows
    assert RET_CHUNK <= dk

    def lrow(b, i):
        return b * n_blocks + (i if fwd else (n_blocks - 1 - i))

    kspec = pl.BlockSpec((block_rows, dk), lambda b, h, i: (blk0 + lrow(b, i), h))
    vspec = pl.BlockSpec((block_rows, dv), lambda b, h, i: (blk0 + lrow(b, i), h))
    local_spec = pl.BlockSpec((block_rows, dv), lambda b, h, i: (lrow(b, i), h))
    state_spec = pl.BlockSpec((1, 1, dk, dv), lambda b, h, i: (b, h, 0, 0))
    lam_spec = pl.BlockSpec((1, 2, dk), lambda b, h, i: (h, 0, 0))
    scratch = [pltpu.VMEM((dk, dv), F32), pltpu.VMEM((RET_CHUNK, LANES), F32)]
    if fwd:
        in_specs = [kspec, kspec, kspec, vspec, lam_spec, state_spec, vspec, local_spec]
        args = [q, k, kd, v, lam, s0, gate, prev]
        scratch.append(pltpu.VMEM((RET_CHUNK, RET_CHUNK), F32))
    else:
        in_specs = [kspec, kspec, vspec, lam_spec, state_spec]
        args = [q, kd, v, lam, s0]
    return pl.pallas_call(
        functools.partial(_ret_kernel, fwd=fwd, n_chunks=block_rows // RET_CHUNK, n_blocks=n_blocks),
        grid=(n_batch, n_heads, n_blocks),
        in_specs=in_specs,
        out_specs=[local_spec, state_spec],
        out_shape=[jax.ShapeDtypeStruct((n_batch * seq, n_heads * dv), BF16),
                   jax.ShapeDtypeStruct((n_batch, n_heads, dk, dv), F32)],
        scratch_shapes=scratch,
        compiler_params=_cparams("parallel", "parallel", "arbitrary"),
        name="retention_fwd" if fwd else "retention_rev",
    )(*args)


def _rope_tables(seq, gw, pad_rows):
    half = LANES // 2
    inv = ROPE_BASE ** (-jnp.arange(half, dtype=F32) / half)
    t = jnp.arange(seq)
    ang_r = (t // gw).astype(F32)[:, None] * inv
    ang_c = (t % gw).astype(F32)[:, None] * inv
    cos = jnp.concatenate([jnp.cos(ang_r)] * 2 + [jnp.cos(ang_c)] * 2, axis=-1)
    sin = jnp.concatenate([-jnp.sin(ang_r), jnp.sin(ang_r), -jnp.sin(ang_c), jnp.sin(ang_c)], axis=-1)
    cos = jnp.concatenate([cos, jnp.ones((pad_rows, cos.shape[1]), F32)], axis=0)
    sin = jnp.concatenate([sin, jnp.zeros((pad_rows, sin.shape[1]), F32)], axis=0)
    return cos, sin


def _even_mixer(h, w_in, rpb, lb, norm_g, *, n_batch, seq, ctx_len, ctx_out, tm):
    n_x = n_batch * seq
    na_w = rpb.shape[0] * NA_HEAD_DIM
    hg_w = lb.shape[1]
    n_heads = hg_w // HG_KEY_DIM
    cuts = [0, 3 * na_w] + [3 * na_w + i * hg_w for i in range(1, 6)]
    seg = lambda i: w_in[:, cuts[i]:cuts[i + 1]]
    w = jnp.concatenate([seg(0), seg(4), seg(1), seg(5), seg(2), seg(3)], axis=1).astype(BF16)
    lb_rows = jnp.stack([jnp.log(lb).reshape(-1), jnp.log1p(-lb).reshape(-1)])
    groups = [(3 * na_w + hg_w, None, BF16),
              (2 * hg_w, "silu", BF16),
              (2 * hg_w, "log_forget", F32)]
    proj, qg, lf = _project(h, w, groups, tm=tm, params=[lb_rows])
    na = _neighbourhood_attention(proj, rpb, n_batch=n_batch, seq=seq, ctx_len=ctx_len, ctx_out=ctx_out)
    zeros = jnp.zeros((n_batch, n_heads, HG_KEY_DIM, HG_KEY_DIM), F32)
    col_i = 3 * na_w // HG_KEY_DIM
    ctx_kw = dict(row0=n_x, n_batch=n_batch, seq=ctx_len, block_rows=ctx_len, n_heads=n_heads, col_i=col_i)
    x_kw = dict(row0=0, n_batch=n_batch, seq=seq, block_rows=min(SCAN_BLOCK, seq), n_heads=n_heads, col_i=col_i)
    oc_f, sc_f = _hg_scan(qg, lf, proj, zeros, rev=False, col_f=0, **ctx_kw)
    gc, sc_b = _hg_scan(qg, lf, proj, zeros, rev=True, col_f=n_heads, prev=oc_f, norm_g=norm_g, **ctx_kw)
    ox_f, _ = _hg_scan(qg, lf, proj, sc_f, rev=False, col_f=0, **x_kw)
    gx, _ = _hg_scan(qg, lf, proj, sc_b, rev=True, col_f=n_heads, prev=ox_f, norm_g=norm_g, **x_kw)
    return [na[0], gx], ([na[1], gc] if ctx_out else None)


def _odd_mixer(h, w_in, log_decay, rope, *, n_batch, seq, ctx_len, ctx_out, tm):
    n_x = n_batch * seq
    n_heads = log_decay.shape[1]
    qk_w = n_heads * RET_QK_DIM
    v_w = n_heads * RET_V_DIM
    lam = jnp.broadcast_to(log_decay.astype(F32).T[:, :, None], (n_heads, 2, RET_QK_DIM))
    lam_rows = lam.transpose(1, 0, 2).reshape(2, qk_w)
    groups = [(qk_w, "rope", BF16), (qk_w, "ret_k", BF16), (v_w, None, BF16), (v_w, "silu", BF16)]
    q, k, kf, kb, v, g = _project(h, w_in.astype(BF16), groups, tm=tm, params=[lam_rows], rope=rope)
    zeros = jnp.zeros((n_batch, n_heads, RET_QK_DIM, RET_V_DIM), F32)
    ctx_kw = dict(row0=n_x, n_batch=n_batch, seq=ctx_len, block_rows=ctx_len, n_heads=n_heads)
    x_kw = dict(row0=0, n_batch=n_batch, seq=seq, block_rows=min(SCAN_BLOCK, seq), n_heads=n_heads)
    oc_b, sc_b = _ret_scan(q, k, kb, v, lam, zeros, fwd=False, **ctx_kw)
    yc, sc_f = _ret_scan(q, k, kf, v, lam, zeros, fwd=True, gate=g, prev=oc_b, **ctx_kw)
    ox_b, _ = _ret_scan(q, k, kb, v, lam, sc_b, fwd=False, **x_kw)
    yx, _ = _ret_scan(q, k, kf, v, lam, sc_f, fwd=True, gate=g, prev=ox_b, **x_kw)
    return [yx], ([yc] if ctx_out else None)


def kernel(x, c, ctx, c_ctx, ada_w, ada_b, ln_g, ln_b, e_w_in, e_w_out, na_rpb, hg_lb_logits, hg_norm_g, ffn_w1, ffn_w3,
           ffn_w2, o_w_in, o_w_out, ret_log_decay, router_w, router_b, moe_w1, moe_w3, moe_w2):
    n_batch, seq, d = x.shape
    ctx_len = ctx.shape[1]
    depth = ada_w.shape[0]
    n_x = n_batch * seq
    n_all = n_x + n_batch * ctx_len
    alpha = (2 * depth) ** 0.25
    row_gcd = math.gcd(math.gcd(n_x, n_all), seq)
    tm_even = _pick_tile(row_gcd, EVEN_PROJ_TILE, unit=RET_CHUNK)
    tm_odd = _pick_tile(row_gcd, ODD_PROJ_TILE, unit=RET_CHUNK)
    dims = dict(n_batch=n_batch, seq=seq, ctx_len=ctx_len)

    lb_cum = jnp.cumsum(jax.nn.softmax(hg_lb_logits.astype(F32), axis=1), axis=1)
    lower_bounds = lb_cum - lb_cum[:, :1]
    cos, sin = _rope_tables(seq, GRID_W, tm_odd)
    tiles_per_seq = seq // tm_odd
    rope = (cos, sin, lambda i: jnp.where(i < n_x // tm_odd, i % tiles_per_seq, tiles_per_seq))

    cond = jnp.concatenate([c, c_ctx[None, :], jnp.zeros((8 - n_batch - 1, d), F32)], axis=0)
    mods = [_silu_matmul(cond, ada_w[l], ada_b[l], tn=d).reshape(8, 6, d)[:n_batch + 1] for l in range(depth)]

    ffn_w = [w.astype(BF16)[:, None] for w in (ffn_w1, ffn_w3, ffn_w2)]
    moe_w = [w.astype(BF16) for w in (moe_w1, moe_w3, moe_w2)]

    tok = jnp.concatenate([x.reshape(n_x, d), ctx.reshape(n_batch * ctx_len, d)], axis=0)
    h = _modulate(tok, mods[0][:, 0:2], rows_per_batch=seq)
    for layer in range(depth):
        j = layer // 2
        last = layer == depth - 1
        mod = mods[layer]
        n_rows = n_x if last else n_all
        if layer % 2 == 0:
            parts_x, parts_c = _even_mixer(h, e_w_in[j], na_rpb[j], lower_bounds[:, j], hg_norm_g[j], ctx_out=not last,
                                           tm=tm_even, **dims)
            w_out, router = e_w_out[j], None
        else:
            parts_x, parts_c = _odd_mixer(h, o_w_in[j], ret_log_decay[j], rope, ctx_out=not last, tm=tm_odd, **dims)
            rw = jnp.zeros((d, LANES), F32).at[:, :N_EXPERTS].set(router_w[j])
            rb = jnp.zeros((1, LANES), F32).at[0, :N_EXPERTS].set(router_b[j])
            w_out, router = o_w_out[j], (rw, rb)
        lnp = lambda i: jnp.stack([ln_g[layer, i], ln_b[layer, i]])
        res = _mix_out(parts_x, parts_c, w_out.astype(BF16), tok, mod[:, 2:5], lnp(0), router, alpha=alpha,
                       rows_per_batch=seq)
        tok, h = res[0], res[1]
        nxt = mods[layer + 1][:, 0:2] if not last else jnp.stack([jnp.zeros_like(mod[:, 0])] * 2, axis=1)
        mod2 = jnp.concatenate([mod[:, 5:6], nxt], axis=1)
        if layer % 2 == 0:
            ones = jnp.ones((n_rows // ROW_TILE,), jnp.int32)
            res = _grouped_swiglu(h, *ffn_w, j, 0 * ones, ones, tm=ROW_TILE, tf=_pick_tile(ffn_w1.shape[2], 1408),
                                  ln=dict(x=tok, mod=mod2, lnp=lnp(1), alpha=alpha, rows_per_batch=seq))
        else:
            gates, idx = res[2], res[3]
            slot_token, assign_slot, tile_expert, tile_valid = _route(idx[:, :2], N_EXPERTS, ROW_TILE)
            xs = jnp.take(h, slot_token, axis=0, mode="clip")
            ys = _grouped_swiglu(xs, *moe_w, j, tile_expert, tile_valid, tm=ROW_TILE,
                                 tf=_pick_tile(moe_w1.shape[3], 1792))
            y12 = jnp.take(ys, assign_slot.T.reshape(-1), axis=0, mode="clip")
            res = _residual_ln(tok, y12, mod2, lnp(1), alpha=alpha, rows_per_batch=seq, y_gates=gates)
        tok, h = res[0], res[1]
    return tok[:n_x].reshape(n_batch, seq, d)
```

```python
import functools
import math

import jax
import jax.numpy as jnp
from jax import lax
from jax.experimental import pallas as pl
from jax.experimental.pallas import tpu as pltpu

F32 = jnp.float32
BF16 = jnp.bfloat16

GRID_W = 64
NA_HEAD_DIM = 64
NA_KH = 8
NA_KW = 16
HG_KEY_DIM = 128
RET_QK_DIM = 256
RET_V_DIM = 512
N_EXPERTS = 8
ROPE_BASE = 10000.0
LN_EPS = 1e-5
NORM_EPS = 1e-6
LOG2_E = 1.4426950408889634
LANES = 128
HG_CHUNK = 128
RET_CHUNK = 256
ROW_TILE = 512
EVEN_PROJ_TILE = 512
ODD_PROJ_TILE = 512
SCAN_BLOCK = 2048
VMEM_LIMIT = 48 * 1024 * 1024
MASK_VALUE = -1e30


def _cparams(*sem):
    return pltpu.CompilerParams(dimension_semantics=sem, vmem_limit_bytes=VMEM_LIMIT)


def _dot(a, b):
    return jnp.dot(a, b, preferred_element_type=F32)


def _dot_nt(a, b):
    return lax.dot_general(a, b, (((1,), (1,)), ((), ())), preferred_element_type=F32)


def _dot_tn(a, b):
    return lax.dot_general(a, b, (((0,), (0,)), ((), ())), preferred_element_type=F32)


def _dot_3pass(a, b):
    a_hi = a.astype(BF16)
    b_hi = b.astype(BF16)
    a_lo = (a - a_hi.astype(F32)).astype(BF16)
    b_lo = (b - b_hi.astype(F32)).astype(BF16)
    return _dot(a_hi, b_hi) + _dot(a_lo, b_hi) + _dot(a_hi, b_lo)


def _silu(x):
    return x * jax.nn.sigmoid(x)


def _pick_tile(n, target, unit=LANES):
    best = None
    for t in range(unit, min(n, target) + 1, unit):
        if n % t == 0:
            best = t
    assert best is not None, (n, target, unit)
    return best


def _rope(x, cos, sin):
    half = LANES // 2
    swapped = jnp.concatenate([pltpu.roll(x[:, g * LANES:(g + 1) * LANES], half, 1) for g in range(x.shape[1] // LANES)],
                              axis=1)
    return x * cos + swapped * sin


def _mm_kernel(a_ref, w_ref, b_ref, o_ref):
    a = _silu(a_ref[...])
    o_ref[...] = _dot(a.astype(BF16), w_ref[...].astype(BF16)) + b_ref[...]


def _silu_matmul(a, w, bias, *, tn):
    m, k = a.shape
    n = w.shape[1]
    assert n % tn == 0
    return pl.pallas_call(
        _mm_kernel,
        grid=(n // tn,),
        in_specs=[pl.BlockSpec((m, k), lambda j: (0, 0)),
                  pl.BlockSpec((k, tn), lambda j: (0, j)),
                  pl.BlockSpec((1, tn), lambda j: (0, j))],
        out_specs=pl.BlockSpec((m, tn), lambda j: (0, j)),
        out_shape=jax.ShapeDtypeStruct((m, n), F32),
        compiler_params=_cparams("parallel"),
        name="silu_matmul",
    )(a, w, bias.reshape(1, n))


def _proj_kernel(*refs, groups, sub, has_rope):
    a_ref, w_ref = refs[0], refs[1]
    pos = 2
    if has_rope:
        cos_ref, sin_ref = refs[2], refs[3]
        pos = 4
    params = []
    for _, epilogue, _ in groups:
        params.append(refs[pos] if epilogue in ("log_forget", "ret_k") else None)
        pos += epilogue in ("log_forget", "ret_k")
    outs = refs[pos:]
    a = a_ref[...]
    c0 = 0
    oi = 0
    for (n_cols, epilogue, _), pr in zip(groups, params):
        for s0 in range(0, n_cols, sub):
            cols = slice(s0, s0 + sub)
            acc = _dot(a, w_ref[:, c0 + s0:c0 + s0 + sub])
            if epilogue is None:
                outs[oi][:, cols] = acc.astype(outs[oi].dtype)
            elif epilogue == "silu":
                outs[oi][:, cols] = _silu(acc).astype(outs[oi].dtype)
            elif epilogue == "log_forget":
                log_sig = jnp.minimum(acc, 0.0) - jnp.log(1.0 + jnp.exp(-jnp.abs(acc)))
                log_lb = pr[0:1, cols]
                bb = pr[1:2, cols] + log_sig
                outs[oi][:, cols] = jnp.maximum(log_lb, bb) + jnp.log(1.0 + jnp.exp(-jnp.abs(log_lb - bb)))
            else:
                reps = sub // cos_ref.shape[1]
                r = _rope(acc, jnp.concatenate([cos_ref[...]] * reps, axis=1), jnp.concatenate([sin_ref[...]] * reps, axis=1))
                if epilogue == "rope":
                    outs[oi][:, cols] = r.astype(outs[oi].dtype)
                else:
                    k = r * (RET_QK_DIM ** -0.5)
                    ipos = (lax.broadcasted_iota(jnp.int32, k.shape, 0) & (RET_CHUNK - 1)).astype(F32)
                    outs[oi][:, cols] = k.astype(BF16)
                    outs[oi + 1][:, cols] = (k * jnp.exp((RET_CHUNK - 1.0 - ipos) * pr[0:1, cols])).astype(BF16)
                    outs[oi + 2][:, cols] = (k * jnp.exp(ipos * pr[1:2, cols])).astype(BF16)
        c0 += n_cols
        oi += 3 if epilogue == "ret_k" else 1


def _project(a, w, groups, *, tm, params=(), rope=None, sub=512):
    n_rows, k = a.shape
    assert n_rows % tm == 0 and sum(g[0] for g in groups) == w.shape[1] and all(g[0] % sub == 0 for g in groups)
    in_specs = [pl.BlockSpec((tm, k), lambda i: (i, 0)), pl.BlockSpec(w.shape, lambda i: (0, 0))]
    args = [a, w]
    if rope is not None:
        cos, sin, block_fn = rope
        in_specs += [pl.BlockSpec((tm, cos.shape[1]), lambda i: (block_fn(i), 0))] * 2
        args += [cos, sin]
    for pr in params:
        in_specs.append(pl.BlockSpec(pr.shape, lambda i: (0, 0)))
        args.append(pr)
    out_specs, out_shape = [], []
    for n_cols, epilogue, dtype in groups:
        for _ in range(3 if epilogue == "ret_k" else 1):
            out_specs.append(pl.BlockSpec((tm, n_cols), lambda i: (i, 0)))
            out_shape.append(jax.ShapeDtypeStruct((n_rows, n_cols), dtype))
    return pl.pallas_call(
        functools.partial(_proj_kernel, groups=tuple(groups), sub=sub, has_rope=rope is not None),
        grid=(n_rows // tm,),
        in_specs=in_specs,
        out_specs=out_specs,
        out_shape=out_shape,
        compiler_params=_cparams("parallel"),
        name="project",
    )(*args)


def _mix_out_kernel(*refs, n_parts, n_x_tiles, has_ctx, tail):
    n_in = n_parts * (2 if has_ctx else 1)
    w_ref = refs[n_in]
    is_ctx = pl.program_id(0) >= n_x_tiles
    acc = None
    k0 = 0
    for p in range(n_parts):
        if has_ctx:
            a = jnp.where(is_ctx, refs[2 * p + 1][...], refs[2 * p][...])
        else:
            a = refs[p][...]
        k1 = k0 + a.shape[1]
        part = _dot(a, w_ref[k0:k1, :])
        acc = part if acc is None else acc + part
        k0 = k1
    tail(acc, *refs[n_in + 1:])


def _mix_out(parts_x, parts_c, w, x, mod, lnp, router, *, alpha, rows_per_batch, tm=ROW_TILE):
    n_x = parts_x[0].shape[0]
    has_ctx = parts_c is not None
    n_c = parts_c[0].shape[0] if has_ctx else 0
    assert n_x % tm == 0 and n_c % tm == 0
    n_x_tiles, n_c_tiles = n_x // tm, n_c // tm
    in_specs, args = [], []
    for p, ax in enumerate(parts_x):
        in_specs.append(pl.BlockSpec((tm, ax.shape[1]), lambda i: (jnp.minimum(i, n_x_tiles - 1), 0)))
        args.append(ax)
        if has_ctx:
            in_specs.append(pl.BlockSpec((tm, ax.shape[1]), lambda i: (jnp.maximum(i - n_x_tiles, 0), 0)))
            args.append(parts_c[p])
    in_specs.append(pl.BlockSpec(w.shape, lambda i: (0, 0)))
    args.append(w)
    tail, t_in, t_args, out_specs, out_shape = _ln_tail_specs(
        x, mod, lnp, router, alpha=alpha, tm=tm, rows_per_batch=rows_per_batch, n_rows=n_x + n_c, row_of=lambda i: i)
    return pl.pallas_call(
        functools.partial(_mix_out_kernel, n_parts=len(parts_x), n_x_tiles=n_x_tiles, has_ctx=has_ctx, tail=tail),
        grid=(n_x_tiles + n_c_tiles,),
        in_specs=in_specs + t_in,
        out_specs=out_specs,
        out_shape=out_shape,
        compiler_params=_cparams("parallel"),
        name="mix_out",
    )(*args, *t_args)


def _modulate_kernel(x_ref, mod_ref, h_ref):
    m = mod_ref[0]
    h_ref[...] = (x_ref[...] * (1.0 + m[1:2]) + m[0:1]).astype(h_ref.dtype)


def _mod_index(tiles_per_batch, n_batch):
    return lambda i: (jnp.minimum(i // tiles_per_batch, n_batch), 0, 0)


def _modulate(x, mod, *, rows_per_batch, tm=ROW_TILE):
    m, d = x.shape
    return pl.pallas_call(
        _modulate_kernel,
        grid=(m // tm,),
        in_specs=[pl.BlockSpec((tm, d), lambda i: (i, 0)),
                  pl.BlockSpec((1, 2, d), _mod_index(rows_per_batch // tm, mod.shape[0] - 1))],
        out_specs=pl.BlockSpec((tm, d), lambda i: (i, 0)),
        out_shape=jax.ShapeDtypeStruct((m, d), BF16),
        compiler_params=_cparams("parallel"),
        name="modulate",
    )(x, mod)


def _ln_tail(y, x_ref, mod_ref, lnp_ref, *rest, alpha, router, n_experts):
    if router:
        rw_ref, rb_ref, xo_ref, h_ref, g_ref, i_ref = rest
    else:
        xo_ref, h_ref = rest
    m = mod_ref[0]
    z = alpha * x_ref[...] + m[0:1] * y
    mu = jnp.mean(z, axis=-1, keepdims=True)
    zc = z - mu
    var = jnp.mean(zc * zc, axis=-1, keepdims=True)
    xn = zc * lax.rsqrt(var + LN_EPS) * lnp_ref[0:1, :] + lnp_ref[1:2, :]
    xo_ref[...] = xn
    hf = xn * (1.0 + m[2:3]) + m[1:2]
    h_ref[...] = hf.astype(h_ref.dtype)
    if router:
        logits = _dot_3pass(hf, rw_ref[...]) + rb_ref[...]
        lane = lax.broadcasted_iota(jnp.int32, logits.shape, 1).astype(F32)
        neg = -jnp.inf
        lg = jnp.where(lane < n_experts, logits, neg)
        m1 = jnp.max(lg, axis=-1, keepdims=True)
        i1 = jnp.min(jnp.where(lg == m1, lane, float(LANES)), axis=-1, keepdims=True)
        lg2 = jnp.where(lane == i1, neg, lg)
        m2 = jnp.max(lg2, axis=-1, keepdims=True)
        i2 = jnp.min(jnp.where(lg2 == m2, lane, float(LANES)), axis=-1, keepdims=True)
        e2 = jnp.exp(m2 - m1)
        den = 1.0 + e2
        g_ref[...] = jnp.where(lane == 0, 1.0 / den, jnp.where(lane == 1, e2 / den, 0.0))
        i_ref[...] = jnp.where(lane == 0, i1, jnp.where(lane == 1, i2, 0.0)).astype(jnp.int32)


def _ln_tail_specs(x, mod, lnp, router, *, alpha, tm, rows_per_batch, n_rows, row_of):
    d = x.shape[1]
    tiles_per_batch, n_batch = rows_per_batch // tm, mod.shape[0] - 1
    row = pl.BlockSpec((tm, d), lambda *g: (row_of(*g), 0))
    lane_row = pl.BlockSpec((tm, LANES), lambda *g: (row_of(*g), 0))
    const = lambda shape: pl.BlockSpec(shape, lambda *g: (0,) * len(shape))
    in_specs = [row, pl.BlockSpec((1, 3, d), lambda *g: (jnp.minimum(row_of(*g) // tiles_per_batch, n_batch), 0, 0)),
                const((2, d))]
    args = [x, mod, lnp]
    out_specs = [row, row]
    out_shape = [jax.ShapeDtypeStruct((n_rows, d), F32), jax.ShapeDtypeStruct((n_rows, d), BF16 if router is None else F32)]
    if router is not None:
        in_specs += [const((d, LANES)), const((1, LANES))]
        args += list(router)
        out_specs += [lane_row, lane_row]
        out_shape += [jax.ShapeDtypeStruct((n_rows, LANES), F32), jax.ShapeDtypeStruct((n_rows, LANES), jnp.int32)]
    tail = functools.partial(_ln_tail, alpha=alpha, router=router is not None, n_experts=N_EXPERTS)
    return tail, in_specs, args, out_specs, out_shape


def _ln_kernel(*refs, two_y, tail):
    if two_y:
        y_ref, y2_ref, yg_ref = refs[:3]
        yg = yg_ref[...]
        y = yg[:, 0:1] * y_ref[...].astype(F32) + yg[:, 1:2] * y2_ref[...].astype(F32)
        refs = refs[3:]
    else:
        y = refs[0][...].astype(F32)
        refs = refs[1:]
    tail(y, *refs)


def _residual_ln(x, y, mod, lnp, *, alpha, rows_per_batch, n_rows=None, y_gates=None, router=None, tm=ROW_TILE):
    d = x.shape[1]
    n_rows = x.shape[0] if n_rows is None else n_rows
    row = pl.BlockSpec((tm, d), lambda i: (i, 0))
    two_y = y_gates is not None
    in_specs, args = [row], [y]
    if two_y:
        second_half = n_rows // tm
        in_specs += [pl.BlockSpec((tm, d), lambda i: (second_half + i, 0)), pl.BlockSpec((tm, LANES), lambda i: (i, 0))]
        args += [y, y_gates]
    tail, t_in, t_args, out_specs, out_shape = _ln_tail_specs(
        x, mod, lnp, router, alpha=alpha, tm=tm, rows_per_batch=rows_per_batch, n_rows=n_rows, row_of=lambda i: i)
    return pl.pallas_call(
        functools.partial(_ln_kernel, two_y=two_y, tail=tail),
        grid=(n_rows // tm,),
        in_specs=in_specs + t_in,
        out_specs=out_specs,
        out_shape=out_shape,
        compiler_params=_cparams("parallel"),
        name="residual_ln",
    )(*args, *t_args)


def _gffn_kernel(te_ref, tv_ref, x_ref, w1_ref, w3_ref, w2_ref, *rest, n_f, tail):
    acc_scr = rest[-1]
    i, f = pl.program_id(0), pl.program_id(1)

    @pl.when(f == 0)
    def _():
        acc_scr[...] = jnp.zeros_like(acc_scr)

    @pl.when(tv_ref[i] != 0)
    def _():
        x = x_ref[...].astype(BF16)
        h1 = _dot(x, w1_ref[0, 0])
        h3 = _dot(x, w3_ref[0, 0])
        a = (_silu(h1) * h3).astype(BF16)
        acc_scr[...] += _dot(a, w2_ref[0, 0].astype(BF16))

    @pl.when(f == n_f - 1)
    def _():
        if tail is None:
            rest[0][...] = acc_scr[...]
        else:
            tail(acc_scr[...], *rest[:-1])


def _grouped_swiglu(xs, w1, w3, w2, layer, tile_expert, tile_valid, *, tm, tf, ln=None):
    p, d = xs.shape
    f = w1.shape[3]
    assert p % tm == 0 and f % tf == 0
    in_specs = [pl.BlockSpec((tm, d), lambda i, j, te, tv: (i, 0)),
                pl.BlockSpec((1, 1, d, tf), lambda i, j, te, tv: (layer, te[i], 0, j * tv[i])),
                pl.BlockSpec((1, 1, d, tf), lambda i, j, te, tv: (layer, te[i], 0, j * tv[i])),
                pl.BlockSpec((1, 1, tf, d), lambda i, j, te, tv: (layer, te[i], j * tv[i], 0))]
    args = [xs, w1, w3, w2]
    if ln is None:
        tail = None
        out_specs = pl.BlockSpec((tm, d), lambda i, j, te, tv: (i, 0))
        out_shape = jax.ShapeDtypeStruct((p, d), F32)
    else:
        tail, t_in, t_args, out_specs, out_shape = _ln_tail_specs(
            ln["x"], ln["mod"], ln["lnp"], None, alpha=ln["alpha"], tm=tm, rows_per_batch=ln["rows_per_batch"],
            n_rows=p, row_of=lambda i, j, te, tv: i)
        in_specs += t_in
        args += t_args
    grid_spec = pltpu.PrefetchScalarGridSpec(
        num_scalar_prefetch=2,
        grid=(p // tm, f // tf),
        in_specs=in_specs,
        out_specs=out_specs,
        scratch_shapes=[pltpu.VMEM((tm, d), F32)],
    )
    return pl.pallas_call(
        functools.partial(_gffn_kernel, n_f=f // tf, tail=tail),
        grid_spec=grid_spec,
        out_shape=out_shape,
        compiler_params=_cparams("parallel", "arbitrary"),
        name="grouped_swiglu",
    )(tile_expert, tile_valid, *args)


def _route(idx, n_experts, tm):
    m = idx.shape[0]
    flat = idx.reshape(-1)
    n = flat.shape[0]
    p = -(-(n + n_experts * (tm - 1)) // tm) * tm
    onehot = (flat[:, None] == jnp.arange(n_experts, dtype=jnp.int32)[None, :]).astype(jnp.int32)
    csum = jnp.cumsum(onehot, axis=0)
    counts = csum[-1]
    padded = -(-counts // tm) * tm
    pad_end = jnp.cumsum(padded)
    pad_off = pad_end - padded
    rank = jnp.sum(csum * onehot, axis=1) - 1
    assign_slot = (pad_off[flat] + rank).reshape(m, 2)
    slot = jnp.arange(p, dtype=jnp.int32)
    slot_expert = jnp.minimum(jnp.sum(pad_end[None, :] <= slot[:, None], axis=1), n_experts - 1).astype(jnp.int32)
    slot_rank = slot - pad_off[slot_expert]
    order = jnp.argsort(flat, stable=True).astype(jnp.int32)
    off = jnp.cumsum(counts) - counts
    src = order[jnp.minimum(off[slot_expert] + slot_rank, n - 1)]
    slot_token = jnp.where(slot_rank < counts[slot_expert], src // 2, slot % m).astype(jnp.int32)
    tile_start = slot[::tm]
    tile_valid = (tile_start < pad_end[-1]).astype(jnp.int32)
    tile_expert = slot_expert[::tm]
    return slot_token, assign_slot, tile_expert, tile_valid


def _softmax_pv(parts):
    mx = functools.reduce(jnp.maximum, [jnp.max(s, axis=-1, keepdims=True) for s, _ in parts])
    ps = [jnp.exp(s - mx) for s, _ in parts]
    den = functools.reduce(jnp.add, [jnp.sum(p, axis=-1, keepdims=True) for p in ps])
    num = functools.reduce(jnp.add, [_dot(p.astype(BF16), v) for p, (_, v) in zip(ps, parts)])
    return num / den


def _na_kernel(*refs, rows, gw, ctx_out):
    if ctx_out:
        q_ref, k_ref, v_ref, kc_ref, vc_ref, qc_ref, bias_ref, o_ref, oc_ref, s_scr, p_scr, l_scr = refs
    else:
        q_ref, k_ref, v_ref, kc_ref, vc_ref, bias_ref, o_ref, s_scr, p_scr, l_scr = refs
    lane = lax.broadcasted_iota(jnp.int32, (1, LANES), 1)
    head_lanes = [lane < NA_HEAD_DIM, lane >= NA_HEAD_DIM]
    scale = NA_HEAD_DIM ** -0.5
    band = NA_KH * gw

    def band_rows(r):
        b0 = jnp.clip(r - NA_KH // 2, 0, rows - NA_KH)
        return b0, pl.ds(pl.multiple_of(b0 * gw, gw), band)

    def scores(r, par):
        r = jnp.minimum(r, rows - 1)
        b0, ks = band_rows(r)
        dr0 = b0 - r + NA_KH - 1
        q = q_ref[pl.ds(pl.multiple_of(r * gw, gw), gw), :]
        kb = k_ref[ks, :]
        for h in range(2):
            qh = jnp.where(head_lanes[h], q, 0) * scale
            s_scr[par, h, :, :band] = _dot_nt(qh, kb) + bias_ref[h, dr0]
            s_scr[par, h, :, band:] = _dot_nt(qh, kc_ref[...])

    def softmax(par):
        for h in range(2):
            s = s_scr[par, h]
            e = jnp.exp(s - jnp.max(s, axis=-1, keepdims=True))
            l_scr[par, h] = jnp.broadcast_to(jnp.sum(e, axis=-1, keepdims=True), (gw, LANES))
            p_scr[par, h] = e.astype(BF16)

    def weighted_values(r, par):
        _, ks = band_rows(r)
        vb = v_ref[ks, :]
        outs = []
        for h in range(2):
            num = _dot(p_scr[par, h, :, :band], vb) + _dot(p_scr[par, h, :, band:], vc_ref[...])
            outs.append(num / l_scr[par, h])
        o_ref[pl.ds(pl.multiple_of(r * gw, gw), gw), :] = jnp.where(head_lanes[0], outs[0], outs[1]).astype(o_ref.dtype)

    scores(0, 0)
    scores(1, 1)
    softmax(0)

    def body(i, carry):
        r = 2 * i
        weighted_values(r, 0)
        softmax(1)
        scores(r + 2, 0)
        weighted_values(r + 1, 1)
        softmax(0)
        scores(r + 3, 1)
        return carry

    lax.fori_loop(0, rows // 2, body, 0)
    if ctx_out:
        qc = qc_ref[...]
        kc = kc_ref[...]
        vc = vc_ref[...]
        outs = []
        for h in range(2):
            qh = jnp.where(head_lanes[h], qc, 0) * scale
            outs.append(_softmax_pv([(_dot_nt(qh, kc), vc)]))
        oc_ref[...] = jnp.where(head_lanes[0], outs[0], outs[1]).astype(oc_ref.dtype)


def _na_bias_table(rpb, gw):
    col = jnp.arange(gw)
    c_start = jnp.clip(col - NA_KW // 2, 0, gw - NA_KW)
    col_in = (col[None, :] >= c_start[:, None]) & (col[None, :] < c_start[:, None] + NA_KW)
    dc = jnp.clip(col[None, :] - col[:, None], 1 - NA_KW, NA_KW - 1) + NA_KW - 1
    t = jnp.where(col_in[None, None], rpb[:, :, dc].astype(F32), MASK_VALUE)
    win = jnp.stack([t[:, d:d + NA_KH] for d in range(NA_KH)], axis=1)
    return win.transpose(0, 1, 3, 2, 4).reshape(rpb.shape[0], NA_KH, gw, NA_KH * gw)


def _neighbourhood_attention(proj, rpb, *, n_batch, seq, ctx_len, ctx_out):
    width = rpb.shape[0] * NA_HEAD_DIM
    pairs = width // LANES
    rows = seq // GRID_W
    bias = _na_bias_table(rpb, GRID_W)
    n_keys = NA_KH * GRID_W + ctx_len
    cblk0 = n_batch * seq // ctx_len
    xspec = lambda g: pl.BlockSpec((seq, LANES), lambda b, p: (b, g * pairs + p))
    cspec = lambda g: pl.BlockSpec((ctx_len, LANES), lambda b, p: (cblk0 + b, g * pairs + p))
    in_specs = [xspec(0), xspec(1), xspec(2), cspec(1), cspec(2)]
    args = [proj, proj, proj, proj, proj]
    if ctx_out:
        in_specs.append(cspec(0))
        args.append(proj)
    in_specs.append(pl.BlockSpec((2, NA_KH, GRID_W, NA_KH * GRID_W), lambda b, p: (p, 0, 0, 0)))
    args.append(bias)
    out_specs = [pl.BlockSpec((seq, LANES), lambda b, p: (b, p))]
    out_shape = [jax.ShapeDtypeStruct((n_batch * seq, width), BF16)]
    if ctx_out:
        out_specs.append(pl.BlockSpec((ctx_len, LANES), lambda b, p: (b, p)))
        out_shape.append(jax.ShapeDtypeStruct((n_batch * ctx_len, width), BF16))
    return pl.pallas_call(
        functools.partial(_na_kernel, rows=rows, gw=GRID_W, ctx_out=ctx_out),
        grid=(n_batch, pairs),
        in_specs=in_specs,
        out_specs=out_specs,
        out_shape=out_shape,
        scratch_shapes=[pltpu.VMEM((2, 2, GRID_W, n_keys), F32),
                        pltpu.VMEM((2, 2, GRID_W, n_keys), BF16),
                        pltpu.VMEM((2, 2, GRID_W, LANES), F32)],
        compiler_params=_cparams("parallel", "parallel"),
        name="neighbourhood_attention",
    )(*args)


def _hg_kernel(*refs, rev, n_chunks, n_blocks, epilogue):
    if epilogue:
        q_ref, f_ref, i_ref, s0_ref, pm_ref, rm_ref, g_ref, prev_ref, ng_ref, o_ref, sT_ref = refs[:11]
    else:
        q_ref, f_ref, i_ref, s0_ref, pm_ref, rm_ref, o_ref, sT_ref = refs[:8]
    st_scr, att_scr, qd_scr, ku_scr, dec_scr = refs[-5:]
    blk = pl.program_id(2)

    @pl.when(blk == 0)
    def _():
        st_scr[...] = s0_ref[0, 0]

    c = HG_CHUNK

    def rows_of(ci):
        ci = jnp.minimum(ci, n_chunks - 1)
        cc = (n_chunks - 1 - ci) if rev else ci
        return pl.ds(pl.multiple_of(cc * c, c), c)

    def decays(ci, slot):
        sl = rows_of(ci)
        q = q_ref[sl, :].astype(F32)
        lf = f_ref[sl, :] * LOG2_E
        k = 1.0 - jnp.exp2(lf)
        p, tot = lf, lf
        att = pm_ref[0] * jnp.sum(q * k, axis=-1, keepdims=True)
        for lvl in range(1, pm_ref.shape[0]):
            m = 1 << (lvl - 1)
            second = rm_ref[lvl - 1] != 0.0
            is_q = jnp.logical_not(second) if rev else second
            z = (jnp.where(is_q, q, k) * jnp.exp2(jnp.where(is_q, p, tot - p))).astype(BF16)
            att = att + pm_ref[lvl] * _dot_nt(z, z)
            t_up = pltpu.roll(tot, m, 0)
            t_dn = pltpu.roll(tot, c - m, 0)
            p = p + (jnp.where(second, 0.0, t_dn) if rev else jnp.where(second, t_up, 0.0))
            tot = tot + jnp.where(second, t_up, t_dn)
        att_scr[slot] = att.astype(BF16)
        qd_scr[slot] = (q * jnp.exp2(p)).astype(BF16)
        ku_scr[slot] = (k * jnp.exp2(tot - p)).astype(BF16)
        dec_scr[slot] = jnp.exp2(tot[0:8, :])

    def outputs(ci, slot):
        sl = rows_of(ci)
        v = i_ref[sl, :]
        st = st_scr[...]
        o = _dot(att_scr[slot], v) + _dot_nt(qd_scr[slot], st.astype(BF16))
        st_scr[...] = st * dec_scr[slot, 0:1, :] + _dot_tn(v, ku_scr[slot])
        if epilogue:
            o = o + prev_ref[sl, :]
            o = o * lax.rsqrt(jnp.mean(o * o, axis=-1, keepdims=True) + NORM_EPS) * ng_ref[...]
            o = o * g_ref[sl, :].astype(F32)
        o_ref[sl, :] = o.astype(o_ref.dtype)

    decays(0, 0)

    def pair(i, carry):
        ci = 2 * i
        outputs(ci, 0)
        decays(ci + 1, 1)
        outputs(ci + 1, 1)
        decays(ci + 2, 0)
        return carry

    lax.fori_loop(0, n_chunks // 2, pair, 0)

    @pl.when(blk == n_blocks - 1)
    def _():
        sT_ref[0, 0] = st_scr[...]


def _hg_level_masks(rev):
    c = HG_CHUNK
    t = jnp.arange(c)[:, None]
    s = jnp.arange(c)[None, :]
    pair, row = [t == s], []
    m = 1
    while m < c:
        t_second = (t & m) != 0
        s_second = (s & m) != 0
        same = (t ^ s) < 2 * m
        pair.append(same & (~t_second & s_second if rev else t_second & ~s_second))
        row.append(jnp.broadcast_to(t_second, (c, HG_KEY_DIM)))
        m *= 2
    return jnp.stack(pair).astype(F32), jnp.stack(row).astype(F32)


def _hg_scan(qg, lf, proj, s0, *, rev, row0, n_batch, seq, block_rows, n_heads, col_f, col_i, prev=None, norm_g=None):
    dk = HG_KEY_DIM
    n_blocks = seq // block_rows
    blk0 = row0 // block_rows
    epilogue = prev is not None

    def local_rows(b, i):
        return b * n_blocks + ((n_blocks - 1 - i) if rev else i)

    pspec = lambda col: pl.BlockSpec((block_rows, dk), lambda b, h, i: (blk0 + local_rows(b, i), col + h))
    state_spec = pl.BlockSpec((1, 1, dk, dk), lambda b, h, i: (b, h, 0, 0))
    local_spec = pl.BlockSpec((block_rows, dk), lambda b, h, i: (local_rows(b, i), h))
    pair_mask, row_mask = _hg_level_masks(rev)
    const_spec = lambda a: pl.BlockSpec(a.shape, lambda b, h, i: (0, 0, 0))
    in_specs = [pspec(0), pspec(col_f), pspec(col_i), state_spec, const_spec(pair_mask), const_spec(row_mask)]
    args = [qg, lf, proj, s0, pair_mask, row_mask]
    if epilogue:
        in_specs += [pspec(n_heads), local_spec, pl.BlockSpec((1, dk), lambda b, h, i: (0, 0))]
        args += [qg, prev, norm_g.reshape(1, dk)]
    return pl.pallas_call(
        functools.partial(_hg_kernel, rev=rev, n_chunks=block_rows // HG_CHUNK, n_blocks=n_blocks, epilogue=epilogue),
        grid=(n_batch, n_heads, n_blocks),
        in_specs=in_specs,
        out_specs=[local_spec, state_spec],
        out_shape=[jax.ShapeDtypeStruct((n_batch * seq, n_heads * dk), BF16 if epilogue else F32),
                   jax.ShapeDtypeStruct((n_batch, n_heads, dk, dk), F32)],
        scratch_shapes=[pltpu.VMEM((dk, dk), F32),
                        pltpu.VMEM((2, HG_CHUNK, HG_CHUNK), BF16),
                        pltpu.VMEM((2, HG_CHUNK, dk), BF16),
                        pltpu.VMEM((2, HG_CHUNK, dk), BF16),
                        pltpu.VMEM((2, 8, dk), F32)],
        compiler_params=_cparams("parallel", "parallel", "arbitrary"),
        name="hgrn2_scan_rev" if rev else "hgrn2_scan_fwd",
    )(*args)


def _ret_kernel(*refs, fwd, n_chunks, n_blocks):
    if fwd:
        (q_ref, k_ref, kd_ref, v_ref, lam_ref, s0_ref, g_ref, prev_ref, o_ref, sT_ref, st_scr, qd_scr, dm_scr,
         att_scr) = refs
    else:
        q_ref, kd_ref, v_ref, lam_ref, s0_ref, o_ref, sT_ref, st_scr, qd_scr = refs
    blk = pl.program_id(2)
    c = RET_CHUNK
    lam_f = lam_ref[0, 0:1, :]
    lam_b = lam_ref[0, 1:2, :]
    lam = lam_f if fwd else lam_b

    @pl.when(blk == 0)
    def _():
        st_scr[...] = s0_ref[0, 0]
        ipos = lax.broadcasted_iota(jnp.int32, (c, LANES), 0).astype(F32)
        steps = (ipos + 1.0) if fwd else (c - ipos)
        qd_scr[...] = jnp.exp(steps * lam[:, :LANES])
        if fwd:
            dist = (lax.broadcasted_iota(jnp.int32, (c, c), 0) - lax.broadcasted_iota(jnp.int32, (c, c), 1)).astype(F32)
            dm_scr[...] = (jnp.where(dist >= 0, jnp.exp(jnp.maximum(dist, 0.0) * lam_f[:, :c]), 0.0)
                           + jnp.where(dist <= 0, jnp.exp(jnp.maximum(-dist, 0.0) * lam_b[:, :c]), 0.0))

    chunk_decay = jnp.exp(float(c) * lam[:, 0:1])

    def rows_of(ci):
        ci = jnp.minimum(ci, n_chunks - 1)
        cc = ci if fwd else (n_chunks - 1 - ci)
        return pl.ds(pl.multiple_of(cc * c, c), c)

    def scores(ci, slot):
        sl = rows_of(ci)
        att_scr[slot] = (_dot_nt(q_ref[sl, :], k_ref[sl, :]) * dm_scr[...]).astype(BF16)

    def outputs(ci, slot):
        sl = rows_of(ci)
        q = q_ref[sl, :]
        v = v_ref[sl, :]
        st = st_scr[...]
        o = _dot(q, st.astype(BF16)) * qd_scr[:, 0:1]
        st_scr[...] = st * chunk_decay + _dot_tn(kd_ref[sl, :], v)
        if fwd:
            o = o + _dot(att_scr[slot], v) + prev_ref[sl, :].astype(F32)
            mu = jnp.mean(o, axis=-1, keepdims=True)
            oc = o - mu
            var = jnp.mean(oc * oc, axis=-1, keepdims=True)
            o = oc * lax.rsqrt(var + LN_EPS) * g_ref[sl, :].astype(F32)
        o_ref[sl, :] = o.astype(o_ref.dtype)

    if not fwd:
        def chunk(ci, carry):
            outputs(ci, 0)
            return carry

        lax.fori_loop(0, n_chunks, chunk, 0, unroll=2)
    elif n_chunks == 1:
        scores(0, 0)
        outputs(0, 0)
    else:
        scores(0, 0)

        def pair(i, carry):
            ci = 2 * i
            outputs(ci, 0)
            scores(ci + 1, 1)
            outputs(ci + 1, 1)
            scores(ci + 2, 0)
            return carry

        lax.fori_loop(0, n_chunks // 2, pair, 0)

    @pl.when(blk == n_blocks - 1)
    def _():
        sT_ref[0, 0] = st_scr[...]


def _ret_scan(q, k, kd, v, lam, s0, *, fwd, row0, n_batch, seq, block_rows, n_heads, gate=None, prev=None):
    dk, dv = RET_QK_DIM, RET_V_DIM
    n_blocks = seq // block_rows
    blk0 = row0 // block_rows
    assert RET_CHUNK <= dk

    def lrow(b, i):
        return b * n_blocks + (i if fwd else (n_blocks - 1 - i))

    kspec = pl.BlockSpec((block_rows, dk), lambda b, h, i: (blk0 + lrow(b, i), h))
    vspec = pl.BlockSpec((block_rows, dv), lambda b, h, i: (blk0 + lrow(b, i), h))
    local_spec = pl.BlockSpec((block_rows, dv), lambda b, h, i: (lrow(b, i), h))
    state_spec = pl.BlockSpec((1, 1, dk, dv), lambda b, h, i: (b, h, 0, 0))
    lam_spec = pl.BlockSpec((1, 2, dk), lambda b, h, i: (h, 0, 0))
    scratch = [pltpu.VMEM((dk, dv), F32), pltpu.VMEM((RET_CHUNK, LANES), F32)]
    if fwd:
        in_specs = [kspec, kspec, kspec, vspec, lam_spec, state_spec, vspec, local_spec]
        args = [q, k, kd, v, lam, s0, gate, prev]
        scratch += [pltpu.VMEM((RET_CHUNK, RET_CHUNK), F32),
                    pltpu.VMEM((2, RET_CHUNK, RET_CHUNK), BF16)]
    else:
        in_specs = [kspec, kspec, vspec, lam_spec, state_spec]
        args = [q, kd, v, lam, s0]
    return pl.pallas_call(
        functools.partial(_ret_kernel, fwd=fwd, n_chunks=block_rows // RET_CHUNK, n_blocks=n_blocks),
        grid=(n_batch, n_heads, n_blocks),
        in_specs=in_specs,
        out_specs=[local_spec, state_spec],
        out_shape=[jax.ShapeDtypeStruct((n_batch * seq, n_heads * dv), BF16),
                   jax.ShapeDtypeStruct((n_batch, n_heads, dk, dv), F32)],
        scratch_shapes=scratch,
        compiler_params=_cparams("parallel", "parallel", "arbitrary"),
        name="retention_fwd" if fwd else "retention_rev",
    )(*args)


def _rope_tables(seq, gw, pad_rows):
    half = LANES // 2
    inv = ROPE_BASE ** (-jnp.arange(half, dtype=F32) / half)
    t = jnp.arange(seq)
    ang_r = (t // gw).astype(F32)[:, None] * inv
    ang_c = (t % gw).astype(F32)[:, None] * inv
    cos = jnp.concatenate([jnp.cos(ang_r)] * 2 + [jnp.cos(ang_c)] * 2, axis=-1)
    sin = jnp.concatenate([-jnp.sin(ang_r), jnp.sin(ang_r), -jnp.sin(ang_c), jnp.sin(ang_c)], axis=-1)
    cos = jnp.concatenate([cos, jnp.ones((pad_rows, cos.shape[1]), F32)], axis=0)
    sin = jnp.concatenate([sin, jnp.zeros((pad_rows, sin.shape[1]), F32)], axis=0)
    return cos, sin


def _even_mixer(h, w_in, rpb, lb, norm_g, *, n_batch, seq, ctx_len, ctx_out, tm):
    n_x = n_batch * seq
    na_w = rpb.shape[0] * NA_HEAD_DIM
    hg_w = lb.shape[1]
    n_heads = hg_w // HG_KEY_DIM
    cuts = [0, 3 * na_w] + [3 * na_w + i * hg_w for i in range(1, 6)]
    seg = lambda i: w_in[:, cuts[i]:cuts[i + 1]]
    w = jnp.concatenate([seg(0), seg(4), seg(1), seg(5), seg(2), seg(3)], axis=1).astype(BF16)
    lb_rows = jnp.stack([jnp.log(lb).reshape(-1), jnp.log1p(-lb).reshape(-1)])
    groups = [(3 * na_w + hg_w, None, BF16),
              (2 * hg_w, "silu", BF16),
              (2 * hg_w, "log_forget", F32)]
    proj, qg, lf = _project(h, w, groups, tm=tm, params=[lb_rows])
    na = _neighbourhood_attention(proj, rpb, n_batch=n_batch, seq=seq, ctx_len=ctx_len, ctx_out=ctx_out)
    zeros = jnp.zeros((n_batch, n_heads, HG_KEY_DIM, HG_KEY_DIM), F32)
    col_i = 3 * na_w // HG_KEY_DIM
    ctx_kw = dict(row0=n_x, n_batch=n_batch, seq=ctx_len, block_rows=ctx_len, n_heads=n_heads, col_i=col_i)
    x_kw = dict(row0=0, n_batch=n_batch, seq=seq, block_rows=min(SCAN_BLOCK, seq), n_heads=n_heads, col_i=col_i)
    oc_f, sc_f = _hg_scan(qg, lf, proj, zeros, rev=False, col_f=0, **ctx_kw)
    gc, sc_b = _hg_scan(qg, lf, proj, zeros, rev=True, col_f=n_heads, prev=oc_f, norm_g=norm_g, **ctx_kw)
    ox_f, _ = _hg_scan(qg, lf, proj, sc_f, rev=False, col_f=0, **x_kw)
    gx, _ = _hg_scan(qg, lf, proj, sc_b, rev=True, col_f=n_heads, prev=ox_f, norm_g=norm_g, **x_kw)
    return [na[0], gx], ([na[1], gc] if ctx_out else None)


def _odd_mixer(h, w_in, log_decay, rope, *, n_batch, seq, ctx_len, ctx_out, tm):
    n_x = n_batch * seq
    n_heads = log_decay.shape[1]
    qk_w = n_heads * RET_QK_DIM
    v_w = n_heads * RET_V_DIM
    lam = jnp.broadcast_to(log_decay.astype(F32).T[:, :, None], (n_heads, 2, RET_QK_DIM))
    lam_rows = lam.transpose(1, 0, 2).reshape(2, qk_w)
    groups = [(qk_w, "rope", BF16), (qk_w, "ret_k", BF16), (v_w, None, BF16), (v_w, "silu", BF16)]
    q, k, kf, kb, v, g = _project(h, w_in.astype(BF16), groups, tm=tm, params=[lam_rows], rope=rope)
    zeros = jnp.zeros((n_batch, n_heads, RET_QK_DIM, RET_V_DIM), F32)
    ctx_kw = dict(row0=n_x, n_batch=n_batch, seq=ctx_len, block_rows=ctx_len, n_heads=n_heads)
    x_kw = dict(row0=0, n_batch=n_batch, seq=seq, block_rows=min(SCAN_BLOCK, seq), n_heads=n_heads)
    oc_b, sc_b = _ret_scan(q, k, kb, v, lam, zeros, fwd=False, **ctx_kw)
    yc, sc_f = _ret_scan(q, k, kf, v, lam, zeros, fwd=True, gate=g, prev=oc_b, **ctx_kw)
    ox_b, _ = _ret_scan(q, k, kb, v, lam, sc_b, fwd=False, **x_kw)
    yx, _ = _ret_scan(q, k, kf, v, lam, sc_f, fwd=True, gate=g, prev=ox_b, **x_kw)
    return [yx], ([yc] if ctx_out else None)


def kernel(x, c, ctx, c_ctx, ada_w, ada_b, ln_g, ln_b, e_w_in, e_w_out, na_rpb, hg_lb_logits, hg_norm_g, ffn_w1, ffn_w3,
           ffn_w2, o_w_in, o_w_out, ret_log_decay, router_w, router_b, moe_w1, moe_w3, moe_w2):
    n_batch, seq, d = x.shape
    ctx_len = ctx.shape[1]
    depth = ada_w.shape[0]
    n_x = n_batch * seq
    n_all = n_x + n_batch * ctx_len
    alpha = (2 * depth) ** 0.25
    row_gcd = math.gcd(math.gcd(n_x, n_all), seq)
    tm_even = _pick_tile(row_gcd, EVEN_PROJ_TILE, unit=RET_CHUNK)
    tm_odd = _pick_tile(row_gcd, ODD_PROJ_TILE, unit=RET_CHUNK)
    dims = dict(n_batch=n_batch, seq=seq, ctx_len=ctx_len)

    lb_cum = jnp.cumsum(jax.nn.softmax(hg_lb_logits.astype(F32), axis=1), axis=1)
    lower_bounds = lb_cum - lb_cum[:, :1]
    cos, sin = _rope_tables(seq, GRID_W, tm_odd)
    tiles_per_seq = seq // tm_odd
    rope = (cos, sin, lambda i: jnp.where(i < n_x // tm_odd, i % tiles_per_seq, tiles_per_seq))

    cond = jnp.concatenate([c, c_ctx[None, :], jnp.zeros((8 - n_batch - 1, d), F32)], axis=0)
    mods = [_silu_matmul(cond, ada_w[l], ada_b[l], tn=d).reshape(8, 6, d)[:n_batch + 1] for l in range(depth)]

    ffn_w = [w.astype(BF16)[:, None] for w in (ffn_w1, ffn_w3, ffn_w2)]
    moe_w = [moe_w1.astype(BF16), moe_w3.astype(BF16), moe_w2]

    tok = jnp.concatenate([x.reshape(n_x, d), ctx.reshape(n_batch * ctx_len, d)], axis=0)
    h = _modulate(tok, mods[0][:, 0:2], rows_per_batch=seq)
    for layer in range(depth):
        j = layer // 2
        last = layer == depth - 1
        mod = mods[layer]
        n_rows = n_x if last else n_all
        if layer % 2 == 0:
            parts_x, parts_c = _even_mixer(h, e_w_in[j], na_rpb[j], lower_bounds[:, j], hg_norm_g[j], ctx_out=not last,
                                           tm=tm_even, **dims)
            w_out, router = e_w_out[j], None
        else:
            parts_x, parts_c = _odd_mixer(h, o_w_in[j], ret_log_decay[j], rope, ctx_out=not last, tm=tm_odd, **dims)
            rw = jnp.zeros((d, LANES), F32).at[:, :N_EXPERTS].set(router_w[j])
            rb = jnp.zeros((1, LANES), F32).at[0, :N_EXPERTS].set(router_b[j])
            w_out, router = o_w_out[j], (rw, rb)
        lnp = lambda i: jnp.stack([ln_g[layer, i], ln_b[layer, i]])
        res = _mix_out(parts_x, parts_c, w_out.astype(BF16), tok, mod[:, 2:5], lnp(0), router, alpha=alpha,
                       rows_per_batch=seq)
        tok, h = res[0], res[1]
        nxt = mods[layer + 1][:, 0:2] if not last else jnp.stack([jnp.zeros_like(mod[:, 0])] * 2, axis=1)
        mod2 = jnp.concatenate([mod[:, 5:6], nxt], axis=1)
        if layer % 2 == 0:
            ones = jnp.ones((n_rows // ROW_TILE,), jnp.int32)
            res = _grouped_swiglu(h, *ffn_w, j, 0 * ones, ones, tm=ROW_TILE, tf=_pick_tile(ffn_w1.shape[2], 1408),
                                  ln=dict(x=tok, mod=mod2, lnp=lnp(1), alpha=alpha, rows_per_batch=seq))
        else:
            gates, idx = res[2], res[3]
            slot_token, assign_slot, tile_expert, tile_valid = _route(idx[:, :2], N_EXPERTS, ROW_TILE)
            xs = jnp.take(h, slot_token, axis=0, mode="clip")
            ys = _grouped_swiglu(xs, *moe_w, j, tile_expert, tile_valid, tm=ROW_TILE,
                                 tf=_pick_tile(moe_w1.shape[3], 1792))
            y12 = jnp.take(ys, assign_slot.T.reshape(-1), axis=0, mode="clip")
            res = _residual_ln(tok, y12, mod2, lnp(1), alpha=alpha, rows_per_batch=seq, y_gates=gates)
        tok, h = res[0], res[1]
    return tok[:n_x].reshape(n_batch, seq, d)
```

```python
import functools
import math

import jax
import jax.numpy as jnp
import numpy as np
from jax import lax
from jax.experimental import pallas as pl
from jax.experimental.pallas import tpu as pltpu

F32 = jnp.float32
BF16 = jnp.bfloat16

GRID_W = 64
NA_HEAD_DIM = 64
NA_KH = 8
NA_KW = 16
HG_KEY_DIM = 128
RET_QK_DIM = 256
RET_V_DIM = 512
N_EXPERTS = 8
ROPE_BASE = 10000.0
LN_EPS = 1e-5
NORM_EPS = 1e-6
LOG2_E = 1.4426950408889634
LANES = 128
HG_CHUNK = 128
RET_CHUNK = 256
ROW_TILE = 512
EVEN_PROJ_TILE = 512
ODD_PROJ_TILE = 512
SCAN_BLOCK = 2048
FFN_SUB = 256
VMEM_LIMIT = 48 * 1024 * 1024
MASK_VALUE = -1e30


def _cparams(*sem):
    return pltpu.CompilerParams(dimension_semantics=sem, vmem_limit_bytes=VMEM_LIMIT)


def _dot(a, b):
    return jnp.dot(a, b, preferred_element_type=F32)


def _dot_nt(a, b):
    return lax.dot_general(a, b, (((1,), (1,)), ((), ())), preferred_element_type=F32)


def _dot_tn(a, b):
    return lax.dot_general(a, b, (((0,), (0,)), ((), ())), preferred_element_type=F32)


def _dot_3pass(a, b):
    a_hi = a.astype(BF16)
    b_hi = b.astype(BF16)
    a_lo = (a - a_hi.astype(F32)).astype(BF16)
    b_lo = (b - b_hi.astype(F32)).astype(BF16)
    return _dot(a_hi, b_hi) + _dot(a_lo, b_hi) + _dot(a_hi, b_lo)


def _silu(x):
    return x * jax.nn.sigmoid(x)


def _pick_tile(n, target, unit=LANES):
    best = None
    for t in range(unit, min(n, target) + 1, unit):
        if n % t == 0:
            best = t
    assert best is not None, (n, target, unit)
    return best


def _rope(x, cos, sin):
    half = LANES // 2
    swapped = jnp.concatenate([pltpu.roll(x[:, g * LANES:(g + 1) * LANES], half, 1) for g in range(x.shape[1] // LANES)],
                              axis=1)
    return x * cos + swapped * sin


def _mm_kernel(a_ref, w_ref, b_ref, o_ref):
    a = _silu(a_ref[...])
    o_ref[...] = _dot(a.astype(BF16), w_ref[...].astype(BF16)) + b_ref[...]


def _silu_matmul(a, w, bias, *, tn):
    m, k = a.shape
    n = w.shape[1]
    assert n % tn == 0
    return pl.pallas_call(
        _mm_kernel,
        grid=(n // tn,),
        in_specs=[pl.BlockSpec((m, k), lambda j: (0, 0)),
                  pl.BlockSpec((k, tn), lambda j: (0, j)),
                  pl.BlockSpec((1, tn), lambda j: (0, j))],
        out_specs=pl.BlockSpec((m, tn), lambda j: (0, j)),
        out_shape=jax.ShapeDtypeStruct((m, n), F32),
        compiler_params=_cparams("parallel"),
        name="silu_matmul",
    )(a, w, bias.reshape(1, n))


def _proj_kernel(*refs, groups, sub, has_rope):
    a_ref, w_ref = refs[0], refs[1]
    pos = 2
    if has_rope:
        cos_ref, sin_ref = refs[2], refs[3]
        pos = 4
    params = []
    for _, epilogue, _ in groups:
        params.append(refs[pos] if epilogue in ("log_forget", "ret_k") else None)
        pos += epilogue in ("log_forget", "ret_k")
    outs = refs[pos:]
    a = a_ref[...]
    c0 = 0
    oi = 0
    for (n_cols, epilogue, _), pr in zip(groups, params):
        for s0 in range(0, n_cols, sub):
            cols = slice(s0, s0 + sub)
            acc = _dot(a, w_ref[:, c0 + s0:c0 + s0 + sub])
            if epilogue is None:
                outs[oi][:, cols] = acc.astype(outs[oi].dtype)
            elif epilogue == "silu":
                outs[oi][:, cols] = _silu(acc).astype(outs[oi].dtype)
            elif epilogue == "log_forget":
                log_sig = jnp.minimum(acc, 0.0) - jnp.log(1.0 + jnp.exp(-jnp.abs(acc)))
                log_lb = pr[0:1, cols]
                bb = pr[1:2, cols] + log_sig
                outs[oi][:, cols] = jnp.maximum(log_lb, bb) + jnp.log(1.0 + jnp.exp(-jnp.abs(log_lb - bb)))
            else:
                reps = sub // cos_ref.shape[1]
                r = _rope(acc, jnp.concatenate([cos_ref[...]] * reps, axis=1), jnp.concatenate([sin_ref[...]] * reps, axis=1))
                if epilogue == "rope":
                    outs[oi][:, cols] = r.astype(outs[oi].dtype)
                else:
                    k = r * (RET_QK_DIM ** -0.5)
                    ipos = (lax.broadcasted_iota(jnp.int32, k.shape, 0) & (RET_CHUNK - 1)).astype(F32)
                    outs[oi][:, cols] = k.astype(BF16)
                    outs[oi + 1][:, cols] = (k * jnp.exp((RET_CHUNK - 1.0 - ipos) * pr[0:1, cols])).astype(BF16)
                    outs[oi + 2][:, cols] = (k * jnp.exp(ipos * pr[1:2, cols])).astype(BF16)
        c0 += n_cols
        oi += 3 if epilogue == "ret_k" else 1


def _project(a, w, groups, *, tm, params=(), rope=None, sub=512):
    n_rows, k = a.shape
    assert n_rows % tm == 0 and sum(g[0] for g in groups) == w.shape[1] and all(g[0] % sub == 0 for g in groups)
    in_specs = [pl.BlockSpec((tm, k), lambda i: (i, 0)), pl.BlockSpec(w.shape, lambda i: (0, 0))]
    args = [a, w]
    if rope is not None:
        cos, sin, block_fn = rope
        in_specs += [pl.BlockSpec((tm, cos.shape[1]), lambda i: (block_fn(i), 0))] * 2
        args += [cos, sin]
    for pr in params:
        in_specs.append(pl.BlockSpec(pr.shape, lambda i: (0, 0)))
        args.append(pr)
    out_specs, out_shape = [], []
    for n_cols, epilogue, dtype in groups:
        for _ in range(3 if epilogue == "ret_k" else 1):
            out_specs.append(pl.BlockSpec((tm, n_cols), lambda i: (i, 0)))
            out_shape.append(jax.ShapeDtypeStruct((n_rows, n_cols), dtype))
    return pl.pallas_call(
        functools.partial(_proj_kernel, groups=tuple(groups), sub=sub, has_rope=rope is not None),
        grid=(n_rows // tm,),
        in_specs=in_specs,
        out_specs=out_specs,
        out_shape=out_shape,
        compiler_params=_cparams("parallel"),
        name="project",
    )(*args)


def _mix_out_kernel(*refs, n_parts, n_x_tiles, has_ctx, tail):
    n_in = n_parts * (2 if has_ctx else 1)
    w_ref = refs[n_in]
    is_ctx = pl.program_id(0) >= n_x_tiles
    acc = None
    k0 = 0
    for p in range(n_parts):
        if has_ctx:
            a = jnp.where(is_ctx, refs[2 * p + 1][...], refs[2 * p][...])
        else:
            a = refs[p][...]
        k1 = k0 + a.shape[1]
        part = _dot(a, w_ref[k0:k1, :])
        acc = part if acc is None else acc + part
        k0 = k1
    tail(acc, *refs[n_in + 1:])


def _mix_out(parts_x, parts_c, w, x, mod, lnp, router, *, alpha, rows_per_batch, tm=ROW_TILE):
    n_x = parts_x[0].shape[0]
    has_ctx = parts_c is not None
    n_c = parts_c[0].shape[0] if has_ctx else 0
    assert n_x % tm == 0 and n_c % tm == 0
    n_x_tiles, n_c_tiles = n_x // tm, n_c // tm
    in_specs, args = [], []
    for p, ax in enumerate(parts_x):
        in_specs.append(pl.BlockSpec((tm, ax.shape[1]), lambda i: (jnp.minimum(i, n_x_tiles - 1), 0)))
        args.append(ax)
        if has_ctx:
            in_specs.append(pl.BlockSpec((tm, ax.shape[1]), lambda i: (jnp.maximum(i - n_x_tiles, 0), 0)))
            args.append(parts_c[p])
    in_specs.append(pl.BlockSpec(w.shape, lambda i: (0, 0)))
    args.append(w)
    tail, t_in, t_args, out_specs, out_shape = _ln_tail_specs(
        x, mod, lnp, router, alpha=alpha, tm=tm, rows_per_batch=rows_per_batch, n_rows=n_x + n_c, row_of=lambda i: i)
    return pl.pallas_call(
        functools.partial(_mix_out_kernel, n_parts=len(parts_x), n_x_tiles=n_x_tiles, has_ctx=has_ctx, tail=tail),
        grid=(n_x_tiles + n_c_tiles,),
        in_specs=in_specs + t_in,
        out_specs=out_specs,
        out_shape=out_shape,
        compiler_params=_cparams("parallel"),
        name="mix_out",
    )(*args, *t_args)


def _modulate_kernel(x_ref, mod_ref, h_ref):
    m = mod_ref[0]
    h_ref[...] = (x_ref[...] * (1.0 + m[1:2]) + m[0:1]).astype(h_ref.dtype)


def _mod_index(tiles_per_batch, n_batch):
    return lambda i: (jnp.minimum(i // tiles_per_batch, n_batch), 0, 0)


def _modulate(x, mod, *, rows_per_batch, tm=ROW_TILE):
    m, d = x.shape
    return pl.pallas_call(
        _modulate_kernel,
        grid=(m // tm,),
        in_specs=[pl.BlockSpec((tm, d), lambda i: (i, 0)),
                  pl.BlockSpec((1, 2, d), _mod_index(rows_per_batch // tm, mod.shape[0] - 1))],
        out_specs=pl.BlockSpec((tm, d), lambda i: (i, 0)),
        out_shape=jax.ShapeDtypeStruct((m, d), BF16),
        compiler_params=_cparams("parallel"),
        name="modulate",
    )(x, mod)


def _ln_tail(y, x_ref, mod_ref, lnp_ref, *rest, alpha, router, n_experts):
    if router:
        rw_ref, rb_ref, xo_ref, h_ref, g_ref, i_ref = rest
    else:
        xo_ref, h_ref = rest
    m = mod_ref[0]
    z = alpha * x_ref[...] + m[0:1] * y
    mu = jnp.mean(z, axis=-1, keepdims=True)
    zc = z - mu
    var = jnp.mean(zc * zc, axis=-1, keepdims=True)
    xn = zc * lax.rsqrt(var + LN_EPS) * lnp_ref[0:1, :] + lnp_ref[1:2, :]
    xo_ref[...] = xn
    hf = xn * (1.0 + m[2:3]) + m[1:2]
    h_ref[...] = hf.astype(h_ref.dtype)
    if router:
        logits = _dot_3pass(hf, rw_ref[...]) + rb_ref[...]
        lane = lax.broadcasted_iota(jnp.int32, logits.shape, 1).astype(F32)
        neg = -jnp.inf
        lg = jnp.where(lane < n_experts, logits, neg)
        m1 = jnp.max(lg, axis=-1, keepdims=True)
        i1 = jnp.min(jnp.where(lg == m1, lane, float(LANES)), axis=-1, keepdims=True)
        lg2 = jnp.where(lane == i1, neg, lg)
        m2 = jnp.max(lg2, axis=-1, keepdims=True)
        i2 = jnp.min(jnp.where(lg2 == m2, lane, float(LANES)), axis=-1, keepdims=True)
        e2 = jnp.exp(m2 - m1)
        den = 1.0 + e2
        g_ref[...] = jnp.where(lane == 0, 1.0 / den, jnp.where(lane == 1, e2 / den, 0.0))
        i_ref[...] = jnp.where(lane == 0, i1, jnp.where(lane == 1, i2, 0.0)).astype(jnp.int32)


def _ln_tail_specs(x, mod, lnp, router, *, alpha, tm, rows_per_batch, n_rows, row_of):
    d = x.shape[1]
    tiles_per_batch, n_batch = rows_per_batch // tm, mod.shape[0] - 1
    row = pl.BlockSpec((tm, d), lambda *g: (row_of(*g), 0))
    lane_row = pl.BlockSpec((tm, LANES), lambda *g: (row_of(*g), 0))
    const = lambda shape: pl.BlockSpec(shape, lambda *g: (0,) * len(shape))
    in_specs = [row, pl.BlockSpec((1, 3, d), lambda *g: (jnp.minimum(row_of(*g) // tiles_per_batch, n_batch), 0, 0)),
                const((2, d))]
    args = [x, mod, lnp]
    out_specs = [row, row]
    out_shape = [jax.ShapeDtypeStruct((n_rows, d), F32), jax.ShapeDtypeStruct((n_rows, d), BF16 if router is None else F32)]
    if router is not None:
        in_specs += [const((d, LANES)), const((1, LANES))]
        args += list(router)
        out_specs += [lane_row, lane_row]
        out_shape += [jax.ShapeDtypeStruct((n_rows, LANES), F32), jax.ShapeDtypeStruct((n_rows, LANES), jnp.int32)]
    tail = functools.partial(_ln_tail, alpha=alpha, router=router is not None, n_experts=N_EXPERTS)
    return tail, in_specs, args, out_specs, out_shape


def _ln_kernel(*refs, two_y, tail):
    if two_y:
        y_ref, y2_ref, yg_ref = refs[:3]
        yg = yg_ref[...]
        y = yg[:, 0:1] * y_ref[...].astype(F32) + yg[:, 1:2] * y2_ref[...].astype(F32)
        refs = refs[3:]
    else:
        y = refs[0][...].astype(F32)
        refs = refs[1:]
    tail(y, *refs)


def _residual_ln(x, y, mod, lnp, *, alpha, rows_per_batch, n_rows=None, y_gates=None, router=None, tm=ROW_TILE):
    d = x.shape[1]
    n_rows = x.shape[0] if n_rows is None else n_rows
    row = pl.BlockSpec((tm, d), lambda i: (i, 0))
    two_y = y_gates is not None
    in_specs, args = [row], [y]
    if two_y:
        second_half = n_rows // tm
        in_specs += [pl.BlockSpec((tm, d), lambda i: (second_half + i, 0)), pl.BlockSpec((tm, LANES), lambda i: (i, 0))]
        args += [y, y_gates]
    tail, t_in, t_args, out_specs, out_shape = _ln_tail_specs(
        x, mod, lnp, router, alpha=alpha, tm=tm, rows_per_batch=rows_per_batch, n_rows=n_rows, row_of=lambda i: i)
    return pl.pallas_call(
        functools.partial(_ln_kernel, two_y=two_y, tail=tail),
        grid=(n_rows // tm,),
        in_specs=in_specs + t_in,
        out_specs=out_specs,
        out_shape=out_shape,
        compiler_params=_cparams("parallel"),
        name="residual_ln",
    )(*args, *t_args)


def _gffn_kernel(te_ref, tv_ref, x_ref, w1_ref, w3_ref, w2_ref, *rest, n_f, tail):
    acc_scr = rest[-1]
    i, f = pl.program_id(0), pl.program_id(1)

    @pl.when(f == 0)
    def _():
        acc_scr[...] = jnp.zeros_like(acc_scr)

    @pl.when(tv_ref[i] != 0)
    def _():
        x = x_ref[...].astype(BF16)
        tf = w1_ref.shape[3]
        sub = _pick_tile(tf, FFN_SUB)
        part = None
        for c0 in range(0, tf, sub):
            h1 = _dot(x, w1_ref[0, 0, :, c0:c0 + sub])
            h3 = _dot(x, w3_ref[0, 0, :, c0:c0 + sub])
            a = (_silu(h1) * h3).astype(BF16)
            down = _dot(a, w2_ref[0, 0, c0:c0 + sub, :].astype(BF16))
            part = down if part is None else part + down
        acc_scr[...] += part

    @pl.when(f == n_f - 1)
    def _():
        if tail is None:
            rest[0][...] = acc_scr[...]
        else:
            tail(acc_scr[...], *rest[:-1])


def _grouped_swiglu(xs, w1, w3, w2, layer, tile_expert, tile_valid, *, tm, tf, ln=None):
    p, d = xs.shape
    f = w1.shape[3]
    assert p % tm == 0 and f % tf == 0
    if w1.shape[1] == 1:
        expert_of = lambda i, te: 0
        f_tile_of = lambda i, j, tv: j
    else:
        expert_of = lambda i, te: te[i]
        f_tile_of = lambda i, j, tv: j * tv[i]
    in_specs = [pl.BlockSpec((tm, d), lambda i, j, te, tv: (i, 0)),
                pl.BlockSpec((1, 1, d, tf), lambda i, j, te, tv: (layer, expert_of(i, te), 0, f_tile_of(i, j, tv))),
                pl.BlockSpec((1, 1, d, tf), lambda i, j, te, tv: (layer, expert_of(i, te), 0, f_tile_of(i, j, tv))),
                pl.BlockSpec((1, 1, tf, d), lambda i, j, te, tv: (layer, expert_of(i, te), f_tile_of(i, j, tv), 0))]
    args = [xs, w1, w3, w2]
    if ln is None:
        tail = None
        out_specs = pl.BlockSpec((tm, d), lambda i, j, te, tv: (i, 0))
        out_shape = jax.ShapeDtypeStruct((p, d), F32)
    else:
        tail, t_in, t_args, out_specs, out_shape = _ln_tail_specs(
            ln["x"], ln["mod"], ln["lnp"], None, alpha=ln["alpha"], tm=tm, rows_per_batch=ln["rows_per_batch"],
            n_rows=p, row_of=lambda i, j, te, tv: i)
        in_specs += t_in
        args += t_args
    grid_spec = pltpu.PrefetchScalarGridSpec(
        num_scalar_prefetch=2,
        grid=(p // tm, f // tf),
        in_specs=in_specs,
        out_specs=out_specs,
        scratch_shapes=[pltpu.VMEM((tm, d), F32)],
    )
    return pl.pallas_call(
        functools.partial(_gffn_kernel, n_f=f // tf, tail=tail),
        grid_spec=grid_spec,
        out_shape=out_shape,
        compiler_params=_cparams("parallel", "arbitrary"),
        name="grouped_swiglu",
    )(tile_expert, tile_valid, *args)


def _route(idx, n_experts, tm):
    m = idx.shape[0]
    flat = idx.reshape(-1)
    n = flat.shape[0]
    p = -(-(n + n_experts * (tm - 1)) // tm) * tm
    onehot = (flat[:, None] == jnp.arange(n_experts, dtype=jnp.int32)[None, :]).astype(jnp.int32)
    csum = jnp.cumsum(onehot, axis=0)
    counts = csum[-1]
    padded = -(-counts // tm) * tm
    pad_end = jnp.cumsum(padded)
    pad_off = pad_end - padded
    rank = jnp.sum(csum * onehot, axis=1) - 1
    assign_slot = (pad_off[flat] + rank).reshape(m, 2)
    slot = jnp.arange(p, dtype=jnp.int32)
    slot_expert = jnp.minimum(jnp.sum(pad_end[None, :] <= slot[:, None], axis=1), n_experts - 1).astype(jnp.int32)
    slot_rank = slot - pad_off[slot_expert]
    order = jnp.argsort(flat, stable=True).astype(jnp.int32)
    off = jnp.cumsum(counts) - counts
    src = order[jnp.minimum(off[slot_expert] + slot_rank, n - 1)]
    slot_token = jnp.where(slot_rank < counts[slot_expert], src // 2, slot % m).astype(jnp.int32)
    tile_start = slot[::tm]
    tile_valid = (tile_start < pad_end[-1]).astype(jnp.int32)
    tile_expert = slot_expert[::tm]
    return slot_token, assign_slot, tile_expert, tile_valid


def _softmax_pv(parts):
    mx = functools.reduce(jnp.maximum, [jnp.max(s, axis=-1, keepdims=True) for s, _ in parts])
    ps = [jnp.exp(s - mx) for s, _ in parts]
    den = functools.reduce(jnp.add, [jnp.sum(p, axis=-1, keepdims=True) for p in ps])
    num = functools.reduce(jnp.add, [_dot(p.astype(BF16), v) for p, (_, v) in zip(ps, parts)])
    return num / den


def _na_kernel(*refs, rows, gw, ctx_out):
    if ctx_out:
        q_ref, k_ref, v_ref, kc_ref, vc_ref, qc_ref, bias_ref, o_ref, oc_ref, s_scr, p_scr, l_scr = refs
    else:
        q_ref, k_ref, v_ref, kc_ref, vc_ref, bias_ref, o_ref, s_scr, p_scr, l_scr = refs
    lane = lax.broadcasted_iota(jnp.int32, (1, LANES), 1)
    head_lanes = [lane < NA_HEAD_DIM, lane >= NA_HEAD_DIM]
    scale = NA_HEAD_DIM ** -0.5
    band = NA_KH * gw

    def band_rows(r):
        b0 = jnp.clip(r - NA_KH // 2, 0, rows - NA_KH)
        return b0, pl.ds(pl.multiple_of(b0 * gw, gw), band)

    def scores(r, par):
        r = jnp.minimum(r, rows - 1)
        b0, ks = band_rows(r)
        dr0 = b0 - r + NA_KH - 1
        q = q_ref[pl.ds(pl.multiple_of(r * gw, gw), gw), :]
        q2 = jnp.concatenate([jnp.where(head_lanes[0], q, 0), jnp.where(head_lanes[1], q, 0)], axis=0) * scale
        s_scr[par, :, :band] = _dot_nt(q2, k_ref[ks, :]) + bias_ref[0, dr0]
        s_scr[par, :, band:] = _dot_nt(q2, kc_ref[...])

    def softmax(par):
        s = s_scr[par]
        e = jnp.exp(s - jnp.max(s, axis=-1, keepdims=True))
        l_scr[par] = jnp.broadcast_to(jnp.sum(e, axis=-1, keepdims=True), (2 * gw, LANES))
        p_scr[par] = e.astype(BF16)

    def weighted_values(r, par):
        _, ks = band_rows(r)
        num = _dot(p_scr[par, :, :band], v_ref[ks, :]) + _dot(p_scr[par, :, band:], vc_ref[...])
        out = num / l_scr[par]
        o_ref[pl.ds(pl.multiple_of(r * gw, gw), gw), :] = jnp.where(head_lanes[0], out[:gw], out[gw:]).astype(o_ref.dtype)

    scores(0, 0)
    scores(1, 1)
    softmax(0)

    def body(i, carry):
        r = 2 * i
        weighted_values(r, 0)
        softmax(1)
        scores(r + 2, 0)
        weighted_values(r + 1, 1)
        softmax(0)
        scores(r + 3, 1)
        return carry

    lax.fori_loop(0, rows // 2, body, 0)
    if ctx_out:
        qc = qc_ref[...]
        kc = kc_ref[...]
        vc = vc_ref[...]
        outs = []
        for h in range(2):
            qh = jnp.where(head_lanes[h], qc, 0) * scale
            outs.append(_softmax_pv([(_dot_nt(qh, kc), vc)]))
        oc_ref[...] = jnp.where(head_lanes[0], outs[0], outs[1]).astype(oc_ref.dtype)


def _na_bias_table(rpb, gw):
    col = jnp.arange(gw)
    c_start = jnp.clip(col - NA_KW // 2, 0, gw - NA_KW)
    col_in = (col[None, :] >= c_start[:, None]) & (col[None, :] < c_start[:, None] + NA_KW)
    dc = jnp.clip(col[None, :] - col[:, None], 1 - NA_KW, NA_KW - 1) + NA_KW - 1
    t = jnp.where(col_in[None, None], rpb[:, :, dc].astype(F32), MASK_VALUE)
    win = jnp.stack([t[:, d:d + NA_KH] for d in range(NA_KH)], axis=1)
    per_head = win.transpose(0, 1, 3, 2, 4).reshape(rpb.shape[0] // 2, 2, NA_KH, gw, NA_KH * gw)
    return per_head.transpose(0, 2, 1, 3, 4).reshape(rpb.shape[0] // 2, NA_KH, 2 * gw, NA_KH * gw)


def _neighbourhood_attention(proj, rpb, *, n_batch, seq, ctx_len, ctx_out):
    width = rpb.shape[0] * NA_HEAD_DIM
    pairs = width // LANES
    rows = seq // GRID_W
    bias = _na_bias_table(rpb, GRID_W)
    n_keys = NA_KH * GRID_W + ctx_len
    cblk0 = n_batch * seq // ctx_len
    xspec = lambda g: pl.BlockSpec((seq, LANES), lambda b, p: (b, g * pairs + p))
    cspec = lambda g: pl.BlockSpec((ctx_len, LANES), lambda b, p: (cblk0 + b, g * pairs + p))
    in_specs = [xspec(0), xspec(1), xspec(2), cspec(1), cspec(2)]
    args = [proj, proj, proj, proj, proj]
    if ctx_out:
        in_specs.append(cspec(0))
        args.append(proj)
    in_specs.append(pl.BlockSpec((1, NA_KH, 2 * GRID_W, NA_KH * GRID_W), lambda b, p: (p, 0, 0, 0)))
    args.append(bias)
    out_specs = [pl.BlockSpec((seq, LANES), lambda b, p: (b, p))]
    out_shape = [jax.ShapeDtypeStruct((n_batch * seq, width), BF16)]
    if ctx_out:
        out_specs.append(pl.BlockSpec((ctx_len, LANES), lambda b, p: (b, p)))
        out_shape.append(jax.ShapeDtypeStruct((n_batch * ctx_len, width), BF16))
    return pl.pallas_call(
        functools.partial(_na_kernel, rows=rows, gw=GRID_W, ctx_out=ctx_out),
        grid=(n_batch, pairs),
        in_specs=in_specs,
        out_specs=out_specs,
        out_shape=out_shape,
        scratch_shapes=[pltpu.VMEM((2, 2 * GRID_W, n_keys), F32),
                        pltpu.VMEM((2, 2 * GRID_W, n_keys), BF16),
                        pltpu.VMEM((2, 2 * GRID_W, LANES), F32)],
        compiler_params=_cparams("parallel", "parallel"),
        name="neighbourhood_attention",
    )(*args)


def _hg_kernel(*refs, rev, n_chunks, n_blocks, epilogue):
    if epilogue:
        q_ref, f_ref, i_ref, s0_ref, pm_ref, rm_ref, g_ref, prev_ref, ng_ref, o_ref, sT_ref = refs[:11]
    else:
        q_ref, f_ref, i_ref, s0_ref, pm_ref, rm_ref, o_ref, sT_ref = refs[:8]
    st_scr, att_scr, qd_scr, ku_scr, dec_scr = refs[-5:]
    blk = pl.program_id(2)

    @pl.when(blk == 0)
    def _():
        st_scr[...] = s0_ref[0, 0]

    c = HG_CHUNK

    def rows_of(ci):
        ci = jnp.minimum(ci, n_chunks - 1)
        cc = (n_chunks - 1 - ci) if rev else ci
        return pl.ds(pl.multiple_of(cc * c, c), c)

    def decays(ci, slot):
        sl = rows_of(ci)
        q = q_ref[sl, :].astype(F32)
        lf = f_ref[sl, :] * LOG2_E
        k = 1.0 - jnp.exp2(lf)
        p, tot = lf, lf
        att = pm_ref[0] * jnp.sum(q * k, axis=-1, keepdims=True)
        for lvl in range(1, pm_ref.shape[0]):
            m = 1 << (lvl - 1)
            second = rm_ref[lvl - 1] != 0.0
            is_q = jnp.logical_not(second) if rev else second
            z = (jnp.where(is_q, q, k) * jnp.exp2(jnp.where(is_q, p, tot - p))).astype(BF16)
            att = att + pm_ref[lvl] * _dot_nt(z, z)
            t_up = pltpu.roll(tot, m, 0)
            t_dn = pltpu.roll(tot, c - m, 0)
            p = p + (jnp.where(second, 0.0, t_dn) if rev else jnp.where(second, t_up, 0.0))
            tot = tot + jnp.where(second, t_up, t_dn)
        att_scr[slot] = att.astype(BF16)
        qd_scr[slot] = (q * jnp.exp2(p)).astype(BF16)
        ku_scr[slot] = (k * jnp.exp2(tot - p)).astype(BF16)
        dec_scr[slot] = jnp.exp2(tot[0:8, :])

    def outputs(ci, slot):
        sl = rows_of(ci)
        v = i_ref[sl, :]
        st = st_scr[...]
        o = _dot(att_scr[slot], v) + _dot_nt(qd_scr[slot], st.astype(BF16))
        st_scr[...] = st * dec_scr[slot, 0:1, :] + _dot_tn(v, ku_scr[slot])
        if epilogue:
            o = o + prev_ref[sl, :]
            o = o * lax.rsqrt(jnp.mean(o * o, axis=-1, keepdims=True) + NORM_EPS) * ng_ref[...]
            o = o * g_ref[sl, :].astype(F32)
        o_ref[sl, :] = o.astype(o_ref.dtype)

    decays(0, 0)

    def pair(i, carry):
        ci = 2 * i
        outputs(ci, 0)
        decays(ci + 1, 1)
        outputs(ci + 1, 1)
        decays(ci + 2, 0)
        return carry

    lax.fori_loop(0, n_chunks // 2, pair, 0)

    @pl.when(blk == n_blocks - 1)
    def _():
        sT_ref[0, 0] = st_scr[...]


def _hg_level_masks(rev):
    c = HG_CHUNK
    t = np.arange(c)[:, None]
    s = np.arange(c)[None, :]
    pair, row = [t == s], []
    m = 1
    while m < c:
        t_second = (t & m) != 0
        s_second = (s & m) != 0
        same = (t ^ s) < 2 * m
        pair.append(same & (~t_second & s_second if rev else t_second & ~s_second))
        row.append(np.broadcast_to(t_second, (c, HG_KEY_DIM)))
        m *= 2
    return jnp.asarray(np.stack(pair), F32), jnp.asarray(np.stack(row), F32)


def _hg_scan(qg, lf, proj, s0, *, rev, row0, n_batch, seq, block_rows, n_heads, col_f, col_i, prev=None, norm_g=None):
    dk = HG_KEY_DIM
    n_blocks = seq // block_rows
    blk0 = row0 // block_rows
    epilogue = prev is not None

    def local_rows(b, i):
        return b * n_blocks + ((n_blocks - 1 - i) if rev else i)

    pspec = lambda col: pl.BlockSpec((block_rows, dk), lambda b, h, i: (blk0 + local_rows(b, i), col + h))
    state_spec = pl.BlockSpec((1, 1, dk, dk), lambda b, h, i: (b, h, 0, 0))
    local_spec = pl.BlockSpec((block_rows, dk), lambda b, h, i: (local_rows(b, i), h))
    pair_mask, row_mask = _hg_level_masks(rev)
    const_spec = lambda a: pl.BlockSpec(a.shape, lambda b, h, i: (0, 0, 0))
    in_specs = [pspec(0), pspec(col_f), pspec(col_i), state_spec, const_spec(pair_mask), const_spec(row_mask)]
    args = [qg, lf, proj, s0, pair_mask, row_mask]
    if epilogue:
        in_specs += [pspec(n_heads), local_spec, pl.BlockSpec((1, dk), lambda b, h, i: (0, 0))]
        args += [qg, prev, norm_g.reshape(1, dk)]
    return pl.pallas_call(
        functools.partial(_hg_kernel, rev=rev, n_chunks=block_rows // HG_CHUNK, n_blocks=n_blocks, epilogue=epilogue),
        grid=(n_batch, n_heads, n_blocks),
        in_specs=in_specs,
        out_specs=[local_spec, state_spec],
        out_shape=[jax.ShapeDtypeStruct((n_batch * seq, n_heads * dk), BF16 if epilogue else F32),
                   jax.ShapeDtypeStruct((n_batch, n_heads, dk, dk), F32)],
        scratch_shapes=[pltpu.VMEM((dk, dk), F32),
                        pltpu.VMEM((2, HG_CHUNK, HG_CHUNK), BF16),
                        pltpu.VMEM((2, HG_CHUNK, dk), BF16),
                        pltpu.VMEM((2, HG_CHUNK, dk), BF16),
                        pltpu.VMEM((2, 8, dk), F32)],
        compiler_params=_cparams("parallel", "parallel", "arbitrary"),
        name="hgrn2_scan_rev" if rev else "hgrn2_scan_fwd",
    )(*args)


def _ret_kernel(*refs, fwd, n_chunks, n_blocks):
    if fwd:
        (q_ref, k_ref, kd_ref, v_ref, lam_ref, s0_ref, g_ref, prev_ref, o_ref, sT_ref, st_scr, qd_scr, dm_scr,
         att_scr) = refs
    else:
        q_ref, kd_ref, v_ref, lam_ref, s0_ref, o_ref, sT_ref, st_scr, qd_scr = refs
    blk = pl.program_id(2)
    c = RET_CHUNK
    lam_f = lam_ref[0, 0:1, :]
    lam_b = lam_ref[0, 1:2, :]
    lam = lam_f if fwd else lam_b

    @pl.when(blk == 0)
    def _():
        st_scr[...] = s0_ref[0, 0]
        ipos = lax.broadcasted_iota(jnp.int32, (c, LANES), 0).astype(F32)
        steps = (ipos + 1.0) if fwd else (c - ipos)
        qd_scr[...] = jnp.exp(steps * lam[:, :LANES])
        if fwd:
            dist = (lax.broadcasted_iota(jnp.int32, (c, c), 0) - lax.broadcasted_iota(jnp.int32, (c, c), 1)).astype(F32)
            dm_scr[...] = (jnp.where(dist >= 0, jnp.exp(jnp.maximum(dist, 0.0) * lam_f[:, :c]), 0.0)
                           + jnp.where(dist <= 0, jnp.exp(jnp.maximum(-dist, 0.0) * lam_b[:, :c]), 0.0))

    chunk_decay = jnp.exp(float(c) * lam[:, 0:1])

    def rows_of(ci):
        ci = jnp.minimum(ci, n_chunks - 1)
        cc = ci if fwd else (n_chunks - 1 - ci)
        return pl.ds(pl.multiple_of(cc * c, c), c)

    def scores(ci, slot):
        sl = rows_of(ci)
        att_scr[slot] = (_dot_nt(q_ref[sl, :], k_ref[sl, :]) * dm_scr[...]).astype(BF16)

    def outputs(ci, slot):
        sl = rows_of(ci)
        q = q_ref[sl, :]
        v = v_ref[sl, :]
        st = st_scr[...]
        o = _dot(q, st.astype(BF16)) * qd_scr[:, 0:1]
        st_scr[...] = st * chunk_decay + _dot_tn(kd_ref[sl, :], v)
        if fwd:
            o = o + _dot(att_scr[slot], v) + prev_ref[sl, :].astype(F32)
            mu = jnp.mean(o, axis=-1, keepdims=True)
            oc = o - mu
            var = jnp.mean(oc * oc, axis=-1, keepdims=True)
            o = oc * lax.rsqrt(var + LN_EPS) * g_ref[sl, :].astype(F32)
        o_ref[sl, :] = o.astype(o_ref.dtype)

    if not fwd:
        def chunk(ci, carry):
            outputs(ci, 0)
            return carry

        lax.fori_loop(0, n_chunks, chunk, 0, unroll=2)
    elif n_chunks == 1:
        scores(0, 0)
        outputs(0, 0)
    else:
        scores(0, 0)

        def pair(i, carry):
            ci = 2 * i
            outputs(ci, 0)
            scores(ci + 1, 1)
            outputs(ci + 1, 1)
            scores(ci + 2, 0)
            return carry

        lax.fori_loop(0, n_chunks // 2, pair, 0)

    @pl.when(blk == n_blocks - 1)
    def _():
        sT_ref[0, 0] = st_scr[...]


def _ret_scan(q, k, kd, v, lam, s0, *, fwd, row0, n_batch, seq, block_rows, n_heads, gate=None, prev=None):
    dk, dv = RET_QK_DIM, RET_V_DIM
    n_blocks = seq // block_rows
    blk0 = row0 // block_rows
    assert RET_CHUNK <= dk

    def lrow(b, i):
        return b * n_blocks + (i if fwd else (n_blocks - 1 - i))

    kspec = pl.BlockSpec((block_rows, dk), lambda b, h, i: (blk0 + lrow(b, i), h))
    vspec = pl.BlockSpec((block_rows, dv), lambda b, h, i: (blk0 + lrow(b, i), h))
    local_spec = pl.BlockSpec((block_rows, dv), lambda b, h, i: (lrow(b, i), h))
    state_spec = pl.BlockSpec((1, 1, dk, dv), lambda b, h, i: (b, h, 0, 0))
    lam_spec = pl.BlockSpec((1, 2, dk), lambda b, h, i: (h, 0, 0))
    scratch = [pltpu.VMEM((dk, dv), F32), pltpu.VMEM((RET_CHUNK, LANES), F32)]
    if fwd:
        in_specs = [kspec, kspec, kspec, vspec, lam_spec, state_spec, vspec, local_spec]
        args = [q, k, kd, v, lam, s0, gate, prev]
        scratch += [pltpu.VMEM((RET_CHUNK, RET_CHUNK), F32),
                    pltpu.VMEM((2, RET_CHUNK, RET_CHUNK), BF16)]
    else:
        in_specs = [kspec, kspec, vspec, lam_spec, state_spec]
        args = [q, kd, v, lam, s0]
    return pl.pallas_call(
        functools.partial(_ret_kernel, fwd=fwd, n_chunks=block_rows // RET_CHUNK, n_blocks=n_blocks),
        grid=(n_batch, n_heads, n_blocks),
        in_specs=in_specs,
        out_specs=[local_spec, state_spec],
        out_shape=[jax.ShapeDtypeStruct((n_batch * seq, n_heads * dv), BF16),
                   jax.ShapeDtypeStruct((n_batch, n_heads, dk, dv), F32)],
        scratch_shapes=scratch,
        compiler_params=_cparams("parallel", "parallel", "arbitrary"),
        name="retention_fwd" if fwd else "retention_rev",
    )(*args)


def _rope_tables(seq, gw, pad_rows):
    half = LANES // 2
    inv = ROPE_BASE ** (-jnp.arange(half, dtype=F32) / half)
    t = jnp.arange(seq)
    ang_r = (t // gw).astype(F32)[:, None] * inv
    ang_c = (t % gw).astype(F32)[:, None] * inv
    cos = jnp.concatenate([jnp.cos(ang_r)] * 2 + [jnp.cos(ang_c)] * 2, axis=-1)
    sin = jnp.concatenate([-jnp.sin(ang_r), jnp.sin(ang_r), -jnp.sin(ang_c), jnp.sin(ang_c)], axis=-1)
    cos = jnp.concatenate([cos, jnp.ones((pad_rows, cos.shape[1]), F32)], axis=0)
    sin = jnp.concatenate([sin, jnp.zeros((pad_rows, sin.shape[1]), F32)], axis=0)
    return cos, sin


def _even_mixer(h, w_in, rpb, lb, norm_g, *, n_batch, seq, ctx_len, ctx_out, tm):
    n_x = n_batch * seq
    na_w = rpb.shape[0] * NA_HEAD_DIM
    hg_w = lb.shape[1]
    n_heads = hg_w // HG_KEY_DIM
    cuts = [0, 3 * na_w] + [3 * na_w + i * hg_w for i in range(1, 6)]
    seg = lambda i: w_in[:, cuts[i]:cuts[i + 1]]
    w = jnp.concatenate([seg(0), seg(4), seg(1), seg(5), seg(2), seg(3)], axis=1).astype(BF16)
    lb_rows = jnp.stack([jnp.log(lb).reshape(-1), jnp.log1p(-lb).reshape(-1)])
    groups = [(3 * na_w + hg_w, None, BF16),
              (2 * hg_w, "silu", BF16),
              (2 * hg_w, "log_forget", F32)]
    proj, qg, lf = _project(h, w, groups, tm=tm, params=[lb_rows])
    na = _neighbourhood_attention(proj, rpb, n_batch=n_batch, seq=seq, ctx_len=ctx_len, ctx_out=ctx_out)
    zeros = jnp.zeros((n_batch, n_heads, HG_KEY_DIM, HG_KEY_DIM), F32)
    col_i = 3 * na_w // HG_KEY_DIM
    ctx_kw = dict(row0=n_x, n_batch=n_batch, seq=ctx_len, block_rows=ctx_len, n_heads=n_heads, col_i=col_i)
    x_kw = dict(row0=0, n_batch=n_batch, seq=seq, block_rows=min(SCAN_BLOCK, seq), n_heads=n_heads, col_i=col_i)
    oc_f, sc_f = _hg_scan(qg, lf, proj, zeros, rev=False, col_f=0, **ctx_kw)
    gc, sc_b = _hg_scan(qg, lf, proj, zeros, rev=True, col_f=n_heads, prev=oc_f, norm_g=norm_g, **ctx_kw)
    ox_f, _ = _hg_scan(qg, lf, proj, sc_f, rev=False, col_f=0, **x_kw)
    gx, _ = _hg_scan(qg, lf, proj, sc_b, rev=True, col_f=n_heads, prev=ox_f, norm_g=norm_g, **x_kw)
    return [na[0], gx], ([na[1], gc] if ctx_out else None)


def _odd_mixer(h, w_in, log_decay, rope, *, n_batch, seq, ctx_len, ctx_out, tm):
    n_x = n_batch * seq
    n_heads = log_decay.shape[1]
    qk_w = n_heads * RET_QK_DIM
    v_w = n_heads * RET_V_DIM
    lam = jnp.broadcast_to(log_decay.astype(F32).T[:, :, None], (n_heads, 2, RET_QK_DIM))
    lam_rows = lam.transpose(1, 0, 2).reshape(2, qk_w)
    groups = [(qk_w, "rope", BF16), (qk_w, "ret_k", BF16), (v_w, None, BF16), (v_w, "silu", BF16)]
    q, k, kf, kb, v, g = _project(h, w_in.astype(BF16), groups, tm=tm, params=[lam_rows], rope=rope)
    zeros = jnp.zeros((n_batch, n_heads, RET_QK_DIM, RET_V_DIM), F32)
    ctx_kw = dict(row0=n_x, n_batch=n_batch, seq=ctx_len, block_rows=ctx_len, n_heads=n_heads)
    x_kw = dict(row0=0, n_batch=n_batch, seq=seq, block_rows=min(SCAN_BLOCK, seq), n_heads=n_heads)
    oc_b, sc_b = _ret_scan(q, k, kb, v, lam, zeros, fwd=False, **ctx_kw)
    yc, sc_f = _ret_scan(q, k, kf, v, lam, zeros, fwd=True, gate=g, prev=oc_b, **ctx_kw)
    ox_b, _ = _ret_scan(q, k, kb, v, lam, sc_b, fwd=False, **x_kw)
    yx, _ = _ret_scan(q, k, kf, v, lam, sc_f, fwd=True, gate=g, prev=ox_b, **x_kw)
    return [yx], ([yc] if ctx_out else None)


def kernel(x, c, ctx, c_ctx, ada_w, ada_b, ln_g, ln_b, e_w_in, e_w_out, na_rpb, hg_lb_logits, hg_norm_g, ffn_w1, ffn_w3,
           ffn_w2, o_w_in, o_w_out, ret_log_decay, router_w, router_b, moe_w1, moe_w3, moe_w2):
    n_batch, seq, d = x.shape
    ctx_len = ctx.shape[1]
    depth = ada_w.shape[0]
    n_x = n_batch * seq
    n_all = n_x + n_batch * ctx_len
    alpha = (2 * depth) ** 0.25
    row_gcd = math.gcd(math.gcd(n_x, n_all), seq)
    tm_even = _pick_tile(row_gcd, EVEN_PROJ_TILE, unit=RET_CHUNK)
    tm_odd = _pick_tile(row_gcd, ODD_PROJ_TILE, unit=RET_CHUNK)
    dims = dict(n_batch=n_batch, seq=seq, ctx_len=ctx_len)

    lb_cum = jnp.cumsum(jax.nn.softmax(hg_lb_logits.astype(F32), axis=1), axis=1)
    lower_bounds = lb_cum - lb_cum[:, :1]
    cos, sin = _rope_tables(seq, GRID_W, tm_odd)
    tiles_per_seq = seq // tm_odd
    rope = (cos, sin, lambda i: jnp.where(i < n_x // tm_odd, i % tiles_per_seq, tiles_per_seq))

    cond = jnp.concatenate([c, c_ctx[None, :], jnp.zeros((8 - n_batch - 1, d), F32)], axis=0)
    mods = [_silu_matmul(cond, ada_w[l], ada_b[l], tn=d).reshape(8, 6, d)[:n_batch + 1] for l in range(depth)]

    ffn_w = [w.astype(BF16)[:, None] for w in (ffn_w1, ffn_w3, ffn_w2)]
    moe_w = [moe_w1.astype(BF16), moe_w3.astype(BF16), moe_w2]

    tok = jnp.concatenate([x.reshape(n_x, d), ctx.reshape(n_batch * ctx_len, d)], axis=0)
    h = _modulate(tok, mods[0][:, 0:2], rows_per_batch=seq)
    for layer in range(depth):
        j = layer // 2
        last = layer == depth - 1
        mod = mods[layer]
        n_rows = n_x if last else n_all
        if layer % 2 == 0:
            parts_x, parts_c = _even_mixer(h, e_w_in[j], na_rpb[j], lower_bounds[:, j], hg_norm_g[j], ctx_out=not last,
                                           tm=tm_even, **dims)
            w_out, router = e_w_out[j], None
        else:
            parts_x, parts_c = _odd_mixer(h, o_w_in[j], ret_log_decay[j], rope, ctx_out=not last, tm=tm_odd, **dims)
            rw = jnp.zeros((d, LANES), F32).at[:, :N_EXPERTS].set(router_w[j])
            rb = jnp.zeros((1, LANES), F32).at[0, :N_EXPERTS].set(router_b[j])
            w_out, router = o_w_out[j], (rw, rb)
        lnp = lambda i: jnp.stack([ln_g[layer, i], ln_b[layer, i]])
        res = _mix_out(parts_x, parts_c, w_out.astype(BF16), tok, mod[:, 2:5], lnp(0), router, alpha=alpha,
                       rows_per_batch=seq)
        tok, h = res[0], res[1]
        nxt = mods[layer + 1][:, 0:2] if not last else jnp.stack([jnp.zeros_like(mod[:, 0])] * 2, axis=1)
        mod2 = jnp.concatenate([mod[:, 5:6], nxt], axis=1)
        if layer % 2 == 0:
            ones = jnp.ones((n_rows // ROW_TILE,), jnp.int32)
            res = _grouped_swiglu(h, *ffn_w, j, 0 * ones, ones, tm=ROW_TILE, tf=ffn_w1.shape[2],
                                  ln=dict(x=tok, mod=mod2, lnp=lnp(1), alpha=alpha, rows_per_batch=seq))
        else:
            gates, idx = res[2], res[3]
            slot_token, assign_slot, tile_expert, tile_valid = _route(idx[:, :2], N_EXPERTS, ROW_TILE)
            xs = jnp.take(h, slot_token, axis=0, mode="clip")
            ys = _grouped_swiglu(xs, *moe_w, j, tile_expert, tile_valid, tm=ROW_TILE,
                                 tf=_pick_tile(moe_w1.shape[3], 1792))
            y12 = jnp.take(ys, assign_slot.T.reshape(-1), axis=0, mode="clip")
            res = _residual_ln(tok, y12, mod2, lnp(1), alpha=alpha, rows_per_batch=seq, y_gates=gates)
        tok, h = res[0], res[1]
    return tok[:n_x].reshape(n_batch, seq, d)
```

```python
import functools
import math

import jax
import jax.numpy as jnp
import numpy as np
from jax import lax
from jax.experimental import pallas as pl
from jax.experimental.pallas import tpu as pltpu

F32 = jnp.float32
BF16 = jnp.bfloat16

GRID_W = 64
NA_HEAD_DIM = 64
NA_KH = 8
NA_KW = 16
HG_KEY_DIM = 128
RET_QK_DIM = 256
RET_V_DIM = 512
N_EXPERTS = 8
ROPE_BASE = 10000.0
LN_EPS = 1e-5
NORM_EPS = 1e-6
LOG2_E = 1.4426950408889634
LANES = 128
SUBLANES = 8
HG_CHUNK = 128
RET_CHUNK = 256
ROW_TILE = 512
EVEN_PROJ_TILE = 512
ODD_PROJ_TILE = 512
SCAN_BLOCK = 2048
FFN_SUB = 256
VMEM_LIMIT = 48 * 1024 * 1024
MASK_VALUE = -1e30


def _cparams(*sem):
    return pltpu.CompilerParams(dimension_semantics=sem, vmem_limit_bytes=VMEM_LIMIT)


def _dot(a, b):
    return jnp.dot(a, b, preferred_element_type=F32)


def _dot_nt(a, b):
    return lax.dot_general(a, b, (((1,), (1,)), ((), ())), preferred_element_type=F32)


def _dot_tn(a, b):
    return lax.dot_general(a, b, (((0,), (0,)), ((), ())), preferred_element_type=F32)


def _dot_3pass(a, b):
    a_hi = a.astype(BF16)
    b_hi = b.astype(BF16)
    a_lo = (a - a_hi.astype(F32)).astype(BF16)
    b_lo = (b - b_hi.astype(F32)).astype(BF16)
    return _dot(a_hi, b_hi) + _dot(a_lo, b_hi) + _dot(a_hi, b_lo)


def _silu(x):
    return x * jax.nn.sigmoid(x)


def _pick_tile(n, target, unit=LANES):
    best = None
    for t in range(unit, min(n, target) + 1, unit):
        if n % t == 0:
            best = t
    assert best is not None, (n, target, unit)
    return best


def _rope(x, cos, sin):
    half = LANES // 2
    swapped = jnp.concatenate([pltpu.roll(x[:, g * LANES:(g + 1) * LANES], half, 1) for g in range(x.shape[1] // LANES)],
                              axis=1)
    return x * cos + swapped * sin


def _mm_kernel(a_ref, w_ref, b_ref, o_ref):
    a = _silu(a_ref[...])
    o_ref[...] = _dot(a.astype(BF16), w_ref[...].astype(BF16)) + b_ref[...]


def _silu_matmul(a, w, bias, *, tn):
    m, k = a.shape
    n = w.shape[1]
    assert n % tn == 0
    return pl.pallas_call(
        _mm_kernel,
        grid=(n // tn,),
        in_specs=[pl.BlockSpec((m, k), lambda j: (0, 0)),
                  pl.BlockSpec((k, tn), lambda j: (0, j)),
                  pl.BlockSpec((1, tn), lambda j: (0, j))],
        out_specs=pl.BlockSpec((m, tn), lambda j: (0, j)),
        out_shape=jax.ShapeDtypeStruct((m, n), F32),
        compiler_params=_cparams("parallel"),
        name="silu_matmul",
    )(a, w, bias.reshape(1, n))


def _proj_kernel(*refs, groups, sub, has_rope):
    a_ref, w_ref = refs[0], refs[1]
    pos = 2
    if has_rope:
        cos_ref, sin_ref = refs[2], refs[3]
        pos = 4
    params = []
    for _, epilogue, _ in groups:
        params.append(refs[pos] if epilogue in ("log_forget", "ret_k") else None)
        pos += epilogue in ("log_forget", "ret_k")
    outs = refs[pos:]
    a = a_ref[...]
    c0 = 0
    oi = 0
    for (n_cols, epilogue, _), pr in zip(groups, params):
        for s0 in range(0, n_cols, sub):
            cols = slice(s0, s0 + sub)
            acc = _dot(a, w_ref[:, c0 + s0:c0 + s0 + sub])
            if epilogue is None:
                outs[oi][:, cols] = acc.astype(outs[oi].dtype)
            elif epilogue == "silu":
                outs[oi][:, cols] = _silu(acc).astype(outs[oi].dtype)
            elif epilogue == "log_forget":
                log_sig = jnp.minimum(acc, 0.0) - jnp.log(1.0 + jnp.exp(-jnp.abs(acc)))
                log_lb = pr[0:1, cols]
                bb = pr[1:2, cols] + log_sig
                outs[oi][:, cols] = jnp.maximum(log_lb, bb) + jnp.log(1.0 + jnp.exp(-jnp.abs(log_lb - bb)))
            else:
                reps = sub // cos_ref.shape[1]
                r = _rope(acc, jnp.concatenate([cos_ref[...]] * reps, axis=1), jnp.concatenate([sin_ref[...]] * reps, axis=1))
                if epilogue == "rope":
                    outs[oi][:, cols] = r.astype(outs[oi].dtype)
                else:
                    k = r * (RET_QK_DIM ** -0.5)
                    ipos = (lax.broadcasted_iota(jnp.int32, k.shape, 0) & (RET_CHUNK - 1)).astype(F32)
                    outs[oi][:, cols] = k.astype(BF16)
                    outs[oi + 1][:, cols] = (k * jnp.exp((RET_CHUNK - 1.0 - ipos) * pr[0:1, cols])).astype(BF16)
                    outs[oi + 2][:, cols] = (k * jnp.exp(ipos * pr[1:2, cols])).astype(BF16)
        c0 += n_cols
        oi += 3 if epilogue == "ret_k" else 1


def _project(a, w, groups, *, tm, params=(), rope=None, sub=512):
    n_rows, k = a.shape
    assert n_rows % tm == 0 and sum(g[0] for g in groups) == w.shape[1] and all(g[0] % sub == 0 for g in groups)
    in_specs = [pl.BlockSpec((tm, k), lambda i: (i, 0)), pl.BlockSpec(w.shape, lambda i: (0, 0))]
    args = [a, w]
    if rope is not None:
        cos, sin, block_fn = rope
        in_specs += [pl.BlockSpec((tm, cos.shape[1]), lambda i: (block_fn(i), 0))] * 2
        args += [cos, sin]
    for pr in params:
        in_specs.append(pl.BlockSpec(pr.shape, lambda i: (0, 0)))
        args.append(pr)
    out_specs, out_shape = [], []
    for n_cols, epilogue, dtype in groups:
        for _ in range(3 if epilogue == "ret_k" else 1):
            out_specs.append(pl.BlockSpec((tm, n_cols), lambda i: (i, 0)))
            out_shape.append(jax.ShapeDtypeStruct((n_rows, n_cols), dtype))
    return pl.pallas_call(
        functools.partial(_proj_kernel, groups=tuple(groups), sub=sub, has_rope=rope is not None),
        grid=(n_rows // tm,),
        in_specs=in_specs,
        out_specs=out_specs,
        out_shape=out_shape,
        compiler_params=_cparams("parallel"),
        name="project",
    )(*args)


def _mix_out_kernel(*refs, n_parts, n_x_tiles, has_ctx, tail):
    n_in = n_parts * (2 if has_ctx else 1)
    w_ref = refs[n_in]
    is_ctx = pl.program_id(0) >= n_x_tiles
    acc = None
    k0 = 0
    for p in range(n_parts):
        if has_ctx:
            a = jnp.where(is_ctx, refs[2 * p + 1][...], refs[2 * p][...])
        else:
            a = refs[p][...]
        k1 = k0 + a.shape[1]
        part = _dot(a, w_ref[k0:k1, :])
        acc = part if acc is None else acc + part
        k0 = k1
    tail(acc, *refs[n_in + 1:])


def _mix_out(parts_x, parts_c, w, x, mod, lnp, router, *, alpha, rows_per_batch, tm=ROW_TILE):
    n_x = parts_x[0].shape[0]
    has_ctx = parts_c is not None
    n_c = parts_c[0].shape[0] if has_ctx else 0
    assert n_x % tm == 0 and n_c % tm == 0
    n_x_tiles, n_c_tiles = n_x // tm, n_c // tm
    in_specs, args = [], []
    for p, ax in enumerate(parts_x):
        in_specs.append(pl.BlockSpec((tm, ax.shape[1]), lambda i: (jnp.minimum(i, n_x_tiles - 1), 0)))
        args.append(ax)
        if has_ctx:
            in_specs.append(pl.BlockSpec((tm, ax.shape[1]), lambda i: (jnp.maximum(i - n_x_tiles, 0), 0)))
            args.append(parts_c[p])
    in_specs.append(pl.BlockSpec(w.shape, lambda i: (0, 0)))
    args.append(w)
    tail, t_in, t_args, out_specs, out_shape = _ln_tail_specs(
        x, mod, lnp, router, alpha=alpha, tm=tm, rows_per_batch=rows_per_batch, n_rows=n_x + n_c, row_of=lambda i: i)
    return pl.pallas_call(
        functools.partial(_mix_out_kernel, n_parts=len(parts_x), n_x_tiles=n_x_tiles, has_ctx=has_ctx, tail=tail),
        grid=(n_x_tiles + n_c_tiles,),
        in_specs=in_specs + t_in,
        out_specs=out_specs,
        out_shape=out_shape,
        compiler_params=_cparams("parallel"),
        name="mix_out",
    )(*args, *t_args)


def _modulate_kernel(x_ref, mod_ref, h_ref):
    m = mod_ref[0]
    h_ref[...] = (x_ref[...] * (1.0 + m[1:2]) + m[0:1]).astype(h_ref.dtype)


def _mod_index(tiles_per_batch, n_batch):
    return lambda i: (jnp.minimum(i // tiles_per_batch, n_batch), 0, 0)


def _modulate(x, mod, *, rows_per_batch, tm=ROW_TILE):
    m, d = x.shape
    return pl.pallas_call(
        _modulate_kernel,
        grid=(m // tm,),
        in_specs=[pl.BlockSpec((tm, d), lambda i: (i, 0)),
                  pl.BlockSpec((1, 2, d), _mod_index(rows_per_batch // tm, mod.shape[0] - 1))],
        out_specs=pl.BlockSpec((tm, d), lambda i: (i, 0)),
        out_shape=jax.ShapeDtypeStruct((m, d), BF16),
        compiler_params=_cparams("parallel"),
        name="modulate",
    )(x, mod)


def _ln_tail(y, x_ref, mod_ref, lnp_ref, *rest, alpha, router, n_experts):
    if router:
        rw_ref, rb_ref, xo_ref, h_ref, g_ref, i_ref = rest
    else:
        xo_ref, h_ref = rest
    m = mod_ref[0]
    z = alpha * x_ref[...] + m[0:1] * y
    mu = jnp.mean(z, axis=-1, keepdims=True)
    zc = z - mu
    var = jnp.mean(zc * zc, axis=-1, keepdims=True)
    xn = zc * lax.rsqrt(var + LN_EPS) * lnp_ref[0:1, :] + lnp_ref[1:2, :]
    xo_ref[...] = xn
    hf = xn * (1.0 + m[2:3]) + m[1:2]
    h_ref[...] = hf.astype(h_ref.dtype)
    if router:
        logits = _dot_3pass(hf, rw_ref[...]) + rb_ref[...]
        lane = lax.broadcasted_iota(jnp.int32, logits.shape, 1).astype(F32)
        neg = -jnp.inf
        lg = jnp.where(lane < n_experts, logits, neg)
        m1 = jnp.max(lg, axis=-1, keepdims=True)
        i1 = jnp.min(jnp.where(lg == m1, lane, float(LANES)), axis=-1, keepdims=True)
        lg2 = jnp.where(lane == i1, neg, lg)
        m2 = jnp.max(lg2, axis=-1, keepdims=True)
        i2 = jnp.min(jnp.where(lg2 == m2, lane, float(LANES)), axis=-1, keepdims=True)
        e2 = jnp.exp(m2 - m1)
        den = 1.0 + e2
        g_ref[...] = jnp.where(lane == 0, 1.0 / den, jnp.where(lane == 1, e2 / den, 0.0))
        i_ref[...] = jnp.where(lane == 0, i1, jnp.where(lane == 1, i2, 0.0)).astype(jnp.int32)


def _ln_tail_specs(x, mod, lnp, router, *, alpha, tm, rows_per_batch, n_rows, row_of):
    d = x.shape[1]
    tiles_per_batch, n_batch = rows_per_batch // tm, mod.shape[0] - 1
    row = pl.BlockSpec((tm, d), lambda *g: (row_of(*g), 0))
    lane_row = pl.BlockSpec((tm, LANES), lambda *g: (row_of(*g), 0))
    const = lambda shape: pl.BlockSpec(shape, lambda *g: (0,) * len(shape))
    in_specs = [row, pl.BlockSpec((1, 3, d), lambda *g: (jnp.minimum(row_of(*g) // tiles_per_batch, n_batch), 0, 0)),
                const((2, d))]
    args = [x, mod, lnp]
    out_specs = [row, row]
    out_shape = [jax.ShapeDtypeStruct((n_rows, d), F32), jax.ShapeDtypeStruct((n_rows, d), BF16 if router is None else F32)]
    if router is not None:
        in_specs += [const((d, LANES)), const((1, LANES))]
        args += list(router)
        out_specs += [lane_row, lane_row]
        out_shape += [jax.ShapeDtypeStruct((n_rows, LANES), F32), jax.ShapeDtypeStruct((n_rows, LANES), jnp.int32)]
    tail = functools.partial(_ln_tail, alpha=alpha, router=router is not None, n_experts=N_EXPERTS)
    return tail, in_specs, args, out_specs, out_shape


def _ln_kernel(*refs, two_y, tail):
    if two_y:
        y_ref, y2_ref, yg_ref = refs[:3]
        yg = yg_ref[...]
        y = yg[:, 0:1] * y_ref[...].astype(F32) + yg[:, 1:2] * y2_ref[...].astype(F32)
        refs = refs[3:]
    else:
        y = refs[0][...].astype(F32)
        refs = refs[1:]
    tail(y, *refs)


def _residual_ln(x, y, mod, lnp, *, alpha, rows_per_batch, n_rows=None, y_gates=None, router=None, tm=ROW_TILE):
    d = x.shape[1]
    n_rows = x.shape[0] if n_rows is None else n_rows
    row = pl.BlockSpec((tm, d), lambda i: (i, 0))
    two_y = y_gates is not None
    in_specs, args = [row], [y]
    if two_y:
        second_half = n_rows // tm
        in_specs += [pl.BlockSpec((tm, d), lambda i: (second_half + i, 0)), pl.BlockSpec((tm, LANES), lambda i: (i, 0))]
        args += [y, y_gates]
    tail, t_in, t_args, out_specs, out_shape = _ln_tail_specs(
        x, mod, lnp, router, alpha=alpha, tm=tm, rows_per_batch=rows_per_batch, n_rows=n_rows, row_of=lambda i: i)
    return pl.pallas_call(
        functools.partial(_ln_kernel, two_y=two_y, tail=tail),
        grid=(n_rows // tm,),
        in_specs=in_specs + t_in,
        out_specs=out_specs,
        out_shape=out_shape,
        compiler_params=_cparams("parallel"),
        name="residual_ln",
    )(*args, *t_args)


def _gffn_kernel(te_ref, tv_ref, x_ref, w1_ref, w3_ref, w2_ref, *rest, n_f, tail):
    acc_scr = rest[-1]
    i, f = pl.program_id(0), pl.program_id(1)

    @pl.when(f == 0)
    def _():
        acc_scr[...] = jnp.zeros_like(acc_scr)

    @pl.when(tv_ref[i] != 0)
    def _():
        x = x_ref[...].astype(BF16)
        tf = w1_ref.shape[3]
        sub = _pick_tile(tf, FFN_SUB)
        part = None
        for c0 in range(0, tf, sub):
            h1 = _dot(x, w1_ref[0, 0, :, c0:c0 + sub])
            h3 = _dot(x, w3_ref[0, 0, :, c0:c0 + sub])
            a = (_silu(h1) * h3).astype(BF16)
            down = _dot(a, w2_ref[0, 0, c0:c0 + sub, :].astype(BF16))
            part = down if part is None else part + down
        acc_scr[...] += part

    @pl.when(f == n_f - 1)
    def _():
        if tail is None:
            rest[0][...] = acc_scr[...]
        else:
            tail(acc_scr[...], *rest[:-1])


def _grouped_swiglu(xs, w1, w3, w2, layer, tile_expert, tile_valid, *, tm, tf, ln=None):
    p, d = xs.shape
    f = w1.shape[3]
    assert p % tm == 0 and f % tf == 0
    if w1.shape[1] == 1:
        expert_of = lambda i, te: 0
        f_tile_of = lambda i, j, tv: j
    else:
        expert_of = lambda i, te: te[i]
        f_tile_of = lambda i, j, tv: j * tv[i]
    in_specs = [pl.BlockSpec((tm, d), lambda i, j, te, tv: (i, 0)),
                pl.BlockSpec((1, 1, d, tf), lambda i, j, te, tv: (layer, expert_of(i, te), 0, f_tile_of(i, j, tv))),
                pl.BlockSpec((1, 1, d, tf), lambda i, j, te, tv: (layer, expert_of(i, te), 0, f_tile_of(i, j, tv))),
                pl.BlockSpec((1, 1, tf, d), lambda i, j, te, tv: (layer, expert_of(i, te), f_tile_of(i, j, tv), 0))]
    args = [xs, w1, w3, w2]
    if ln is None:
        tail = None
        out_specs = pl.BlockSpec((tm, d), lambda i, j, te, tv: (i, 0))
        out_shape = jax.ShapeDtypeStruct((p, d), F32)
    else:
        tail, t_in, t_args, out_specs, out_shape = _ln_tail_specs(
            ln["x"], ln["mod"], ln["lnp"], None, alpha=ln["alpha"], tm=tm, rows_per_batch=ln["rows_per_batch"],
            n_rows=p, row_of=lambda i, j, te, tv: i)
        in_specs += t_in
        args += t_args
    grid_spec = pltpu.PrefetchScalarGridSpec(
        num_scalar_prefetch=2,
        grid=(p // tm, f // tf),
        in_specs=in_specs,
        out_specs=out_specs,
        scratch_shapes=[pltpu.VMEM((tm, d), F32)],
    )
    return pl.pallas_call(
        functools.partial(_gffn_kernel, n_f=f // tf, tail=tail),
        grid_spec=grid_spec,
        out_shape=out_shape,
        compiler_params=_cparams("parallel", "arbitrary"),
        name="grouped_swiglu",
    )(tile_expert, tile_valid, *args)


def _route(idx, n_experts, tm):
    m = idx.shape[0]
    flat = idx.reshape(-1)
    n = flat.shape[0]
    p = -(-(n + n_experts * (tm - 1)) // tm) * tm
    onehot = (flat[:, None] == jnp.arange(n_experts, dtype=jnp.int32)[None, :]).astype(jnp.int32)
    csum = jnp.cumsum(onehot, axis=0)
    counts = csum[-1]
    padded = -(-counts // tm) * tm
    pad_end = jnp.cumsum(padded)
    pad_off = pad_end - padded
    rank = jnp.sum(csum * onehot, axis=1) - 1
    assign_slot = (pad_off[flat] + rank).reshape(m, 2)
    slot = jnp.arange(p, dtype=jnp.int32)
    slot_expert = jnp.minimum(jnp.sum(pad_end[None, :] <= slot[:, None], axis=1), n_experts - 1).astype(jnp.int32)
    slot_rank = slot - pad_off[slot_expert]
    order = jnp.argsort(flat, stable=True).astype(jnp.int32)
    off = jnp.cumsum(counts) - counts
    src = order[jnp.minimum(off[slot_expert] + slot_rank, n - 1)]
    slot_token = jnp.where(slot_rank < counts[slot_expert], src // 2, slot % m).astype(jnp.int32)
    tile_start = slot[::tm]
    tile_valid = (tile_start < pad_end[-1]).astype(jnp.int32)
    tile_expert = slot_expert[::tm]
    return slot_token, assign_slot, tile_expert, tile_valid


def _softmax_pv(parts):
    mx = functools.reduce(jnp.maximum, [jnp.max(s, axis=-1, keepdims=True) for s, _ in parts])
    ps = [jnp.exp(s - mx) for s, _ in parts]
    den = functools.reduce(jnp.add, [jnp.sum(p, axis=-1, keepdims=True) for p in ps])
    num = functools.reduce(jnp.add, [_dot(p.astype(BF16), v) for p, (_, v) in zip(ps, parts)])
    return num / den


def _na_kernel(*refs, rows, gw, ctx_out):
    if ctx_out:
        q_ref, k_ref, v_ref, kc_ref, vc_ref, qc_ref, bias_ref, o_ref, oc_ref, s_scr, p_scr, l_scr = refs
    else:
        q_ref, k_ref, v_ref, kc_ref, vc_ref, bias_ref, o_ref, s_scr, p_scr, l_scr = refs
    lane = lax.broadcasted_iota(jnp.int32, (1, LANES), 1)
    head_lanes = [lane < NA_HEAD_DIM, lane >= NA_HEAD_DIM]
    scale = NA_HEAD_DIM ** -0.5
    band = NA_KH * gw

    def band_rows(r):
        b0 = jnp.clip(r - NA_KH // 2, 0, rows - NA_KH)
        return b0, pl.ds(pl.multiple_of(b0 * gw, gw), band)

    def scores(r, par):
        r = jnp.minimum(r, rows - 1)
        b0, ks = band_rows(r)
        dr0 = b0 - r + NA_KH - 1
        q = q_ref[pl.ds(pl.multiple_of(r * gw, gw), gw), :]
        q2 = jnp.concatenate([jnp.where(head_lanes[0], q, 0), jnp.where(head_lanes[1], q, 0)], axis=0) * scale
        s_scr[par, :, :band] = _dot_nt(q2, k_ref[ks, :]) + bias_ref[0, dr0]
        s_scr[par, :, band:] = _dot_nt(q2, kc_ref[...])

    def softmax(par):
        s = s_scr[par]
        e = jnp.exp(s - jnp.max(s, axis=-1, keepdims=True))
        l_scr[par] = jnp.broadcast_to(jnp.sum(e, axis=-1, keepdims=True), (2 * gw, LANES))
        p_scr[par] = e.astype(BF16)

    def weighted_values(r, par):
        _, ks = band_rows(r)
        num = _dot(p_scr[par, :, :band], v_ref[ks, :]) + _dot(p_scr[par, :, band:], vc_ref[...])
        out = num / l_scr[par]
        o_ref[pl.ds(pl.multiple_of(r * gw, gw), gw), :] = jnp.where(head_lanes[0], out[:gw], out[gw:]).astype(o_ref.dtype)

    scores(0, 0)
    scores(1, 1)
    softmax(0)

    def body(i, carry):
        r = 2 * i
        weighted_values(r, 0)
        softmax(1)
        scores(r + 2, 0)
        weighted_values(r + 1, 1)
        softmax(0)
        scores(r + 3, 1)
        return carry

    lax.fori_loop(0, rows // 2, body, 0)
    if ctx_out:
        qc = qc_ref[...]
        kc = kc_ref[...]
        vc = vc_ref[...]
        outs = []
        for h in range(2):
            qh = jnp.where(head_lanes[h], qc, 0) * scale
            outs.append(_softmax_pv([(_dot_nt(qh, kc), vc)]))
        oc_ref[...] = jnp.where(head_lanes[0], outs[0], outs[1]).astype(oc_ref.dtype)


def _na_bias_table(rpb, gw):
    col = jnp.arange(gw)
    c_start = jnp.clip(col - NA_KW // 2, 0, gw - NA_KW)
    col_in = (col[None, :] >= c_start[:, None]) & (col[None, :] < c_start[:, None] + NA_KW)
    dc = jnp.clip(col[None, :] - col[:, None], 1 - NA_KW, NA_KW - 1) + NA_KW - 1
    t = jnp.where(col_in[None, None], rpb[:, :, dc].astype(F32), MASK_VALUE)
    win = jnp.stack([t[:, d:d + NA_KH] for d in range(NA_KH)], axis=1)
    per_head = win.transpose(0, 1, 3, 2, 4).reshape(rpb.shape[0] // 2, 2, NA_KH, gw, NA_KH * gw)
    return per_head.transpose(0, 2, 1, 3, 4).reshape(rpb.shape[0] // 2, NA_KH, 2 * gw, NA_KH * gw)


def _neighbourhood_attention(proj, rpb, *, n_batch, seq, ctx_len, ctx_out):
    width = rpb.shape[0] * NA_HEAD_DIM
    pairs = width // LANES
    rows = seq // GRID_W
    bias = _na_bias_table(rpb, GRID_W)
    n_keys = NA_KH * GRID_W + ctx_len
    cblk0 = n_batch * seq // ctx_len
    xspec = lambda g: pl.BlockSpec((seq, LANES), lambda b, p: (b, g * pairs + p))
    cspec = lambda g: pl.BlockSpec((ctx_len, LANES), lambda b, p: (cblk0 + b, g * pairs + p))
    in_specs = [xspec(0), xspec(1), xspec(2), cspec(1), cspec(2)]
    args = [proj, proj, proj, proj, proj]
    if ctx_out:
        in_specs.append(cspec(0))
        args.append(proj)
    in_specs.append(pl.BlockSpec((1, NA_KH, 2 * GRID_W, NA_KH * GRID_W), lambda b, p: (p, 0, 0, 0)))
    args.append(bias)
    out_specs = [pl.BlockSpec((seq, LANES), lambda b, p: (b, p))]
    out_shape = [jax.ShapeDtypeStruct((n_batch * seq, width), BF16)]
    if ctx_out:
        out_specs.append(pl.BlockSpec((ctx_len, LANES), lambda b, p: (b, p)))
        out_shape.append(jax.ShapeDtypeStruct((n_batch * ctx_len, width), BF16))
    return pl.pallas_call(
        functools.partial(_na_kernel, rows=rows, gw=GRID_W, ctx_out=ctx_out),
        grid=(n_batch, pairs),
        in_specs=in_specs,
        out_specs=out_specs,
        out_shape=out_shape,
        scratch_shapes=[pltpu.VMEM((2, 2 * GRID_W, n_keys), F32),
                        pltpu.VMEM((2, 2 * GRID_W, n_keys), BF16),
                        pltpu.VMEM((2, 2 * GRID_W, LANES), F32)],
        compiler_params=_cparams("parallel", "parallel"),
        name="neighbourhood_attention",
    )(*args)


def _hg_kernel(*refs, rev, n_chunks, n_blocks, epilogue):
    if epilogue:
        q_ref, f_ref, i_ref, s0_ref, pm_ref, rm_ref, g_ref, prev_ref, ng_ref, o_ref, sT_ref = refs[:11]
    else:
        q_ref, f_ref, i_ref, s0_ref, pm_ref, rm_ref, o_ref, sT_ref = refs[:8]
    st_scr, att_scr, qd_scr, ku_scr, dec_scr = refs[-5:]
    blk = pl.program_id(2)

    @pl.when(blk == 0)
    def _():
        st_scr[...] = s0_ref[0, 0]

    c = HG_CHUNK

    def rows_of(ci):
        ci = jnp.minimum(ci, n_chunks - 1)
        cc = (n_chunks - 1 - ci) if rev else ci
        return pl.ds(pl.multiple_of(cc * c, c), c)

    def decays(ci, slot):
        sl = rows_of(ci)
        q = q_ref[sl, :].astype(F32)
        lf = f_ref[sl, :] * LOG2_E
        k = 1.0 - jnp.exp2(lf)
        p, tot = lf, lf
        att = pm_ref[0] * jnp.sum(q * k, axis=-1, keepdims=True)
        for lvl in range(1, pm_ref.shape[0]):
            m = 1 << (lvl - 1)
            if m < SUBLANES:
                second = rm_ref[lvl - 1] != 0.0
                is_q = jnp.logical_not(second) if rev else second
                z = (jnp.where(is_q, q, k) * jnp.exp2(jnp.where(is_q, p, tot - p))).astype(BF16)
                t_up = pltpu.roll(tot, m, 0)
                t_dn = pltpu.roll(tot, c - m, 0)
                p = p + (jnp.where(second, 0.0, t_dn) if rev else jnp.where(second, t_up, 0.0))
                tot = tot + jnp.where(second, t_up, t_dn)
            else:
                halves = lambda a: a.reshape(c // (2 * m), 2, m, a.shape[-1])
                join = lambda first, second: jnp.stack([first, second], axis=1).reshape(c, first.shape[-1])
                q4, k4, p4, t4 = halves(q), halves(k), halves(p), halves(tot)
                qh, kh = (0, 1) if rev else (1, 0)
                zq = q4[:, qh] * jnp.exp2(p4[:, qh])
                zk = k4[:, kh] * jnp.exp2(t4[:, kh] - p4[:, kh])
                z = (join(zq, zk) if rev else join(zk, zq)).astype(BF16)
                p_q = p4[:, qh] + t4[:, kh]
                p = join(p_q, p4[:, 1]) if rev else join(p4[:, 0], p_q)
                block_total = t4[:, 0] + t4[:, 1]
                tot = join(block_total, block_total)
            att = att + pm_ref[lvl] * _dot_nt(z, z)
        att_scr[slot] = att.astype(BF16)
        qd_scr[slot] = (q * jnp.exp2(p)).astype(BF16)
        ku_scr[slot] = (k * jnp.exp2(tot - p)).astype(BF16)
        dec_scr[slot] = jnp.exp2(tot[0:8, :])

    def outputs(ci, slot):
        sl = rows_of(ci)
        v = i_ref[sl, :]
        st = st_scr[...]
        o = _dot(att_scr[slot], v) + _dot_nt(qd_scr[slot], st.astype(BF16))
        st_scr[...] = st * dec_scr[slot, 0:1, :] + _dot_tn(v, ku_scr[slot])
        if epilogue:
            o = o + prev_ref[sl, :]
            o = o * lax.rsqrt(jnp.mean(o * o, axis=-1, keepdims=True) + NORM_EPS) * ng_ref[...]
            o = o * g_ref[sl, :].astype(F32)
        o_ref[sl, :] = o.astype(o_ref.dtype)

    decays(0, 0)

    def pair(i, carry):
        ci = 2 * i
        outputs(ci, 0)
        decays(ci + 1, 1)
        outputs(ci + 1, 1)
        decays(ci + 2, 0)
        return carry

    lax.fori_loop(0, n_chunks // 2, pair, 0)

    @pl.when(blk == n_blocks - 1)
    def _():
        sT_ref[0, 0] = st_scr[...]


def _hg_level_masks(rev):
    c = HG_CHUNK
    t = np.arange(c)[:, None]
    s = np.arange(c)[None, :]
    pair, row = [t == s], []
    m = 1
    while m < c:
        t_second = (t & m) != 0
        s_second = (s & m) != 0
        same = (t ^ s) < 2 * m
        pair.append(same & (~t_second & s_second if rev else t_second & ~s_second))
        if m < SUBLANES:
            row.append(np.broadcast_to(t_second, (c, HG_KEY_DIM)))
        m *= 2
    return jnp.asarray(np.stack(pair), F32), jnp.asarray(np.stack(row), F32)


def _hg_scan(qg, lf, proj, s0, *, rev, row0, n_batch, seq, block_rows, n_heads, col_f, col_i, prev=None, norm_g=None):
    dk = HG_KEY_DIM
    n_blocks = seq // block_rows
    blk0 = row0 // block_rows
    epilogue = prev is not None

    def local_rows(b, i):
        return b * n_blocks + ((n_blocks - 1 - i) if rev else i)

    pspec = lambda col: pl.BlockSpec((block_rows, dk), lambda b, h, i: (blk0 + local_rows(b, i), col + h))
    state_spec = pl.BlockSpec((1, 1, dk, dk), lambda b, h, i: (b, h, 0, 0))
    local_spec = pl.BlockSpec((block_rows, dk), lambda b, h, i: (local_rows(b, i), h))
    pair_mask, row_mask = _hg_level_masks(rev)
    const_spec = lambda a: pl.BlockSpec(a.shape, lambda b, h, i: (0, 0, 0))
    in_specs = [pspec(0), pspec(col_f), pspec(col_i), state_spec, const_spec(pair_mask), const_spec(row_mask)]
    args = [qg, lf, proj, s0, pair_mask, row_mask]
    if epilogue:
        in_specs += [pspec(n_heads), local_spec, pl.BlockSpec((1, dk), lambda b, h, i: (0, 0))]
        args += [qg, prev, norm_g.reshape(1, dk)]
    return pl.pallas_call(
        functools.partial(_hg_kernel, rev=rev, n_chunks=block_rows // HG_CHUNK, n_blocks=n_blocks, epilogue=epilogue),
        grid=(n_batch, n_heads, n_blocks),
        in_specs=in_specs,
        out_specs=[local_spec, state_spec],
        out_shape=[jax.ShapeDtypeStruct((n_batch * seq, n_heads * dk), BF16 if epilogue else F32),
                   jax.ShapeDtypeStruct((n_batch, n_heads, dk, dk), F32)],
        scratch_shapes=[pltpu.VMEM((dk, dk), F32),
                        pltpu.VMEM((2, HG_CHUNK, HG_CHUNK), BF16),
                        pltpu.VMEM((2, HG_CHUNK, dk), BF16),
                        pltpu.VMEM((2, HG_CHUNK, dk), BF16),
                        pltpu.VMEM((2, 8, dk), F32)],
        compiler_params=_cparams("parallel", "parallel", "arbitrary"),
        name="hgrn2_scan_rev" if rev else "hgrn2_scan_fwd",
    )(*args)


def _ret_kernel(*refs, fwd, n_chunks, n_blocks):
    if fwd:
        (q_ref, k_ref, kd_ref, v_ref, lam_ref, s0_ref, g_ref, prev_ref, o_ref, sT_ref, st_scr, qd_scr, dm_scr,
         att_scr) = refs
    else:
        q_ref, kd_ref, v_ref, lam_ref, s0_ref, o_ref, sT_ref, st_scr, qd_scr = refs
    blk = pl.program_id(2)
    c = RET_CHUNK
    lam_f = lam_ref[0, 0:1, :]
    lam_b = lam_ref[0, 1:2, :]
    lam = lam_f if fwd else lam_b

    @pl.when(blk == 0)
    def _():
        st_scr[...] = s0_ref[0, 0]
        ipos = lax.broadcasted_iota(jnp.int32, (c, LANES), 0).astype(F32)
        steps = (ipos + 1.0) if fwd else (c - ipos)
        qd_scr[...] = jnp.exp(steps * lam[:, :LANES])
        if fwd:
            dist = (lax.broadcasted_iota(jnp.int32, (c, c), 0) - lax.broadcasted_iota(jnp.int32, (c, c), 1)).astype(F32)
            dm_scr[...] = (jnp.where(dist >= 0, jnp.exp(jnp.maximum(dist, 0.0) * lam_f[:, :c]), 0.0)
                           + jnp.where(dist <= 0, jnp.exp(jnp.maximum(-dist, 0.0) * lam_b[:, :c]), 0.0))

    chunk_decay = jnp.exp(float(c) * lam[:, 0:1])

    def rows_of(ci):
        ci = jnp.minimum(ci, n_chunks - 1)
        cc = ci if fwd else (n_chunks - 1 - ci)
        return pl.ds(pl.multiple_of(cc * c, c), c)

    def scores(ci, slot):
        sl = rows_of(ci)
        att_scr[slot] = (_dot_nt(q_ref[sl, :], k_ref[sl, :]) * dm_scr[...]).astype(BF16)

    def outputs(ci, slot):
        sl = rows_of(ci)
        q = q_ref[sl, :]
        v = v_ref[sl, :]
        st = st_scr[...]
        o = _dot(q, st.astype(BF16)) * qd_scr[:, 0:1]
        st_scr[...] = st * chunk_decay + _dot_tn(kd_ref[sl, :], v)
        if fwd:
            o = o + _dot(att_scr[slot], v) + prev_ref[sl, :].astype(F32)
            mu = jnp.mean(o, axis=-1, keepdims=True)
            oc = o - mu
            var = jnp.mean(oc * oc, axis=-1, keepdims=True)
            o = oc * lax.rsqrt(var + LN_EPS) * g_ref[sl, :].astype(F32)
        o_ref[sl, :] = o.astype(o_ref.dtype)

    if not fwd:
        def chunk(ci, carry):
            outputs(ci, 0)
            return carry

        lax.fori_loop(0, n_chunks, chunk, 0, unroll=2)
    elif n_chunks == 1:
        scores(0, 0)
        outputs(0, 0)
    else:
        scores(0, 0)

        def pair(i, carry):
            ci = 2 * i
            outputs(ci, 0)
            scores(ci + 1, 1)
            outputs(ci + 1, 1)
            scores(ci + 2, 0)
            return carry

        lax.fori_loop(0, n_chunks // 2, pair, 0)

    @pl.when(blk == n_blocks - 1)
    def _():
        sT_ref[0, 0] = st_scr[...]


def _ret_scan(q, k, kd, v, lam, s0, *, fwd, row0, n_batch, seq, block_rows, n_heads, gate=None, prev=None):
    dk, dv = RET_QK_DIM, RET_V_DIM
    n_blocks = seq // block_rows
    blk0 = row0 // block_rows
    assert RET_CHUNK <= dk

    def lrow(b, i):
        return b * n_blocks + (i if fwd else (n_blocks - 1 - i))

    kspec = pl.BlockSpec((block_rows, dk), lambda b, h, i: (blk0 + lrow(b, i), h))
    vspec = pl.BlockSpec((block_rows, dv), lambda b, h, i: (blk0 + lrow(b, i), h))
    local_spec = pl.BlockSpec((block_rows, dv), lambda b, h, i: (lrow(b, i), h))
    state_spec = pl.BlockSpec((1, 1, dk, dv), lambda b, h, i: (b, h, 0, 0))
    lam_spec = pl.BlockSpec((1, 2, dk), lambda b, h, i: (h, 0, 0))
    scratch = [pltpu.VMEM((dk, dv), F32), pltpu.VMEM((RET_CHUNK, LANES), F32)]
    if fwd:
        in_specs = [kspec, kspec, kspec, vspec, lam_spec, state_spec, vspec, local_spec]
        args = [q, k, kd, v, lam, s0, gate, prev]
        scratch += [pltpu.VMEM((RET_CHUNK, RET_CHUNK), F32),
                    pltpu.VMEM((2, RET_CHUNK, RET_CHUNK), BF16)]
    else:
        in_specs = [kspec, kspec, vspec, lam_spec, state_spec]
        args = [q, kd, v, lam, s0]
    return pl.pallas_call(
        functools.partial(_ret_kernel, fwd=fwd, n_chunks=block_rows // RET_CHUNK, n_blocks=n_blocks),
        grid=(n_batch, n_heads, n_blocks),
        in_specs=in_specs,
        out_specs=[local_spec, state_spec],
        out_shape=[jax.ShapeDtypeStruct((n_batch * seq, n_heads * dv), BF16),
                   jax.ShapeDtypeStruct((n_batch, n_heads, dk, dv), F32)],
        scratch_shapes=scratch,
        compiler_params=_cparams("parallel", "parallel", "arbitrary"),
        name="retention_fwd" if fwd else "retention_rev",
    )(*args)


def _rope_tables(seq, gw, pad_rows):
    half = LANES // 2
    inv = ROPE_BASE ** (-jnp.arange(half, dtype=F32) / half)
    t = jnp.arange(seq)
    ang_r = (t // gw).astype(F32)[:, None] * inv
    ang_c = (t % gw).astype(F32)[:, None] * inv
    cos = jnp.concatenate([jnp.cos(ang_r)] * 2 + [jnp.cos(ang_c)] * 2, axis=-1)
    sin = jnp.concatenate([-jnp.sin(ang_r), jnp.sin(ang_r), -jnp.sin(ang_c), jnp.sin(ang_c)], axis=-1)
    cos = jnp.concatenate([cos, jnp.ones((pad_rows, cos.shape[1]), F32)], axis=0)
    sin = jnp.concatenate([sin, jnp.zeros((pad_rows, sin.shape[1]), F32)], axis=0)
    return cos, sin


def _even_mixer(h, w_in, rpb, lb, norm_g, *, n_batch, seq, ctx_len, ctx_out, tm):
    n_x = n_batch * seq
    na_w = rpb.shape[0] * NA_HEAD_DIM
    hg_w = lb.shape[1]
    n_heads = hg_w // HG_KEY_DIM
    cuts = [0, 3 * na_w] + [3 * na_w + i * hg_w for i in range(1, 6)]
    seg = lambda i: w_in[:, cuts[i]:cuts[i + 1]]
    w = jnp.concatenate([seg(0), seg(4), seg(1), seg(5), seg(2), seg(3)], axis=1).astype(BF16)
    lb_rows = jnp.stack([jnp.log(lb).reshape(-1), jnp.log1p(-lb).reshape(-1)])
    groups = [(3 * na_w + hg_w, None, BF16),
              (2 * hg_w, "silu", BF16),
              (2 * hg_w, "log_forget", F32)]
    proj, qg, lf = _project(h, w, groups, tm=tm, params=[lb_rows])
    na = _neighbourhood_attention(proj, rpb, n_batch=n_batch, seq=seq, ctx_len=ctx_len, ctx_out=ctx_out)
    zeros = jnp.zeros((n_batch, n_heads, HG_KEY_DIM, HG_KEY_DIM), F32)
    col_i = 3 * na_w // HG_KEY_DIM
    ctx_kw = dict(row0=n_x, n_batch=n_batch, seq=ctx_len, block_rows=ctx_len, n_heads=n_heads, col_i=col_i)
    x_kw = dict(row0=0, n_batch=n_batch, seq=seq, block_rows=min(SCAN_BLOCK, seq), n_heads=n_heads, col_i=col_i)
    oc_f, sc_f = _hg_scan(qg, lf, proj, zeros, rev=False, col_f=0, **ctx_kw)
    gc, sc_b = _hg_scan(qg, lf, proj, zeros, rev=True, col_f=n_heads, prev=oc_f, norm_g=norm_g, **ctx_kw)
    ox_f, _ = _hg_scan(qg, lf, proj, sc_f, rev=False, col_f=0, **x_kw)
    gx, _ = _hg_scan(qg, lf, proj, sc_b, rev=True, col_f=n_heads, prev=ox_f, norm_g=norm_g, **x_kw)
    return [na[0], gx], ([na[1], gc] if ctx_out else None)


def _odd_mixer(h, w_in, log_decay, rope, *, n_batch, seq, ctx_len, ctx_out, tm):
    n_x = n_batch * seq
    n_heads = log_decay.shape[1]
    qk_w = n_heads * RET_QK_DIM
    v_w = n_heads * RET_V_DIM
    lam = jnp.broadcast_to(log_decay.astype(F32).T[:, :, None], (n_heads, 2, RET_QK_DIM))
    lam_rows = lam.transpose(1, 0, 2).reshape(2, qk_w)
    groups = [(qk_w, "rope", BF16), (qk_w, "ret_k", BF16), (v_w, None, BF16), (v_w, "silu", BF16)]
    q, k, kf, kb, v, g = _project(h, w_in.astype(BF16), groups, tm=tm, params=[lam_rows], rope=rope)
    zeros = jnp.zeros((n_batch, n_heads, RET_QK_DIM, RET_V_DIM), F32)
    ctx_kw = dict(row0=n_x, n_batch=n_batch, seq=ctx_len, block_rows=ctx_len, n_heads=n_heads)
    x_kw = dict(row0=0, n_batch=n_batch, seq=seq, block_rows=min(SCAN_BLOCK, seq), n_heads=n_heads)
    oc_b, sc_b = _ret_scan(q, k, kb, v, lam, zeros, fwd=False, **ctx_kw)
    yc, sc_f = _ret_scan(q, k, kf, v, lam, zeros, fwd=True, gate=g, prev=oc_b, **ctx_kw)
    ox_b, _ = _ret_scan(q, k, kb, v, lam, sc_b, fwd=False, **x_kw)
    yx, _ = _ret_scan(q, k, kf, v, lam, sc_f, fwd=True, gate=g, prev=ox_b, **x_kw)
    return [yx], ([yc] if ctx_out else None)


def kernel(x, c, ctx, c_ctx, ada_w, ada_b, ln_g, ln_b, e_w_in, e_w_out, na_rpb, hg_lb_logits, hg_norm_g, ffn_w1, ffn_w3,
           ffn_w2, o_w_in, o_w_out, ret_log_decay, router_w, router_b, moe_w1, moe_w3, moe_w2):
    n_batch, seq, d = x.shape
    ctx_len = ctx.shape[1]
    depth = ada_w.shape[0]
    n_x = n_batch * seq
    n_all = n_x + n_batch * ctx_len
    alpha = (2 * depth) ** 0.25
    row_gcd = math.gcd(math.gcd(n_x, n_all), seq)
    tm_even = _pick_tile(row_gcd, EVEN_PROJ_TILE, unit=RET_CHUNK)
    tm_odd = _pick_tile(row_gcd, ODD_PROJ_TILE, unit=RET_CHUNK)
    dims = dict(n_batch=n_batch, seq=seq, ctx_len=ctx_len)

    lb_cum = jnp.cumsum(jax.nn.softmax(hg_lb_logits.astype(F32), axis=1), axis=1)
    lower_bounds = lb_cum - lb_cum[:, :1]
    cos, sin = _rope_tables(seq, GRID_W, tm_odd)
    tiles_per_seq = seq // tm_odd
    rope = (cos, sin, lambda i: jnp.where(i < n_x // tm_odd, i % tiles_per_seq, tiles_per_seq))

    cond = jnp.concatenate([c, c_ctx[None, :], jnp.zeros((8 - n_batch - 1, d), F32)], axis=0)
    mods = [_silu_matmul(cond, ada_w[l], ada_b[l], tn=d).reshape(8, 6, d)[:n_batch + 1] for l in range(depth)]

    ffn_w = [w.astype(BF16)[:, None] for w in (ffn_w1, ffn_w3, ffn_w2)]
    moe_w = [moe_w1.astype(BF16), moe_w3.astype(BF16), moe_w2]

    tok = jnp.concatenate([x.reshape(n_x, d), ctx.reshape(n_batch * ctx_len, d)], axis=0)
    h = _modulate(tok, mods[0][:, 0:2], rows_per_batch=seq)
    for layer in range(depth):
        j = layer // 2
        last = layer == depth - 1
        mod = mods[layer]
        n_rows = n_x if last else n_all
        if layer % 2 == 0:
            parts_x, parts_c = _even_mixer(h, e_w_in[j], na_rpb[j], lower_bounds[:, j], hg_norm_g[j], ctx_out=not last,
                                           tm=tm_even, **dims)
            w_out, router = e_w_out[j], None
        else:
            parts_x, parts_c = _odd_mixer(h, o_w_in[j], ret_log_decay[j], rope, ctx_out=not last, tm=tm_odd, **dims)
            rw = jnp.zeros((d, LANES), F32).at[:, :N_EXPERTS].set(router_w[j])
            rb = jnp.zeros((1, LANES), F32).at[0, :N_EXPERTS].set(router_b[j])
            w_out, router = o_w_out[j], (rw, rb)
        lnp = lambda i: jnp.stack([ln_g[layer, i], ln_b[layer, i]])
        res = _mix_out(parts_x, parts_c, w_out.astype(BF16), tok, mod[:, 2:5], lnp(0), router, alpha=alpha,
                       rows_per_batch=seq)
        tok, h = res[0], res[1]
        nxt = mods[layer + 1][:, 0:2] if not last else jnp.stack([jnp.zeros_like(mod[:, 0])] * 2, axis=1)
        mod2 = jnp.concatenate([mod[:, 5:6], nxt], axis=1)
        if layer % 2 == 0:
            ones = jnp.ones((n_rows // ROW_TILE,), jnp.int32)
            res = _grouped_swiglu(h, *ffn_w, j, 0 * ones, ones, tm=ROW_TILE, tf=ffn_w1.shape[2],
                                  ln=dict(x=tok, mod=mod2, lnp=lnp(1), alpha=alpha, rows_per_batch=seq))
        else:
            gates, idx = res[2], res[3]
            slot_token, assign_slot, tile_expert, tile_valid = _route(idx[:, :2], N_EXPERTS, ROW_TILE)
            xs = jnp.take(h, slot_token, axis=0, mode="clip")
            ys = _grouped_swiglu(xs, *moe_w, j, tile_expert, tile_valid, tm=ROW_TILE,
                                 tf=_pick_tile(moe_w1.shape[3], 1792))
            y12 = jnp.take(ys, assign_slot.T.reshape(-1), axis=0, mode="clip")
            res = _residual_ln(tok, y12, mod2, lnp(1), alpha=alpha, rows_per_batch=seq, y_gates=gates)
        tok, h = res[0], res[1]
    return tok[:n_x].reshape(n_batch, seq, d)
```

```python
import functools
import math

import jax
import jax.numpy as jnp
import numpy as np
from jax import lax
from jax.experimental import pallas as pl
from jax.experimental.pallas import tpu as pltpu

F32 = jnp.float32
BF16 = jnp.bfloat16

GRID_W = 64
NA_HEAD_DIM = 64
NA_KH = 8
NA_KW = 16
HG_KEY_DIM = 128
RET_QK_DIM = 256
RET_V_DIM = 512
N_EXPERTS = 8
ROPE_BASE = 10000.0
LN_EPS = 1e-5
NORM_EPS = 1e-6
LOG2_E = 1.4426950408889634
LANES = 128
SUBLANES = 8
HG_CHUNK = 128
RET_CHUNK = 256
ROW_TILE = 512
EVEN_PROJ_TILE = 512
ODD_PROJ_TILE = 512
SCAN_BLOCK = 2048
FFN_SUB = 256
VMEM_LIMIT = 48 * 1024 * 1024
MASK_VALUE = -1e30


def _cparams(*sem):
    return pltpu.CompilerParams(dimension_semantics=sem, vmem_limit_bytes=VMEM_LIMIT)


def _dot(a, b):
    return jnp.dot(a, b, preferred_element_type=F32)


def _dot_nt(a, b):
    return lax.dot_general(a, b, (((1,), (1,)), ((), ())), preferred_element_type=F32)


def _dot_tn(a, b):
    return lax.dot_general(a, b, (((0,), (0,)), ((), ())), preferred_element_type=F32)


def _dot_3pass(a, b):
    a_hi = a.astype(BF16)
    b_hi = b.astype(BF16)
    a_lo = (a - a_hi.astype(F32)).astype(BF16)
    b_lo = (b - b_hi.astype(F32)).astype(BF16)
    return _dot(a_hi, b_hi) + _dot(a_lo, b_hi) + _dot(a_hi, b_lo)


def _silu(x):
    return x * jax.nn.sigmoid(x)


def _pick_tile(n, target, unit=LANES):
    best = None
    for t in range(unit, min(n, target) + 1, unit):
        if n % t == 0:
            best = t
    assert best is not None, (n, target, unit)
    return best


def _rope(x, cos, sin):
    half = LANES // 2
    swapped = jnp.concatenate([pltpu.roll(x[:, g * LANES:(g + 1) * LANES], half, 1) for g in range(x.shape[1] // LANES)],
                              axis=1)
    return x * cos + swapped * sin


def _mm_kernel(a_ref, w_ref, b_ref, o_ref):
    a = _silu(a_ref[...])
    o_ref[0] = _dot(a.astype(BF16), w_ref[0].astype(BF16)) + b_ref[0]


def _silu_matmul(a, w, bias, *, tn):
    m, k = a.shape
    n_layers, _, n = w.shape
    assert n % tn == 0
    return pl.pallas_call(
        _mm_kernel,
        grid=(n_layers, n // tn),
        in_specs=[pl.BlockSpec((m, k), lambda l, j: (0, 0)),
                  pl.BlockSpec((1, k, tn), lambda l, j: (l, 0, j)),
                  pl.BlockSpec((1, 1, tn), lambda l, j: (l, 0, j))],
        out_specs=pl.BlockSpec((1, m, tn), lambda l, j: (l, 0, j)),
        out_shape=jax.ShapeDtypeStruct((n_layers, m, n), F32),
        compiler_params=_cparams("parallel", "parallel"),
        name="silu_matmul",
    )(a, w, bias.reshape(n_layers, 1, n))


def _proj_kernel(*refs, groups, sub, has_rope):
    a_ref, w_ref = refs[0], refs[1]
    pos = 2
    if has_rope:
        cos_ref, sin_ref = refs[2], refs[3]
        pos = 4
    params = []
    for _, epilogue, _ in groups:
        params.append(refs[pos] if epilogue in ("log_forget", "ret_k") else None)
        pos += epilogue in ("log_forget", "ret_k")
    outs = refs[pos:]
    a = a_ref[...]
    c0 = 0
    oi = 0
    for (n_cols, epilogue, _), pr in zip(groups, params):
        for s0 in range(0, n_cols, sub):
            cols = slice(s0, s0 + sub)
            acc = _dot(a, w_ref[:, c0 + s0:c0 + s0 + sub])
            if epilogue is None:
                outs[oi][:, cols] = acc.astype(outs[oi].dtype)
            elif epilogue == "silu":
                outs[oi][:, cols] = _silu(acc).astype(outs[oi].dtype)
            elif epilogue == "log_forget":
                log_sig = jnp.minimum(acc, 0.0) - jnp.log(1.0 + jnp.exp(-jnp.abs(acc)))
                log_lb = pr[0:1, cols]
                bb = pr[1:2, cols] + log_sig
                outs[oi][:, cols] = jnp.maximum(log_lb, bb) + jnp.log(1.0 + jnp.exp(-jnp.abs(log_lb - bb)))
            else:
                reps = sub // cos_ref.shape[1]
                r = _rope(acc, jnp.concatenate([cos_ref[...]] * reps, axis=1), jnp.concatenate([sin_ref[...]] * reps, axis=1))
                if epilogue == "rope":
                    outs[oi][:, cols] = r.astype(outs[oi].dtype)
                else:
                    k = r * (RET_QK_DIM ** -0.5)
                    ipos = (lax.broadcasted_iota(jnp.int32, k.shape, 0) & (RET_CHUNK - 1)).astype(F32)
                    outs[oi][:, cols] = k.astype(BF16)
                    outs[oi + 1][:, cols] = (k * jnp.exp((RET_CHUNK - 1.0 - ipos) * pr[0:1, cols])).astype(BF16)
                    outs[oi + 2][:, cols] = (k * jnp.exp(ipos * pr[1:2, cols])).astype(BF16)
        c0 += n_cols
        oi += 3 if epilogue == "ret_k" else 1


def _project(a, w, groups, *, tm, params=(), rope=None, sub=512):
    n_rows, k = a.shape
    assert n_rows % tm == 0 and sum(g[0] for g in groups) == w.shape[1] and all(g[0] % sub == 0 for g in groups)
    in_specs = [pl.BlockSpec((tm, k), lambda i: (i, 0)), pl.BlockSpec(w.shape, lambda i: (0, 0))]
    args = [a, w]
    if rope is not None:
        cos, sin, block_fn = rope
        in_specs += [pl.BlockSpec((tm, cos.shape[1]), lambda i: (block_fn(i), 0))] * 2
        args += [cos, sin]
    for pr in params:
        in_specs.append(pl.BlockSpec(pr.shape, lambda i: (0, 0)))
        args.append(pr)
    out_specs, out_shape = [], []
    for n_cols, epilogue, dtype in groups:
        for _ in range(3 if epilogue == "ret_k" else 1):
            out_specs.append(pl.BlockSpec((tm, n_cols), lambda i: (i, 0)))
            out_shape.append(jax.ShapeDtypeStruct((n_rows, n_cols), dtype))
    return pl.pallas_call(
        functools.partial(_proj_kernel, groups=tuple(groups), sub=sub, has_rope=rope is not None),
        grid=(n_rows // tm,),
        in_specs=in_specs,
        out_specs=out_specs,
        out_shape=out_shape,
        compiler_params=_cparams("parallel"),
        name="project",
    )(*args)


def _mix_out_kernel(*refs, n_parts, n_x_tiles, has_ctx, tail):
    n_in = n_parts * (2 if has_ctx else 1)
    w_ref = refs[n_in]
    is_ctx = pl.program_id(0) >= n_x_tiles
    acc = None
    k0 = 0
    for p in range(n_parts):
        if has_ctx:
            a = jnp.where(is_ctx, refs[2 * p + 1][...], refs[2 * p][...])
        else:
            a = refs[p][...]
        k1 = k0 + a.shape[1]
        part = _dot(a, w_ref[k0:k1, :])
        acc = part if acc is None else acc + part
        k0 = k1
    tail(acc, *refs[n_in + 1:])


def _mix_out(parts_x, parts_c, w, x, mod, lnp, router, *, alpha, rows_per_batch, tm=ROW_TILE):
    n_x = parts_x[0].shape[0]
    has_ctx = parts_c is not None
    n_c = parts_c[0].shape[0] if has_ctx else 0
    assert n_x % tm == 0 and n_c % tm == 0
    n_x_tiles, n_c_tiles = n_x // tm, n_c // tm
    in_specs, args = [], []
    for p, ax in enumerate(parts_x):
        in_specs.append(pl.BlockSpec((tm, ax.shape[1]), lambda i: (jnp.minimum(i, n_x_tiles - 1), 0)))
        args.append(ax)
        if has_ctx:
            in_specs.append(pl.BlockSpec((tm, ax.shape[1]), lambda i: (jnp.maximum(i - n_x_tiles, 0), 0)))
            args.append(parts_c[p])
    in_specs.append(pl.BlockSpec(w.shape, lambda i: (0, 0)))
    args.append(w)
    tail, t_in, t_args, out_specs, out_shape = _ln_tail_specs(
        x, mod, lnp, router, alpha=alpha, tm=tm, rows_per_batch=rows_per_batch, n_rows=n_x + n_c, row_of=lambda i: i)
    return pl.pallas_call(
        functools.partial(_mix_out_kernel, n_parts=len(parts_x), n_x_tiles=n_x_tiles, has_ctx=has_ctx, tail=tail),
        grid=(n_x_tiles + n_c_tiles,),
        in_specs=in_specs + t_in,
        out_specs=out_specs,
        out_shape=out_shape,
        compiler_params=_cparams("parallel"),
        name="mix_out",
    )(*args, *t_args)


def _modulate_kernel(x_ref, mod_ref, h_ref):
    m = mod_ref[0]
    h_ref[...] = (x_ref[...] * (1.0 + m[1:2]) + m[0:1]).astype(h_ref.dtype)


def _mod_index(tiles_per_batch, n_batch):
    return lambda i: (jnp.minimum(i // tiles_per_batch, n_batch), 0, 0)


def _modulate(x, mod, *, rows_per_batch, tm=ROW_TILE):
    m, d = x.shape
    return pl.pallas_call(
        _modulate_kernel,
        grid=(m // tm,),
        in_specs=[pl.BlockSpec((tm, d), lambda i: (i, 0)),
                  pl.BlockSpec((1, 2, d), _mod_index(rows_per_batch // tm, mod.shape[0] - 1))],
        out_specs=pl.BlockSpec((tm, d), lambda i: (i, 0)),
        out_shape=jax.ShapeDtypeStruct((m, d), BF16),
        compiler_params=_cparams("parallel"),
        name="modulate",
    )(x, mod)


def _ln_tail(y, x_ref, mod_ref, lnp_ref, *rest, alpha, router, n_experts):
    if router:
        rw_ref, rb_ref, xo_ref, h_ref, g_ref, i_ref = rest
    else:
        xo_ref, h_ref = rest
    m = mod_ref[0]
    z = alpha * x_ref[...] + m[0:1] * y
    mu = jnp.mean(z, axis=-1, keepdims=True)
    zc = z - mu
    var = jnp.mean(zc * zc, axis=-1, keepdims=True)
    xn = zc * lax.rsqrt(var + LN_EPS) * lnp_ref[0:1, :] + lnp_ref[1:2, :]
    xo_ref[...] = xn
    hf = xn * (1.0 + m[2:3]) + m[1:2]
    h_ref[...] = hf.astype(h_ref.dtype)
    if router:
        logits = _dot_3pass(hf, rw_ref[...]) + rb_ref[...]
        lane = lax.broadcasted_iota(jnp.int32, logits.shape, 1).astype(F32)
        neg = -jnp.inf
        lg = jnp.where(lane < n_experts, logits, neg)
        m1 = jnp.max(lg, axis=-1, keepdims=True)
        i1 = jnp.min(jnp.where(lg == m1, lane, float(LANES)), axis=-1, keepdims=True)
        lg2 = jnp.where(lane == i1, neg, lg)
        m2 = jnp.max(lg2, axis=-1, keepdims=True)
        i2 = jnp.min(jnp.where(lg2 == m2, lane, float(LANES)), axis=-1, keepdims=True)
        e2 = jnp.exp(m2 - m1)
        den = 1.0 + e2
        g_ref[...] = jnp.where(lane == 0, 1.0 / den, jnp.where(lane == 1, e2 / den, 0.0))
        i_ref[...] = jnp.where(lane == 0, i1, jnp.where(lane == 1, i2, 0.0)).astype(jnp.int32)


def _ln_tail_specs(x, mod, lnp, router, *, alpha, tm, rows_per_batch, n_rows, row_of):
    d = x.shape[1]
    tiles_per_batch, n_batch = rows_per_batch // tm, mod.shape[0] - 1
    row = pl.BlockSpec((tm, d), lambda *g: (row_of(*g), 0))
    lane_row = pl.BlockSpec((tm, LANES), lambda *g: (row_of(*g), 0))
    const = lambda shape: pl.BlockSpec(shape, lambda *g: (0,) * len(shape))
    in_specs = [row, pl.BlockSpec((1, 3, d), lambda *g: (jnp.minimum(row_of(*g) // tiles_per_batch, n_batch), 0, 0)),
                const((2, d))]
    args = [x, mod, lnp]
    out_specs = [row, row]
    out_shape = [jax.ShapeDtypeStruct((n_rows, d), F32), jax.ShapeDtypeStruct((n_rows, d), BF16 if router is None else F32)]
    if router is not None:
        in_specs += [const((d, LANES)), const((1, LANES))]
        args += list(router)
        out_specs += [lane_row, lane_row]
        out_shape += [jax.ShapeDtypeStruct((n_rows, LANES), F32), jax.ShapeDtypeStruct((n_rows, LANES), jnp.int32)]
    tail = functools.partial(_ln_tail, alpha=alpha, router=router is not None, n_experts=N_EXPERTS)
    return tail, in_specs, args, out_specs, out_shape


def _ln_kernel(*refs, two_y, tail):
    if two_y:
        y_ref, y2_ref, yg_ref = refs[:3]
        yg = yg_ref[...]
        y = yg[:, 0:1] * y_ref[...].astype(F32) + yg[:, 1:2] * y2_ref[...].astype(F32)
        refs = refs[3:]
    else:
        y = refs[0][...].astype(F32)
        refs = refs[1:]
    tail(y, *refs)


def _residual_ln(x, y, mod, lnp, *, alpha, rows_per_batch, n_rows=None, y_gates=None, router=None, tm=ROW_TILE):
    d = x.shape[1]
    n_rows = x.shape[0] if n_rows is None else n_rows
    row = pl.BlockSpec((tm, d), lambda i: (i, 0))
    two_y = y_gates is not None
    in_specs, args = [row], [y]
    if two_y:
        second_half = n_rows // tm
        in_specs += [pl.BlockSpec((tm, d), lambda i: (second_half + i, 0)), pl.BlockSpec((tm, LANES), lambda i: (i, 0))]
        args += [y, y_gates]
    tail, t_in, t_args, out_specs, out_shape = _ln_tail_specs(
        x, mod, lnp, router, alpha=alpha, tm=tm, rows_per_batch=rows_per_batch, n_rows=n_rows, row_of=lambda i: i)
    return pl.pallas_call(
        functools.partial(_ln_kernel, two_y=two_y, tail=tail),
        grid=(n_rows // tm,),
        in_specs=in_specs + t_in,
        out_specs=out_specs,
        out_shape=out_shape,
        compiler_params=_cparams("parallel"),
        name="residual_ln",
    )(*args, *t_args)


def _gffn_kernel(te_ref, tv_ref, x_ref, w1_ref, w3_ref, w2_ref, *rest, n_f, tail):
    acc_scr = rest[-1]
    i, f = pl.program_id(0), pl.program_id(1)

    @pl.when(f == 0)
    def _():
        acc_scr[...] = jnp.zeros_like(acc_scr)

    @pl.when(tv_ref[i] != 0)
    def _():
        x = x_ref[...].astype(BF16)
        tf = w1_ref.shape[3]
        sub = _pick_tile(tf, FFN_SUB)
        part = None
        for c0 in range(0, tf, sub):
            h1 = _dot(x, w1_ref[0, 0, :, c0:c0 + sub])
            h3 = _dot(x, w3_ref[0, 0, :, c0:c0 + sub])
            a = (_silu(h1) * h3).astype(BF16)
            down = _dot(a, w2_ref[0, 0, c0:c0 + sub, :].astype(BF16))
            part = down if part is None else part + down
        acc_scr[...] += part

    @pl.when(f == n_f - 1)
    def _():
        if tail is None:
            rest[0][...] = acc_scr[...]
        else:
            tail(acc_scr[...], *rest[:-1])


def _grouped_swiglu(xs, w1, w3, w2, layer, tile_expert, tile_valid, *, tm, tf, ln=None):
    p, d = xs.shape
    f = w1.shape[3]
    assert p % tm == 0 and f % tf == 0
    if w1.shape[1] == 1:
        expert_of = lambda i, te: 0
        f_tile_of = lambda i, j, tv: j
    else:
        expert_of = lambda i, te: te[i]
        f_tile_of = lambda i, j, tv: j * tv[i]
    in_specs = [pl.BlockSpec((tm, d), lambda i, j, te, tv: (i, 0)),
                pl.BlockSpec((1, 1, d, tf), lambda i, j, te, tv: (layer, expert_of(i, te), 0, f_tile_of(i, j, tv))),
                pl.BlockSpec((1, 1, d, tf), lambda i, j, te, tv: (layer, expert_of(i, te), 0, f_tile_of(i, j, tv))),
                pl.BlockSpec((1, 1, tf, d), lambda i, j, te, tv: (layer, expert_of(i, te), f_tile_of(i, j, tv), 0))]
    args = [xs, w1, w3, w2]
    if ln is None:
        tail = None
        out_specs = pl.BlockSpec((tm, d), lambda i, j, te, tv: (i, 0))
        out_shape = jax.ShapeDtypeStruct((p, d), F32)
    else:
        tail, t_in, t_args, out_specs, out_shape = _ln_tail_specs(
            ln["x"], ln["mod"], ln["lnp"], None, alpha=ln["alpha"], tm=tm, rows_per_batch=ln["rows_per_batch"],
            n_rows=p, row_of=lambda i, j, te, tv: i)
        in_specs += t_in
        args += t_args
    grid_spec = pltpu.PrefetchScalarGridSpec(
        num_scalar_prefetch=2,
        grid=(p // tm, f // tf),
        in_specs=in_specs,
        out_specs=out_specs,
        scratch_shapes=[pltpu.VMEM((tm, d), F32)],
    )
    return pl.pallas_call(
        functools.partial(_gffn_kernel, n_f=f // tf, tail=tail),
        grid_spec=grid_spec,
        out_shape=out_shape,
        compiler_params=_cparams("parallel", "arbitrary"),
        name="grouped_swiglu",
    )(tile_expert, tile_valid, *args)


def _route(idx, n_experts, tm):
    m = idx.shape[0]
    flat = idx.reshape(-1)
    n = flat.shape[0]
    p = -(-(n + n_experts * (tm - 1)) // tm) * tm
    onehot = (flat[:, None] == jnp.arange(n_experts, dtype=jnp.int32)[None, :]).astype(jnp.int32)
    csum = jnp.cumsum(onehot, axis=0)
    counts = csum[-1]
    padded = -(-counts // tm) * tm
    pad_end = jnp.cumsum(padded)
    pad_off = pad_end - padded
    rank = jnp.sum(csum * onehot, axis=1) - 1
    assign_slot = (pad_off[flat] + rank).reshape(m, 2)
    slot = jnp.arange(p, dtype=jnp.int32)
    slot_expert = jnp.minimum(jnp.sum(pad_end[None, :] <= slot[:, None], axis=1), n_experts - 1).astype(jnp.int32)
    slot_rank = slot - pad_off[slot_expert]
    order = jnp.argsort(flat, stable=True).astype(jnp.int32)
    off = jnp.cumsum(counts) - counts
    src = order[jnp.minimum(off[slot_expert] + slot_rank, n - 1)]
    slot_token = jnp.where(slot_rank < counts[slot_expert], src // 2, slot % m).astype(jnp.int32)
    tile_start = slot[::tm]
    tile_valid = (tile_start < pad_end[-1]).astype(jnp.int32)
    tile_expert = slot_expert[::tm]
    return slot_token, assign_slot, tile_expert, tile_valid


def _softmax_pv(parts):
    mx = functools.reduce(jnp.maximum, [jnp.max(s, axis=-1, keepdims=True) for s, _ in parts])
    ps = [jnp.exp(s - mx) for s, _ in parts]
    den = functools.reduce(jnp.add, [jnp.sum(p, axis=-1, keepdims=True) for p in ps])
    num = functools.reduce(jnp.add, [_dot(p.astype(BF16), v) for p, (_, v) in zip(ps, parts)])
    return num / den


def _na_kernel(*refs, rows, gw, ctx_out):
    if ctx_out:
        q_ref, k_ref, v_ref, kc_ref, vc_ref, qc_ref, bias_ref, o_ref, oc_ref, s_scr, p_scr, l_scr = refs
    else:
        q_ref, k_ref, v_ref, kc_ref, vc_ref, bias_ref, o_ref, s_scr, p_scr, l_scr = refs
    lane = lax.broadcasted_iota(jnp.int32, (1, LANES), 1)
    head_lanes = [lane < NA_HEAD_DIM, lane >= NA_HEAD_DIM]
    scale = NA_HEAD_DIM ** -0.5
    band = NA_KH * gw

    def band_rows(r):
        b0 = jnp.clip(r - NA_KH // 2, 0, rows - NA_KH)
        return b0, pl.ds(pl.multiple_of(b0 * gw, gw), band)

    def scores(r, par):
        r = jnp.minimum(r, rows - 1)
        b0, ks = band_rows(r)
        dr0 = b0 - r + NA_KH - 1
        q = q_ref[pl.ds(pl.multiple_of(r * gw, gw), gw), :]
        q2 = jnp.concatenate([jnp.where(head_lanes[0], q, 0), jnp.where(head_lanes[1], q, 0)], axis=0) * scale
        s_scr[par, :, :band] = _dot_nt(q2, k_ref[ks, :]) + bias_ref[0, dr0]
        s_scr[par, :, band:] = _dot_nt(q2, kc_ref[...])

    def softmax(par):
        s = s_scr[par]
        e = jnp.exp(s - jnp.max(s, axis=-1, keepdims=True))
        l_scr[par] = jnp.broadcast_to(jnp.sum(e, axis=-1, keepdims=True), (2 * gw, LANES))
        p_scr[par] = e.astype(BF16)

    def weighted_values(r, par):
        _, ks = band_rows(r)
        num = _dot(p_scr[par, :, :band], v_ref[ks, :]) + _dot(p_scr[par, :, band:], vc_ref[...])
        out = num / l_scr[par]
        o_ref[pl.ds(pl.multiple_of(r * gw, gw), gw), :] = jnp.where(head_lanes[0], out[:gw], out[gw:]).astype(o_ref.dtype)

    scores(0, 0)
    scores(1, 1)
    softmax(0)

    def body(i, carry):
        r = 2 * i
        weighted_values(r, 0)
        softmax(1)
        scores(r + 2, 0)
        weighted_values(r + 1, 1)
        softmax(0)
        scores(r + 3, 1)
        return carry

    lax.fori_loop(0, rows // 2, body, 0)
    if ctx_out:
        qc = qc_ref[...]
        kc = kc_ref[...]
        vc = vc_ref[...]
        outs = []
        for h in range(2):
            qh = jnp.where(head_lanes[h], qc, 0) * scale
            outs.append(_softmax_pv([(_dot_nt(qh, kc), vc)]))
        oc_ref[...] = jnp.where(head_lanes[0], outs[0], outs[1]).astype(oc_ref.dtype)


def _na_bias_table(rpb, gw):
    col = jnp.arange(gw)
    c_start = jnp.clip(col - NA_KW // 2, 0, gw - NA_KW)
    col_in = (col[None, :] >= c_start[:, None]) & (col[None, :] < c_start[:, None] + NA_KW)
    dc = jnp.clip(col[None, :] - col[:, None], 1 - NA_KW, NA_KW - 1) + NA_KW - 1
    t = jnp.where(col_in[None, None], rpb[:, :, dc].astype(F32), MASK_VALUE)
    win = jnp.stack([t[:, d:d + NA_KH] for d in range(NA_KH)], axis=1)
    per_head = win.transpose(0, 1, 3, 2, 4).reshape(rpb.shape[0] // 2, 2, NA_KH, gw, NA_KH * gw)
    return per_head.transpose(0, 2, 1, 3, 4).reshape(rpb.shape[0] // 2, NA_KH, 2 * gw, NA_KH * gw)


def _neighbourhood_attention(proj, rpb, *, n_batch, seq, ctx_len, ctx_out):
    width = rpb.shape[0] * NA_HEAD_DIM
    pairs = width // LANES
    rows = seq // GRID_W
    bias = _na_bias_table(rpb, GRID_W)
    n_keys = NA_KH * GRID_W + ctx_len
    cblk0 = n_batch * seq // ctx_len
    xspec = lambda g: pl.BlockSpec((seq, LANES), lambda b, p: (b, g * pairs + p))
    cspec = lambda g: pl.BlockSpec((ctx_len, LANES), lambda b, p: (cblk0 + b, g * pairs + p))
    in_specs = [xspec(0), xspec(1), xspec(2), cspec(1), cspec(2)]
    args = [proj, proj, proj, proj, proj]
    if ctx_out:
        in_specs.append(cspec(0))
        args.append(proj)
    in_specs.append(pl.BlockSpec((1, NA_KH, 2 * GRID_W, NA_KH * GRID_W), lambda b, p: (p, 0, 0, 0)))
    args.append(bias)
    out_specs = [pl.BlockSpec((seq, LANES), lambda b, p: (b, p))]
    out_shape = [jax.ShapeDtypeStruct((n_batch * seq, width), BF16)]
    if ctx_out:
        out_specs.append(pl.BlockSpec((ctx_len, LANES), lambda b, p: (b, p)))
        out_shape.append(jax.ShapeDtypeStruct((n_batch * ctx_len, width), BF16))
    return pl.pallas_call(
        functools.partial(_na_kernel, rows=rows, gw=GRID_W, ctx_out=ctx_out),
        grid=(n_batch, pairs),
        in_specs=in_specs,
        out_specs=out_specs,
        out_shape=out_shape,
        scratch_shapes=[pltpu.VMEM((2, 2 * GRID_W, n_keys), F32),
                        pltpu.VMEM((2, 2 * GRID_W, n_keys), BF16),
                        pltpu.VMEM((2, 2 * GRID_W, LANES), F32)],
        compiler_params=_cparams("parallel", "parallel"),
        name="neighbourhood_attention",
    )(*args)


def _hg_kernel(*refs, rev, n_chunks, n_blocks, epilogue):
    if epilogue:
        q_ref, f_ref, i_ref, s0_ref, pm_ref, rm_ref, g_ref, prev_ref, ng_ref, o_ref, sT_ref = refs[:11]
    else:
        q_ref, f_ref, i_ref, s0_ref, pm_ref, rm_ref, o_ref, sT_ref = refs[:8]
    st_scr, att_scr, qd_scr, ku_scr, dec_scr = refs[-5:]
    blk = pl.program_id(2)

    @pl.when(blk == 0)
    def _():
        st_scr[...] = s0_ref[0, 0]

    c = HG_CHUNK

    def rows_of(ci):
        ci = jnp.minimum(ci, n_chunks - 1)
        cc = (n_chunks - 1 - ci) if rev else ci
        return pl.ds(pl.multiple_of(cc * c, c), c)

    def decays(ci, slot):
        sl = rows_of(ci)
        q = q_ref[sl, :].astype(F32)
        lf = f_ref[sl, :] * LOG2_E
        k = 1.0 - jnp.exp2(lf)
        p, tot = lf, lf
        att = pm_ref[0] * jnp.sum(q * k, axis=-1, keepdims=True)
        for lvl in range(1, pm_ref.shape[0]):
            m = 1 << (lvl - 1)
            if m < SUBLANES:
                second = rm_ref[lvl - 1] != 0.0
                is_q = jnp.logical_not(second) if rev else second
                z = (jnp.where(is_q, q, k) * jnp.exp2(jnp.where(is_q, p, tot - p))).astype(BF16)
                t_up = pltpu.roll(tot, m, 0)
                t_dn = pltpu.roll(tot, c - m, 0)
                p = p + (jnp.where(second, 0.0, t_dn) if rev else jnp.where(second, t_up, 0.0))
                tot = tot + jnp.where(second, t_up, t_dn)
            else:
                halves = lambda a: a.reshape(c // (2 * m), 2, m, a.shape[-1])
                join = lambda first, second: jnp.stack([first, second], axis=1).reshape(c, first.shape[-1])
                q4, k4, p4, t4 = halves(q), halves(k), halves(p), halves(tot)
                qh, kh = (0, 1) if rev else (1, 0)
                zq = q4[:, qh] * jnp.exp2(p4[:, qh])
                zk = k4[:, kh] * jnp.exp2(t4[:, kh] - p4[:, kh])
                z = (join(zq, zk) if rev else join(zk, zq)).astype(BF16)
                p_q = p4[:, qh] + t4[:, kh]
                p = join(p_q, p4[:, 1]) if rev else join(p4[:, 0], p_q)
                block_total = t4[:, 0] + t4[:, 1]
                tot = join(block_total, block_total)
            att = att + pm_ref[lvl] * _dot_nt(z, z)
        att_scr[slot] = att.astype(BF16)
        qd_scr[slot] = (q * jnp.exp2(p)).astype(BF16)
        ku_scr[slot] = (k * jnp.exp2(tot - p)).astype(BF16)
        dec_scr[slot] = jnp.exp2(tot[0:8, :])

    def outputs(ci, slot):
        sl = rows_of(ci)
        v = i_ref[sl, :]
        st = st_scr[...]
        o = _dot(att_scr[slot], v) + _dot_nt(qd_scr[slot], st.astype(BF16))
        st_scr[...] = st * dec_scr[slot, 0:1, :] + _dot_tn(v, ku_scr[slot])
        if epilogue:
            o = o + prev_ref[sl, :]
            o = o * lax.rsqrt(jnp.mean(o * o, axis=-1, keepdims=True) + NORM_EPS) * ng_ref[...]
            o = o * g_ref[sl, :].astype(F32)
        o_ref[sl, :] = o.astype(o_ref.dtype)

    decays(0, 0)

    def pair(i, carry):
        ci = 2 * i
        outputs(ci, 0)
        decays(ci + 1, 1)
        outputs(ci + 1, 1)
        decays(ci + 2, 0)
        return carry

    lax.fori_loop(0, n_chunks // 2, pair, 0)

    @pl.when(blk == n_blocks - 1)
    def _():
        sT_ref[0, 0] = st_scr[...]


def _hg_level_masks(rev):
    c = HG_CHUNK
    t = np.arange(c)[:, None]
    s = np.arange(c)[None, :]
    pair, row = [t == s], []
    m = 1
    while m < c:
        t_second = (t & m) != 0
        s_second = (s & m) != 0
        same = (t ^ s) < 2 * m
        pair.append(same & (~t_second & s_second if rev else t_second & ~s_second))
        if m < SUBLANES:
            row.append(np.broadcast_to(t_second, (c, HG_KEY_DIM)))
        m *= 2
    return jnp.asarray(np.stack(pair), F32), jnp.asarray(np.stack(row), F32)


def _hg_scan(qg, lf, proj, s0, *, rev, row0, n_batch, seq, block_rows, n_heads, col_f, col_i, prev=None, norm_g=None):
    dk = HG_KEY_DIM
    n_blocks = seq // block_rows
    blk0 = row0 // block_rows
    epilogue = prev is not None

    def local_rows(b, i):
        return b * n_blocks + ((n_blocks - 1 - i) if rev else i)

    pspec = lambda col: pl.BlockSpec((block_rows, dk), lambda b, h, i: (blk0 + local_rows(b, i), col + h))
    state_spec = pl.BlockSpec((1, 1, dk, dk), lambda b, h, i: (b, h, 0, 0))
    local_spec = pl.BlockSpec((block_rows, dk), lambda b, h, i: (local_rows(b, i), h))
    pair_mask, row_mask = _hg_level_masks(rev)
    const_spec = lambda a: pl.BlockSpec(a.shape, lambda b, h, i: (0, 0, 0))
    in_specs = [pspec(0), pspec(col_f), pspec(col_i), state_spec, const_spec(pair_mask), const_spec(row_mask)]
    args = [qg, lf, proj, s0, pair_mask, row_mask]
    if epilogue:
        in_specs += [pspec(n_heads), local_spec, pl.BlockSpec((1, dk), lambda b, h, i: (0, 0))]
        args += [qg, prev, norm_g.reshape(1, dk)]
    return pl.pallas_call(
        functools.partial(_hg_kernel, rev=rev, n_chunks=block_rows // HG_CHUNK, n_blocks=n_blocks, epilogue=epilogue),
        grid=(n_batch, n_heads, n_blocks),
        in_specs=in_specs,
        out_specs=[local_spec, state_spec],
        out_shape=[jax.ShapeDtypeStruct((n_batch * seq, n_heads * dk), BF16 if epilogue else F32),
                   jax.ShapeDtypeStruct((n_batch, n_heads, dk, dk), F32)],
        scratch_shapes=[pltpu.VMEM((dk, dk), F32),
                        pltpu.VMEM((2, HG_CHUNK, HG_CHUNK), BF16),
                        pltpu.VMEM((2, HG_CHUNK, dk), BF16),
                        pltpu.VMEM((2, HG_CHUNK, dk), BF16),
                        pltpu.VMEM((2, 8, dk), F32)],
        compiler_params=_cparams("parallel", "parallel", "arbitrary"),
        name="hgrn2_scan_rev" if rev else "hgrn2_scan_fwd",
    )(*args)


def _ret_kernel(*refs, fwd, n_chunks, n_blocks):
    if fwd:
        (q_ref, k_ref, kd_ref, v_ref, lam_ref, s0_ref, g_ref, prev_ref, o_ref, sT_ref, st_scr, qd_scr, dm_scr,
         att_scr) = refs
    else:
        q_ref, kd_ref, v_ref, lam_ref, s0_ref, o_ref, sT_ref, st_scr, qd_scr = refs
    blk = pl.program_id(2)
    c = RET_CHUNK
    lam_f = lam_ref[0, 0:1, :]
    lam_b = lam_ref[0, 1:2, :]
    lam = lam_f if fwd else lam_b

    @pl.when(blk == 0)
    def _():
        st_scr[...] = s0_ref[0, 0]
        ipos = lax.broadcasted_iota(jnp.int32, (c, LANES), 0).astype(F32)
        steps = (ipos + 1.0) if fwd else (c - ipos)
        qd_scr[...] = jnp.exp(steps * lam[:, :LANES])
        if fwd:
            dist = (lax.broadcasted_iota(jnp.int32, (c, c), 0) - lax.broadcasted_iota(jnp.int32, (c, c), 1)).astype(F32)
            dm_scr[...] = (jnp.where(dist >= 0, jnp.exp(jnp.maximum(dist, 0.0) * lam_f[:, :c]), 0.0)
                           + jnp.where(dist <= 0, jnp.exp(jnp.maximum(-dist, 0.0) * lam_b[:, :c]), 0.0))

    chunk_decay = jnp.exp(float(c) * lam[:, 0:1])

    def rows_of(ci):
        ci = jnp.minimum(ci, n_chunks - 1)
        cc = ci if fwd else (n_chunks - 1 - ci)
        return pl.ds(pl.multiple_of(cc * c, c), c)

    def scores(ci, slot):
        sl = rows_of(ci)
        att_scr[slot] = (_dot_nt(q_ref[sl, :], k_ref[sl, :]) * dm_scr[...]).astype(BF16)

    def outputs(ci, slot):
        sl = rows_of(ci)
        q = q_ref[sl, :]
        v = v_ref[sl, :]
        st = st_scr[...]
        o = _dot(q, st.astype(BF16)) * qd_scr[:, 0:1]
        st_scr[...] = st * chunk_decay + _dot_tn(kd_ref[sl, :], v)
        if fwd:
            o = o + _dot(att_scr[slot], v) + prev_ref[sl, :].astype(F32)
            mu = jnp.mean(o, axis=-1, keepdims=True)
            oc = o - mu
            var = jnp.mean(oc * oc, axis=-1, keepdims=True)
            o = oc * lax.rsqrt(var + LN_EPS) * g_ref[sl, :].astype(F32)
        o_ref[sl, :] = o.astype(o_ref.dtype)

    if not fwd:
        def chunk(ci, carry):
            outputs(ci, 0)
            return carry

        lax.fori_loop(0, n_chunks, chunk, 0, unroll=2)
    elif n_chunks == 1:
        scores(0, 0)
        outputs(0, 0)
    else:
        scores(0, 0)

        def pair(i, carry):
            ci = 2 * i
            outputs(ci, 0)
            scores(ci + 1, 1)
            outputs(ci + 1, 1)
            scores(ci + 2, 0)
            return carry

        lax.fori_loop(0, n_chunks // 2, pair, 0)

    @pl.when(blk == n_blocks - 1)
    def _():
        sT_ref[0, 0] = st_scr[...]


def _ret_scan(q, k, kd, v, lam, s0, *, fwd, row0, n_batch, seq, block_rows, n_heads, gate=None, prev=None):
    dk, dv = RET_QK_DIM, RET_V_DIM
    n_blocks = seq // block_rows
    blk0 = row0 // block_rows
    assert RET_CHUNK <= dk

    def lrow(b, i):
        return b * n_blocks + (i if fwd else (n_blocks - 1 - i))

    kspec = pl.BlockSpec((block_rows, dk), lambda b, h, i: (blk0 + lrow(b, i), h))
    vspec = pl.BlockSpec((block_rows, dv), lambda b, h, i: (blk0 + lrow(b, i), h))
    local_spec = pl.BlockSpec((block_rows, dv), lambda b, h, i: (lrow(b, i), h))
    state_spec = pl.BlockSpec((1, 1, dk, dv), lambda b, h, i: (b, h, 0, 0))
    lam_spec = pl.BlockSpec((1, 2, dk), lambda b, h, i: (h, 0, 0))
    scratch = [pltpu.VMEM((dk, dv), F32), pltpu.VMEM((RET_CHUNK, LANES), F32)]
    if fwd:
        in_specs = [kspec, kspec, kspec, vspec, lam_spec, state_spec, vspec, local_spec]
        args = [q, k, kd, v, lam, s0, gate, prev]
        scratch += [pltpu.VMEM((RET_CHUNK, RET_CHUNK), F32),
                    pltpu.VMEM((2, RET_CHUNK, RET_CHUNK), BF16)]
    else:
        in_specs = [kspec, kspec, vspec, lam_spec, state_spec]
        args = [q, kd, v, lam, s0]
    return pl.pallas_call(
        functools.partial(_ret_kernel, fwd=fwd, n_chunks=block_rows // RET_CHUNK, n_blocks=n_blocks),
        grid=(n_batch, n_heads, n_blocks),
        in_specs=in_specs,
        out_specs=[local_spec, state_spec],
        out_shape=[jax.ShapeDtypeStruct((n_batch * seq, n_heads * dv), BF16),
                   jax.ShapeDtypeStruct((n_batch, n_heads, dk, dv), F32)],
        scratch_shapes=scratch,
        compiler_params=_cparams("parallel", "parallel", "arbitrary"),
        name="retention_fwd" if fwd else "retention_rev",
    )(*args)


def _rope_tables(seq, gw, pad_rows):
    half = LANES // 2
    inv = ROPE_BASE ** (-jnp.arange(half, dtype=F32) / half)
    t = jnp.arange(seq)
    ang_r = (t // gw).astype(F32)[:, None] * inv
    ang_c = (t % gw).astype(F32)[:, None] * inv
    cos = jnp.concatenate([jnp.cos(ang_r)] * 2 + [jnp.cos(ang_c)] * 2, axis=-1)
    sin = jnp.concatenate([-jnp.sin(ang_r), jnp.sin(ang_r), -jnp.sin(ang_c), jnp.sin(ang_c)], axis=-1)
    cos = jnp.concatenate([cos, jnp.ones((pad_rows, cos.shape[1]), F32)], axis=0)
    sin = jnp.concatenate([sin, jnp.zeros((pad_rows, sin.shape[1]), F32)], axis=0)
    return cos, sin


def _even_mixer(h, w_in, rpb, lb, norm_g, *, n_batch, seq, ctx_len, ctx_out, tm):
    n_x = n_batch * seq
    na_w = rpb.shape[0] * NA_HEAD_DIM
    hg_w = lb.shape[1]
    n_heads = hg_w // HG_KEY_DIM
    cuts = [0, 3 * na_w] + [3 * na_w + i * hg_w for i in range(1, 6)]
    seg = lambda i: w_in[:, cuts[i]:cuts[i + 1]]
    w = jnp.concatenate([seg(2), seg(3), seg(1), seg(5), seg(0), seg(4)], axis=1).astype(BF16)
    lb_rows = jnp.stack([jnp.log(lb).reshape(-1), jnp.log1p(-lb).reshape(-1)])
    groups = [(2 * hg_w, "log_forget", F32),
              (2 * hg_w, "silu", BF16),
              (3 * na_w + hg_w, None, BF16)]
    lf, qg, proj = _project(h, w, groups, tm=tm, params=[lb_rows])
    na = _neighbourhood_attention(proj, rpb, n_batch=n_batch, seq=seq, ctx_len=ctx_len, ctx_out=ctx_out)
    zeros = jnp.zeros((n_batch, n_heads, HG_KEY_DIM, HG_KEY_DIM), F32)
    col_i = 3 * na_w // HG_KEY_DIM
    ctx_kw = dict(row0=n_x, n_batch=n_batch, seq=ctx_len, block_rows=ctx_len, n_heads=n_heads, col_i=col_i)
    x_kw = dict(row0=0, n_batch=n_batch, seq=seq, block_rows=min(SCAN_BLOCK, seq), n_heads=n_heads, col_i=col_i)
    oc_f, sc_f = _hg_scan(qg, lf, proj, zeros, rev=False, col_f=0, **ctx_kw)
    gc, sc_b = _hg_scan(qg, lf, proj, zeros, rev=True, col_f=n_heads, prev=oc_f, norm_g=norm_g, **ctx_kw)
    ox_f, _ = _hg_scan(qg, lf, proj, sc_f, rev=False, col_f=0, **x_kw)
    gx, _ = _hg_scan(qg, lf, proj, sc_b, rev=True, col_f=n_heads, prev=ox_f, norm_g=norm_g, **x_kw)
    return [na[0], gx], ([na[1], gc] if ctx_out else None)


def _odd_mixer(h, w_in, log_decay, rope, *, n_batch, seq, ctx_len, ctx_out, tm):
    n_x = n_batch * seq
    n_heads = log_decay.shape[1]
    qk_w = n_heads * RET_QK_DIM
    v_w = n_heads * RET_V_DIM
    lam = jnp.broadcast_to(log_decay.astype(F32).T[:, :, None], (n_heads, 2, RET_QK_DIM))
    lam_rows = lam.transpose(1, 0, 2).reshape(2, qk_w)
    cuts = [0, qk_w, 2 * qk_w, 2 * qk_w + v_w, 2 * qk_w + 2 * v_w]
    seg = lambda i: w_in[:, cuts[i]:cuts[i + 1]]
    w = jnp.concatenate([seg(1), seg(0), seg(3), seg(2)], axis=1).astype(BF16)
    groups = [(qk_w, "ret_k", BF16), (qk_w, "rope", BF16), (v_w, "silu", BF16), (v_w, None, BF16)]
    k, kf, kb, q, g, v = _project(h, w, groups, tm=tm, params=[lam_rows], rope=rope)
    zeros = jnp.zeros((n_batch, n_heads, RET_QK_DIM, RET_V_DIM), F32)
    ctx_kw = dict(row0=n_x, n_batch=n_batch, seq=ctx_len, block_rows=ctx_len, n_heads=n_heads)
    x_kw = dict(row0=0, n_batch=n_batch, seq=seq, block_rows=min(SCAN_BLOCK, seq), n_heads=n_heads)
    oc_b, sc_b = _ret_scan(q, k, kb, v, lam, zeros, fwd=False, **ctx_kw)
    yc, sc_f = _ret_scan(q, k, kf, v, lam, zeros, fwd=True, gate=g, prev=oc_b, **ctx_kw)
    ox_b, _ = _ret_scan(q, k, kb, v, lam, sc_b, fwd=False, **x_kw)
    yx, _ = _ret_scan(q, k, kf, v, lam, sc_f, fwd=True, gate=g, prev=ox_b, **x_kw)
    return [yx], ([yc] if ctx_out else None)


def kernel(x, c, ctx, c_ctx, ada_w, ada_b, ln_g, ln_b, e_w_in, e_w_out, na_rpb, hg_lb_logits, hg_norm_g, ffn_w1, ffn_w3,
           ffn_w2, o_w_in, o_w_out, ret_log_decay, router_w, router_b, moe_w1, moe_w3, moe_w2):
    n_batch, seq, d = x.shape
    ctx_len = ctx.shape[1]
    depth = ada_w.shape[0]
    n_x = n_batch * seq
    n_all = n_x + n_batch * ctx_len
    alpha = (2 * depth) ** 0.25
    row_gcd = math.gcd(math.gcd(n_x, n_all), seq)
    tm_even = _pick_tile(row_gcd, EVEN_PROJ_TILE, unit=RET_CHUNK)
    tm_odd = _pick_tile(row_gcd, ODD_PROJ_TILE, unit=RET_CHUNK)
    dims = dict(n_batch=n_batch, seq=seq, ctx_len=ctx_len)

    lb_cum = jnp.cumsum(jax.nn.softmax(hg_lb_logits.astype(F32), axis=1), axis=1)
    lower_bounds = lb_cum - lb_cum[:, :1]
    cos, sin = _rope_tables(seq, GRID_W, tm_odd)
    tiles_per_seq = seq // tm_odd
    rope = (cos, sin, lambda i: jnp.where(i < n_x // tm_odd, i % tiles_per_seq, tiles_per_seq))

    cond = jnp.concatenate([c, c_ctx[None, :], jnp.zeros((8 - n_batch - 1, d), F32)], axis=0)
    all_mods = _silu_matmul(cond, ada_w, ada_b, tn=d).reshape(depth, 8, 6, d)
    mods = [all_mods[l, :n_batch + 1] for l in range(depth)]

    ffn_w = [w.astype(BF16)[:, None] for w in (ffn_w1, ffn_w3, ffn_w2)]
    moe_w = [moe_w1.astype(BF16), moe_w3.astype(BF16), moe_w2]

    tok = jnp.concatenate([x.reshape(n_x, d), ctx.reshape(n_batch * ctx_len, d)], axis=0)
    h = _modulate(tok, mods[0][:, 0:2], rows_per_batch=seq)
    for layer in range(depth):
        j = layer // 2
        last = layer == depth - 1
        mod = mods[layer]
        n_rows = n_x if last else n_all
        if layer % 2 == 0:
            parts_x, parts_c = _even_mixer(h, e_w_in[j], na_rpb[j], lower_bounds[:, j], hg_norm_g[j], ctx_out=not last,
                                           tm=tm_even, **dims)
            w_out, router = e_w_out[j], None
        else:
            parts_x, parts_c = _odd_mixer(h, o_w_in[j], ret_log_decay[j], rope, ctx_out=not last, tm=tm_odd, **dims)
            rw = jnp.zeros((d, LANES), F32).at[:, :N_EXPERTS].set(router_w[j])
            rb = jnp.zeros((1, LANES), F32).at[0, :N_EXPERTS].set(router_b[j])
            w_out, router = o_w_out[j], (rw, rb)
        lnp = lambda i: jnp.stack([ln_g[layer, i], ln_b[layer, i]])
        res = _mix_out(parts_x, parts_c, w_out.astype(BF16), tok, mod[:, 2:5], lnp(0), router, alpha=alpha,
                       rows_per_batch=seq)
        tok, h = res[0], res[1]
        nxt = mods[layer + 1][:, 0:2] if not last else jnp.stack([jnp.zeros_like(mod[:, 0])] * 2, axis=1)
        mod2 = jnp.concatenate([mod[:, 5:6], nxt], axis=1)
        if layer % 2 == 0:
            ones = jnp.ones((n_rows // ROW_TILE,), jnp.int32)
            res = _grouped_swiglu(h, *ffn_w, j, 0 * ones, ones, tm=ROW_TILE, tf=ffn_w1.shape[2],
                                  ln=dict(x=tok, mod=mod2, lnp=lnp(1), alpha=alpha, rows_per_batch=seq))
        else:
            gates, idx = res[2], res[3]
            slot_token, assign_slot, tile_expert, tile_valid = _route(idx[:, :2], N_EXPERTS, ROW_TILE)
            xs = jnp.take(h, slot_token, axis=0, mode="clip")
            ys = _grouped_swiglu(xs, *moe_w, j, tile_expert, tile_valid, tm=ROW_TILE,
                                 tf=_pick_tile(moe_w1.shape[3], 1792))
            y12 = jnp.take(ys, assign_slot.T.reshape(-1), axis=0, mode="clip")
            res = _residual_ln(tok, y12, mod2, lnp(1), alpha=alpha, rows_per_batch=seq, y_gates=gates)
        tok, h = res[0], res[1]
    return tok[:n_x].reshape(n_batch, seq, d)
```

```python
import functools
import math

import jax
import jax.numpy as jnp
import numpy as np
from jax import lax
from jax.experimental import pallas as pl
from jax.experimental.pallas import tpu as pltpu

F32 = jnp.float32
BF16 = jnp.bfloat16

GRID_W = 64
NA_HEAD_DIM = 64
NA_KH = 8
NA_KW = 16
HG_KEY_DIM = 128
RET_QK_DIM = 256
RET_V_DIM = 512
N_EXPERTS = 8
ROPE_BASE = 10000.0
LN_EPS = 1e-5
NORM_EPS = 1e-6
LOG2_E = 1.4426950408889634
LANES = 128
SUBLANES = 8
HG_CHUNK = 128
RET_CHUNK = 256
ROW_TILE = 512
EVEN_PROJ_TILE = 512
ODD_PROJ_TILE = 512
SCAN_BLOCK = 2048
HG_SCAN_BLOCK = 4096
FFN_SUB = 256
VMEM_LIMIT = 48 * 1024 * 1024
MASK_VALUE = -1e30


def _cparams(*sem):
    return pltpu.CompilerParams(dimension_semantics=sem, vmem_limit_bytes=VMEM_LIMIT)


def _dot(a, b):
    return jnp.dot(a, b, preferred_element_type=F32)


def _dot_nt(a, b):
    return lax.dot_general(a, b, (((1,), (1,)), ((), ())), preferred_element_type=F32)


def _dot_tn(a, b):
    return lax.dot_general(a, b, (((0,), (0,)), ((), ())), preferred_element_type=F32)


def _dot_3pass(a, b):
    a_hi = a.astype(BF16)
    b_hi = b.astype(BF16)
    a_lo = (a - a_hi.astype(F32)).astype(BF16)
    b_lo = (b - b_hi.astype(F32)).astype(BF16)
    return _dot(a_hi, b_hi) + _dot(a_lo, b_hi) + _dot(a_hi, b_lo)


def _silu(x):
    return x * jax.nn.sigmoid(x)


def _pick_tile(n, target, unit=LANES):
    best = None
    for t in range(unit, min(n, target) + 1, unit):
        if n % t == 0:
            best = t
    assert best is not None, (n, target, unit)
    return best


def _rope(x, cos, sin):
    half = LANES // 2
    swapped = jnp.concatenate([pltpu.roll(x[:, g * LANES:(g + 1) * LANES], half, 1) for g in range(x.shape[1] // LANES)],
                              axis=1)
    return x * cos + swapped * sin


def _mm_kernel(a_ref, w_ref, b_ref, o_ref):
    a = _silu(a_ref[...])
    o_ref[0] = _dot(a.astype(BF16), w_ref[0].astype(BF16)) + b_ref[0]


def _silu_matmul(a, w, bias, *, tn):
    m, k = a.shape
    n_layers, _, n = w.shape
    assert n % tn == 0
    return pl.pallas_call(
        _mm_kernel,
        grid=(n_layers, n // tn),
        in_specs=[pl.BlockSpec((m, k), lambda l, j: (0, 0)),
                  pl.BlockSpec((1, k, tn), lambda l, j: (l, 0, j)),
                  pl.BlockSpec((1, 1, tn), lambda l, j: (l, 0, j))],
        out_specs=pl.BlockSpec((1, m, tn), lambda l, j: (l, 0, j)),
        out_shape=jax.ShapeDtypeStruct((n_layers, m, n), F32),
        compiler_params=_cparams("parallel", "parallel"),
        name="silu_matmul",
    )(a, w, bias.reshape(n_layers, 1, n))


def _proj_kernel(*refs, groups, sub, has_rope):
    a_ref, w_ref = refs[0], refs[1]
    pos = 2
    if has_rope:
        cos_ref, sin_ref = refs[2], refs[3]
        pos = 4
    params = []
    for _, epilogue, _ in groups:
        params.append(refs[pos] if epilogue in ("log_forget", "ret_k") else None)
        pos += epilogue in ("log_forget", "ret_k")
    outs = refs[pos:]
    a = a_ref[...]
    c0 = 0
    oi = 0
    for (n_cols, epilogue, _), pr in zip(groups, params):
        for s0 in range(0, n_cols, sub):
            cols = slice(s0, s0 + sub)
            acc = _dot(a, w_ref[:, c0 + s0:c0 + s0 + sub])
            if epilogue is None:
                outs[oi][:, cols] = acc.astype(outs[oi].dtype)
            elif epilogue == "silu":
                outs[oi][:, cols] = _silu(acc).astype(outs[oi].dtype)
            elif epilogue == "log_forget":
                log_sig = jnp.minimum(acc, 0.0) - jnp.log(1.0 + jnp.exp(-jnp.abs(acc)))
                log_lb = pr[0:1, cols]
                bb = pr[1:2, cols] + log_sig
                outs[oi][:, cols] = jnp.maximum(log_lb, bb) + jnp.log(1.0 + jnp.exp(-jnp.abs(log_lb - bb)))
            else:
                reps = sub // cos_ref.shape[1]
                r = _rope(acc, jnp.concatenate([cos_ref[...]] * reps, axis=1), jnp.concatenate([sin_ref[...]] * reps, axis=1))
                if epilogue == "rope":
                    outs[oi][:, cols] = r.astype(outs[oi].dtype)
                else:
                    k = r * (RET_QK_DIM ** -0.5)
                    ipos = (lax.broadcasted_iota(jnp.int32, k.shape, 0) & (RET_CHUNK - 1)).astype(F32)
                    outs[oi][:, cols] = k.astype(BF16)
                    outs[oi + 1][:, cols] = (k * jnp.exp((RET_CHUNK - 1.0 - ipos) * pr[0:1, cols])).astype(BF16)
                    outs[oi + 2][:, cols] = (k * jnp.exp(ipos * pr[1:2, cols])).astype(BF16)
        c0 += n_cols
        oi += 3 if epilogue == "ret_k" else 1


def _project(a, w, groups, *, tm, params=(), rope=None, sub=512):
    n_rows, k = a.shape
    assert n_rows % tm == 0 and sum(g[0] for g in groups) == w.shape[1] and all(g[0] % sub == 0 for g in groups)
    in_specs = [pl.BlockSpec((tm, k), lambda i: (i, 0)), pl.BlockSpec(w.shape, lambda i: (0, 0))]
    args = [a, w]
    if rope is not None:
        cos, sin, block_fn = rope
        in_specs += [pl.BlockSpec((tm, cos.shape[1]), lambda i: (block_fn(i), 0))] * 2
        args += [cos, sin]
    for pr in params:
        in_specs.append(pl.BlockSpec(pr.shape, lambda i: (0, 0)))
        args.append(pr)
    out_specs, out_shape = [], []
    for n_cols, epilogue, dtype in groups:
        for _ in range(3 if epilogue == "ret_k" else 1):
            out_specs.append(pl.BlockSpec((tm, n_cols), lambda i: (i, 0)))
            out_shape.append(jax.ShapeDtypeStruct((n_rows, n_cols), dtype))
    return pl.pallas_call(
        functools.partial(_proj_kernel, groups=tuple(groups), sub=sub, has_rope=rope is not None),
        grid=(n_rows // tm,),
        in_specs=in_specs,
        out_specs=out_specs,
        out_shape=out_shape,
        compiler_params=_cparams("parallel"),
        name="project",
    )(*args)


def _mix_out_kernel(*refs, n_parts, n_x_tiles, has_ctx, tail):
    n_in = n_parts * (2 if has_ctx else 1)
    w_ref = refs[n_in]
    is_ctx = pl.program_id(0) >= n_x_tiles
    acc = None
    k0 = 0
    for p in range(n_parts):
        if has_ctx:
            a = jnp.where(is_ctx, refs[2 * p + 1][...], refs[2 * p][...])
        else:
            a = refs[p][...]
        k1 = k0 + a.shape[1]
        part = _dot(a, w_ref[k0:k1, :])
        acc = part if acc is None else acc + part
        k0 = k1
    tail(acc, *refs[n_in + 1:])


def _mix_out(parts_x, parts_c, w, x, mod, lnp, router, *, alpha, rows_per_batch, tm=ROW_TILE):
    n_x = parts_x[0].shape[0]
    has_ctx = parts_c is not None
    n_c = parts_c[0].shape[0] if has_ctx else 0
    assert n_x % tm == 0 and n_c % tm == 0
    n_x_tiles, n_c_tiles = n_x // tm, n_c // tm
    in_specs, args = [], []
    for p, ax in enumerate(parts_x):
        in_specs.append(pl.BlockSpec((tm, ax.shape[1]), lambda i: (jnp.minimum(i, n_x_tiles - 1), 0)))
        args.append(ax)
        if has_ctx:
            in_specs.append(pl.BlockSpec((tm, ax.shape[1]), lambda i: (jnp.maximum(i - n_x_tiles, 0), 0)))
            args.append(parts_c[p])
    in_specs.append(pl.BlockSpec(w.shape, lambda i: (0, 0)))
    args.append(w)
    tail, t_in, t_args, out_specs, out_shape = _ln_tail_specs(
        x, mod, lnp, router, alpha=alpha, tm=tm, rows_per_batch=rows_per_batch, n_rows=n_x + n_c, row_of=lambda i: i)
    return pl.pallas_call(
        functools.partial(_mix_out_kernel, n_parts=len(parts_x), n_x_tiles=n_x_tiles, has_ctx=has_ctx, tail=tail),
        grid=(n_x_tiles + n_c_tiles,),
        in_specs=in_specs + t_in,
        out_specs=out_specs,
        out_shape=out_shape,
        compiler_params=_cparams("parallel"),
        name="mix_out",
    )(*args, *t_args)


def _modulate_kernel(x_ref, mod_ref, h_ref):
    m = mod_ref[0]
    h_ref[...] = (x_ref[...] * (1.0 + m[1:2]) + m[0:1]).astype(h_ref.dtype)


def _mod_index(tiles_per_batch, n_batch):
    return lambda i: (jnp.minimum(i // tiles_per_batch, n_batch), 0, 0)


def _modulate(x, mod, *, rows_per_batch, tm=ROW_TILE):
    m, d = x.shape
    return pl.pallas_call(
        _modulate_kernel,
        grid=(m // tm,),
        in_specs=[pl.BlockSpec((tm, d), lambda i: (i, 0)),
                  pl.BlockSpec((1, 2, d), _mod_index(rows_per_batch // tm, mod.shape[0] - 1))],
        out_specs=pl.BlockSpec((tm, d), lambda i: (i, 0)),
        out_shape=jax.ShapeDtypeStruct((m, d), BF16),
        compiler_params=_cparams("parallel"),
        name="modulate",
    )(x, mod)


def _ln_tail(y, x_ref, mod_ref, lnp_ref, *rest, alpha, router, n_experts):
    if router:
        rw_ref, rb_ref, xo_ref, h_ref, g_ref, i_ref = rest
    else:
        xo_ref, h_ref = rest
    m = mod_ref[0]
    z = alpha * x_ref[...] + m[0:1] * y
    mu = jnp.mean(z, axis=-1, keepdims=True)
    zc = z - mu
    var = jnp.mean(zc * zc, axis=-1, keepdims=True)
    xn = zc * lax.rsqrt(var + LN_EPS) * lnp_ref[0:1, :] + lnp_ref[1:2, :]
    xo_ref[...] = xn
    hf = xn * (1.0 + m[2:3]) + m[1:2]
    h_ref[...] = hf.astype(h_ref.dtype)
    if router:
        logits = _dot_3pass(hf, rw_ref[...]) + rb_ref[...]
        lane = lax.broadcasted_iota(jnp.int32, logits.shape, 1).astype(F32)
        neg = -jnp.inf
        lg = jnp.where(lane < n_experts, logits, neg)
        m1 = jnp.max(lg, axis=-1, keepdims=True)
        i1 = jnp.min(jnp.where(lg == m1, lane, float(LANES)), axis=-1, keepdims=True)
        lg2 = jnp.where(lane == i1, neg, lg)
        m2 = jnp.max(lg2, axis=-1, keepdims=True)
        i2 = jnp.min(jnp.where(lg2 == m2, lane, float(LANES)), axis=-1, keepdims=True)
        e2 = jnp.exp(m2 - m1)
        den = 1.0 + e2
        g_ref[...] = jnp.where(lane == 0, 1.0 / den, jnp.where(lane == 1, e2 / den, 0.0))
        i_ref[...] = jnp.where(lane == 0, i1, jnp.where(lane == 1, i2, 0.0)).astype(jnp.int32)


def _ln_tail_specs(x, mod, lnp, router, *, alpha, tm, rows_per_batch, n_rows, row_of):
    d = x.shape[1]
    tiles_per_batch, n_batch = rows_per_batch // tm, mod.shape[0] - 1
    row = pl.BlockSpec((tm, d), lambda *g: (row_of(*g), 0))
    lane_row = pl.BlockSpec((tm, LANES), lambda *g: (row_of(*g), 0))
    const = lambda shape: pl.BlockSpec(shape, lambda *g: (0,) * len(shape))
    in_specs = [row, pl.BlockSpec((1, 3, d), lambda *g: (jnp.minimum(row_of(*g) // tiles_per_batch, n_batch), 0, 0)),
                const((2, d))]
    args = [x, mod, lnp]
    out_specs = [row, row]
    out_shape = [jax.ShapeDtypeStruct((n_rows, d), F32), jax.ShapeDtypeStruct((n_rows, d), BF16 if router is None else F32)]
    if router is not None:
        in_specs += [const((d, LANES)), const((1, LANES))]
        args += list(router)
        out_specs += [lane_row, lane_row]
        out_shape += [jax.ShapeDtypeStruct((n_rows, LANES), F32), jax.ShapeDtypeStruct((n_rows, LANES), jnp.int32)]
    tail = functools.partial(_ln_tail, alpha=alpha, router=router is not None, n_experts=N_EXPERTS)
    return tail, in_specs, args, out_specs, out_shape


def _ln_kernel(*refs, two_y, tail):
    if two_y:
        y_ref, y2_ref, yg_ref = refs[:3]
        yg = yg_ref[...]
        y = yg[:, 0:1] * y_ref[...].astype(F32) + yg[:, 1:2] * y2_ref[...].astype(F32)
        refs = refs[3:]
    else:
        y = refs[0][...].astype(F32)
        refs = refs[1:]
    tail(y, *refs)


def _residual_ln(x, y, mod, lnp, *, alpha, rows_per_batch, n_rows=None, y_gates=None, router=None, tm=ROW_TILE):
    d = x.shape[1]
    n_rows = x.shape[0] if n_rows is None else n_rows
    row = pl.BlockSpec((tm, d), lambda i: (i, 0))
    two_y = y_gates is not None
    in_specs, args = [row], [y]
    if two_y:
        second_half = n_rows // tm
        in_specs += [pl.BlockSpec((tm, d), lambda i: (second_half + i, 0)), pl.BlockSpec((tm, LANES), lambda i: (i, 0))]
        args += [y, y_gates]
    tail, t_in, t_args, out_specs, out_shape = _ln_tail_specs(
        x, mod, lnp, router, alpha=alpha, tm=tm, rows_per_batch=rows_per_batch, n_rows=n_rows, row_of=lambda i: i)
    return pl.pallas_call(
        functools.partial(_ln_kernel, two_y=two_y, tail=tail),
        grid=(n_rows // tm,),
        in_specs=in_specs + t_in,
        out_specs=out_specs,
        out_shape=out_shape,
        compiler_params=_cparams("parallel"),
        name="residual_ln",
    )(*args, *t_args)


def _gffn_kernel(te_ref, tv_ref, x_ref, w1_ref, w3_ref, w2_ref, *rest, n_f, tail):
    acc_scr = rest[-1]
    i, f = pl.program_id(0), pl.program_id(1)

    @pl.when(f == 0)
    def _():
        acc_scr[...] = jnp.zeros_like(acc_scr)

    @pl.when(tv_ref[i] != 0)
    def _():
        x = x_ref[...].astype(BF16)
        tf = w1_ref.shape[3]
        sub = _pick_tile(tf, FFN_SUB)
        part = None
        for c0 in range(0, tf, sub):
            h1 = _dot(x, w1_ref[0, 0, :, c0:c0 + sub])
            h3 = _dot(x, w3_ref[0, 0, :, c0:c0 + sub])
            a = (_silu(h1) * h3).astype(BF16)
            down = _dot(a, w2_ref[0, 0, c0:c0 + sub, :].astype(BF16))
            part = down if part is None else part + down
        acc_scr[...] += part

    @pl.when(f == n_f - 1)
    def _():
        if tail is None:
            rest[0][...] = acc_scr[...]
        else:
            tail(acc_scr[...], *rest[:-1])


def _grouped_swiglu(xs, w1, w3, w2, layer, tile_expert, tile_valid, *, tm, tf, ln=None):
    p, d = xs.shape
    f = w1.shape[3]
    assert p % tm == 0 and f % tf == 0
    if w1.shape[1] == 1:
        expert_of = lambda i, te: 0
        f_tile_of = lambda i, j, tv: j
    else:
        expert_of = lambda i, te: te[i]
        f_tile_of = lambda i, j, tv: j * tv[i]
    in_specs = [pl.BlockSpec((tm, d), lambda i, j, te, tv: (i, 0)),
                pl.BlockSpec((1, 1, d, tf), lambda i, j, te, tv: (layer, expert_of(i, te), 0, f_tile_of(i, j, tv))),
                pl.BlockSpec((1, 1, d, tf), lambda i, j, te, tv: (layer, expert_of(i, te), 0, f_tile_of(i, j, tv))),
                pl.BlockSpec((1, 1, tf, d), lambda i, j, te, tv: (layer, expert_of(i, te), f_tile_of(i, j, tv), 0))]
    args = [xs, w1, w3, w2]
    if ln is None:
        tail = None
        out_specs = pl.BlockSpec((tm, d), lambda i, j, te, tv: (i, 0))
        out_shape = jax.ShapeDtypeStruct((p, d), F32)
    else:
        tail, t_in, t_args, out_specs, out_shape = _ln_tail_specs(
            ln["x"], ln["mod"], ln["lnp"], None, alpha=ln["alpha"], tm=tm, rows_per_batch=ln["rows_per_batch"],
            n_rows=p, row_of=lambda i, j, te, tv: i)
        in_specs += t_in
        args += t_args
    grid_spec = pltpu.PrefetchScalarGridSpec(
        num_scalar_prefetch=2,
        grid=(p // tm, f // tf),
        in_specs=in_specs,
        out_specs=out_specs,
        scratch_shapes=[pltpu.VMEM((tm, d), F32)],
    )
    return pl.pallas_call(
        functools.partial(_gffn_kernel, n_f=f // tf, tail=tail),
        grid_spec=grid_spec,
        out_shape=out_shape,
        compiler_params=_cparams("parallel", "arbitrary"),
        name="grouped_swiglu",
    )(tile_expert, tile_valid, *args)


def _route(idx, n_experts, tm):
    m = idx.shape[0]
    flat = idx.reshape(-1)
    n = flat.shape[0]
    p = -(-(n + n_experts * (tm - 1)) // tm) * tm
    onehot = (flat[:, None] == jnp.arange(n_experts, dtype=jnp.int32)[None, :]).astype(jnp.int32)
    csum = jnp.cumsum(onehot, axis=0)
    counts = csum[-1]
    padded = -(-counts // tm) * tm
    pad_end = jnp.cumsum(padded)
    pad_off = pad_end - padded
    rank = jnp.sum(csum * onehot, axis=1) - 1
    assign_slot = (pad_off[flat] + rank).reshape(m, 2)
    slot = jnp.arange(p, dtype=jnp.int32)
    slot_expert = jnp.minimum(jnp.sum(pad_end[None, :] <= slot[:, None], axis=1), n_experts - 1).astype(jnp.int32)
    slot_rank = slot - pad_off[slot_expert]
    order = jnp.argsort(flat, stable=True).astype(jnp.int32)
    off = jnp.cumsum(counts) - counts
    src = order[jnp.minimum(off[slot_expert] + slot_rank, n - 1)]
    slot_token = jnp.where(slot_rank < counts[slot_expert], src // 2, slot % m).astype(jnp.int32)
    tile_start = slot[::tm]
    tile_valid = (tile_start < pad_end[-1]).astype(jnp.int32)
    tile_expert = slot_expert[::tm]
    return slot_token, assign_slot, tile_expert, tile_valid


def _softmax_pv(parts):
    mx = functools.reduce(jnp.maximum, [jnp.max(s, axis=-1, keepdims=True) for s, _ in parts])
    ps = [jnp.exp(s - mx) for s, _ in parts]
    den = functools.reduce(jnp.add, [jnp.sum(p, axis=-1, keepdims=True) for p in ps])
    num = functools.reduce(jnp.add, [_dot(p.astype(BF16), v) for p, (_, v) in zip(ps, parts)])
    return num / den


def _na_kernel(*refs, rows, gw, ctx_out):
    if ctx_out:
        q_ref, k_ref, v_ref, kc_ref, vc_ref, qc_ref, bias_ref, o_ref, oc_ref, s_scr, p_scr, l_scr = refs
    else:
        q_ref, k_ref, v_ref, kc_ref, vc_ref, bias_ref, o_ref, s_scr, p_scr, l_scr = refs
    lane = lax.broadcasted_iota(jnp.int32, (1, LANES), 1)
    head_lanes = [lane < NA_HEAD_DIM, lane >= NA_HEAD_DIM]
    scale = NA_HEAD_DIM ** -0.5
    band = NA_KH * gw

    def band_rows(r):
        b0 = jnp.clip(r - NA_KH // 2, 0, rows - NA_KH)
        return b0, pl.ds(pl.multiple_of(b0 * gw, gw), band)

    def scores(r, par):
        r = jnp.minimum(r, rows - 1)
        b0, ks = band_rows(r)
        dr0 = b0 - r + NA_KH - 1
        q = q_ref[pl.ds(pl.multiple_of(r * gw, gw), gw), :]
        q2 = jnp.concatenate([jnp.where(head_lanes[0], q, 0), jnp.where(head_lanes[1], q, 0)], axis=0) * scale
        s_scr[par, :, :band] = _dot_nt(q2, k_ref[ks, :]) + bias_ref[0, dr0]
        s_scr[par, :, band:] = _dot_nt(q2, kc_ref[...])

    def softmax(par):
        s = s_scr[par]
        e = jnp.exp(s - jnp.max(s, axis=-1, keepdims=True))
        l_scr[par] = jnp.broadcast_to(jnp.sum(e, axis=-1, keepdims=True), (2 * gw, LANES))
        p_scr[par] = e.astype(BF16)

    def weighted_values(r, par):
        _, ks = band_rows(r)
        num = _dot(p_scr[par, :, :band], v_ref[ks, :]) + _dot(p_scr[par, :, band:], vc_ref[...])
        out = num / l_scr[par]
        o_ref[pl.ds(pl.multiple_of(r * gw, gw), gw), :] = jnp.where(head_lanes[0], out[:gw], out[gw:]).astype(o_ref.dtype)

    scores(0, 0)
    scores(1, 1)
    softmax(0)

    def body(i, carry):
        r = 2 * i
        weighted_values(r, 0)
        softmax(1)
        scores(r + 2, 0)
        weighted_values(r + 1, 1)
        softmax(0)
        scores(r + 3, 1)
        return carry

    lax.fori_loop(0, rows // 2, body, 0)
    if ctx_out:
        qc = qc_ref[...]
        kc = kc_ref[...]
        vc = vc_ref[...]
        outs = []
        for h in range(2):
            qh = jnp.where(head_lanes[h], qc, 0) * scale
            outs.append(_softmax_pv([(_dot_nt(qh, kc), vc)]))
        oc_ref[...] = jnp.where(head_lanes[0], outs[0], outs[1]).astype(oc_ref.dtype)


def _na_bias_table(rpb, gw):
    col = jnp.arange(gw)
    c_start = jnp.clip(col - NA_KW // 2, 0, gw - NA_KW)
    col_in = (col[None, :] >= c_start[:, None]) & (col[None, :] < c_start[:, None] + NA_KW)
    dc = jnp.clip(col[None, :] - col[:, None], 1 - NA_KW, NA_KW - 1) + NA_KW - 1
    t = jnp.where(col_in[None, None], rpb[:, :, dc].astype(F32), MASK_VALUE)
    win = jnp.stack([t[:, d:d + NA_KH] for d in range(NA_KH)], axis=1)
    per_head = win.transpose(0, 1, 3, 2, 4).reshape(rpb.shape[0] // 2, 2, NA_KH, gw, NA_KH * gw)
    return per_head.transpose(0, 2, 1, 3, 4).reshape(rpb.shape[0] // 2, NA_KH, 2 * gw, NA_KH * gw)


def _neighbourhood_attention(proj, rpb, *, n_batch, seq, ctx_len, ctx_out):
    width = rpb.shape[0] * NA_HEAD_DIM
    pairs = width // LANES
    rows = seq // GRID_W
    bias = _na_bias_table(rpb, GRID_W)
    n_keys = NA_KH * GRID_W + ctx_len
    cblk0 = n_batch * seq // ctx_len
    xspec = lambda g: pl.BlockSpec((seq, LANES), lambda b, p: (b, g * pairs + p))
    cspec = lambda g: pl.BlockSpec((ctx_len, LANES), lambda b, p: (cblk0 + b, g * pairs + p))
    in_specs = [xspec(0), xspec(1), xspec(2), cspec(1), cspec(2)]
    args = [proj, proj, proj, proj, proj]
    if ctx_out:
        in_specs.append(cspec(0))
        args.append(proj)
    in_specs.append(pl.BlockSpec((1, NA_KH, 2 * GRID_W, NA_KH * GRID_W), lambda b, p: (p, 0, 0, 0)))
    args.append(bias)
    out_specs = [pl.BlockSpec((seq, LANES), lambda b, p: (b, p))]
    out_shape = [jax.ShapeDtypeStruct((n_batch * seq, width), BF16)]
    if ctx_out:
        out_specs.append(pl.BlockSpec((ctx_len, LANES), lambda b, p: (b, p)))
        out_shape.append(jax.ShapeDtypeStruct((n_batch * ctx_len, width), BF16))
    return pl.pallas_call(
        functools.partial(_na_kernel, rows=rows, gw=GRID_W, ctx_out=ctx_out),
        grid=(n_batch, pairs),
        in_specs=in_specs,
        out_specs=out_specs,
        out_shape=out_shape,
        scratch_shapes=[pltpu.VMEM((2, 2 * GRID_W, n_keys), F32),
                        pltpu.VMEM((2, 2 * GRID_W, n_keys), BF16),
                        pltpu.VMEM((2, 2 * GRID_W, LANES), F32)],
        compiler_params=_cparams("parallel", "parallel"),
        name="neighbourhood_attention",
    )(*args)


def _hg_kernel(*refs, rev, n_chunks, n_blocks, epilogue):
    if epilogue:
        q_ref, f_ref, i_ref, s0_ref, pm_ref, rm_ref, g_ref, prev_ref, ng_ref, o_ref, sT_ref = refs[:11]
    else:
        q_ref, f_ref, i_ref, s0_ref, pm_ref, rm_ref, o_ref, sT_ref = refs[:8]
    st_scr, att_scr, qd_scr, ku_scr, dec_scr = refs[-5:]
    blk = pl.program_id(2)

    @pl.when(blk == 0)
    def _():
        st_scr[...] = s0_ref[0, 0]

    c = HG_CHUNK

    def rows_of(ci):
        ci = jnp.minimum(ci, n_chunks - 1)
        cc = (n_chunks - 1 - ci) if rev else ci
        return pl.ds(pl.multiple_of(cc * c, c), c)

    def decays(ci, slot):
        sl = rows_of(ci)
        q = q_ref[sl, :].astype(F32)
        lf = f_ref[sl, :] * LOG2_E
        k = 1.0 - jnp.exp2(lf)
        p, tot = lf, lf
        att = pm_ref[0] * jnp.sum(q * k, axis=-1, keepdims=True)
        for lvl in range(1, pm_ref.shape[0]):
            m = 1 << (lvl - 1)
            if m < SUBLANES:
                second = rm_ref[lvl - 1] != 0.0
                is_q = jnp.logical_not(second) if rev else second
                z = (jnp.where(is_q, q, k) * jnp.exp2(jnp.where(is_q, p, tot - p))).astype(BF16)
                t_up = pltpu.roll(tot, m, 0)
                t_dn = pltpu.roll(tot, c - m, 0)
                p = p + (jnp.where(second, 0.0, t_dn) if rev else jnp.where(second, t_up, 0.0))
                tot = tot + jnp.where(second, t_up, t_dn)
            else:
                halves = lambda a: a.reshape(c // (2 * m), 2, m, a.shape[-1])
                join = lambda first, second: jnp.stack([first, second], axis=1).reshape(c, first.shape[-1])
                q4, k4, p4, t4 = halves(q), halves(k), halves(p), halves(tot)
                qh, kh = (0, 1) if rev else (1, 0)
                zq = q4[:, qh] * jnp.exp2(p4[:, qh])
                zk = k4[:, kh] * jnp.exp2(t4[:, kh] - p4[:, kh])
                z = (join(zq, zk) if rev else join(zk, zq)).astype(BF16)
                p_q = p4[:, qh] + t4[:, kh]
                p = join(p_q, p4[:, 1]) if rev else join(p4[:, 0], p_q)
                block_total = t4[:, 0] + t4[:, 1]
                tot = join(block_total, block_total)
            att = att + pm_ref[lvl] * _dot_nt(z, z)
        att_scr[slot] = att.astype(BF16)
        qd_scr[slot] = (q * jnp.exp2(p)).astype(BF16)
        ku_scr[slot] = (k * jnp.exp2(tot - p)).astype(BF16)
        dec_scr[slot] = jnp.exp2(tot[0:8, :])

    def outputs(ci, slot):
        sl = rows_of(ci)
        v = i_ref[sl, :]
        st = st_scr[...]
        o = _dot(att_scr[slot], v) + _dot_nt(qd_scr[slot], st.astype(BF16))
        st_scr[...] = st * dec_scr[slot, 0:1, :] + _dot_tn(v, ku_scr[slot])
        if epilogue:
            o = o + prev_ref[sl, :]
            o = o * lax.rsqrt(jnp.mean(o * o, axis=-1, keepdims=True) + NORM_EPS) * ng_ref[...]
            o = o * g_ref[sl, :].astype(F32)
        o_ref[sl, :] = o.astype(o_ref.dtype)

    decays(0, 0)

    def pair(i, carry):
        ci = 2 * i
        outputs(ci, 0)
        decays(ci + 1, 1)
        outputs(ci + 1, 1)
        decays(ci + 2, 0)
        return carry

    lax.fori_loop(0, n_chunks // 2, pair, 0)

    @pl.when(blk == n_blocks - 1)
    def _():
        sT_ref[0, 0] = st_scr[...]


def _hg_level_masks(rev):
    c = HG_CHUNK
    t = np.arange(c)[:, None]
    s = np.arange(c)[None, :]
    pair, row = [t == s], []
    m = 1
    while m < c:
        t_second = (t & m) != 0
        s_second = (s & m) != 0
        same = (t ^ s) < 2 * m
        pair.append(same & (~t_second & s_second if rev else t_second & ~s_second))
        if m < SUBLANES:
            row.append(np.broadcast_to(t_second, (c, HG_KEY_DIM)))
        m *= 2
    return jnp.asarray(np.stack(pair), F32), jnp.asarray(np.stack(row), F32)


def _hg_scan(qg, lf, proj, s0, *, rev, row0, n_batch, seq, block_rows, n_heads, col_f, col_i, prev=None, norm_g=None):
    dk = HG_KEY_DIM
    n_blocks = seq // block_rows
    blk0 = row0 // block_rows
    epilogue = prev is not None

    def local_rows(b, i):
        return b * n_blocks + ((n_blocks - 1 - i) if rev else i)

    pspec = lambda col: pl.BlockSpec((block_rows, dk), lambda b, h, i: (blk0 + local_rows(b, i), col + h))
    state_spec = pl.BlockSpec((1, 1, dk, dk), lambda b, h, i: (b, h, 0, 0))
    local_spec = pl.BlockSpec((block_rows, dk), lambda b, h, i: (local_rows(b, i), h))
    pair_mask, row_mask = _hg_level_masks(rev)
    const_spec = lambda a: pl.BlockSpec(a.shape, lambda b, h, i: (0, 0, 0))
    in_specs = [pspec(0), pspec(col_f), pspec(col_i), state_spec, const_spec(pair_mask), const_spec(row_mask)]
    args = [qg, lf, proj, s0, pair_mask, row_mask]
    if epilogue:
        in_specs += [pspec(n_heads), local_spec, pl.BlockSpec((1, dk), lambda b, h, i: (0, 0))]
        args += [qg, prev, norm_g.reshape(1, dk)]
    return pl.pallas_call(
        functools.partial(_hg_kernel, rev=rev, n_chunks=block_rows // HG_CHUNK, n_blocks=n_blocks, epilogue=epilogue),
        grid=(n_batch, n_heads, n_blocks),
        in_specs=in_specs,
        out_specs=[local_spec, state_spec],
        out_shape=[jax.ShapeDtypeStruct((n_batch * seq, n_heads * dk), BF16 if epilogue else F32),
                   jax.ShapeDtypeStruct((n_batch, n_heads, dk, dk), F32)],
        scratch_shapes=[pltpu.VMEM((dk, dk), F32),
                        pltpu.VMEM((2, HG_CHUNK, HG_CHUNK), BF16),
                        pltpu.VMEM((2, HG_CHUNK, dk), BF16),
                        pltpu.VMEM((2, HG_CHUNK, dk), BF16),
                        pltpu.VMEM((2, 8, dk), F32)],
        compiler_params=_cparams("parallel", "parallel", "arbitrary"),
        name="hgrn2_scan_rev" if rev else "hgrn2_scan_fwd",
    )(*args)


def _ret_kernel(*refs, fwd, n_chunks, n_blocks):
    if fwd:
        (q_ref, k_ref, kd_ref, v_ref, lam_ref, s0_ref, g_ref, prev_ref, o_ref, sT_ref, st_scr, qd_scr, dm_scr,
         att_scr) = refs
    else:
        q_ref, kd_ref, v_ref, lam_ref, s0_ref, o_ref, sT_ref, st_scr, qd_scr = refs
    blk = pl.program_id(2)
    c = RET_CHUNK
    lam_f = lam_ref[0, 0:1, :]
    lam_b = lam_ref[0, 1:2, :]
    lam = lam_f if fwd else lam_b

    @pl.when(blk == 0)
    def _():
        st_scr[...] = s0_ref[0, 0]
        ipos = lax.broadcasted_iota(jnp.int32, (c, LANES), 0).astype(F32)
        steps = (ipos + 1.0) if fwd else (c - ipos)
        qd_scr[...] = jnp.exp(steps * lam[:, :LANES])
        if fwd:
            dist = (lax.broadcasted_iota(jnp.int32, (c, c), 0) - lax.broadcasted_iota(jnp.int32, (c, c), 1)).astype(F32)
            dm_scr[...] = (jnp.where(dist >= 0, jnp.exp(jnp.maximum(dist, 0.0) * lam_f[:, :c]), 0.0)
                           + jnp.where(dist <= 0, jnp.exp(jnp.maximum(-dist, 0.0) * lam_b[:, :c]), 0.0))

    chunk_decay = jnp.exp(float(c) * lam[:, 0:1])

    def rows_of(ci):
        ci = jnp.minimum(ci, n_chunks - 1)
        cc = ci if fwd else (n_chunks - 1 - ci)
        return pl.ds(pl.multiple_of(cc * c, c), c)

    def scores(ci, slot):
        sl = rows_of(ci)
        att_scr[slot] = (_dot_nt(q_ref[sl, :], k_ref[sl, :]) * dm_scr[...]).astype(BF16)

    def outputs(ci, slot):
        sl = rows_of(ci)
        q = q_ref[sl, :]
        v = v_ref[sl, :]
        st = st_scr[...]
        o = _dot(q, st.astype(BF16)) * qd_scr[:, 0:1]
        st_scr[...] = st * chunk_decay + _dot_tn(kd_ref[sl, :], v)
        if fwd:
            o = o + _dot(att_scr[slot], v) + prev_ref[sl, :].astype(F32)
            mu = jnp.mean(o, axis=-1, keepdims=True)
            oc = o - mu
            var = jnp.mean(oc * oc, axis=-1, keepdims=True)
            o = oc * lax.rsqrt(var + LN_EPS) * g_ref[sl, :].astype(F32)
        o_ref[sl, :] = o.astype(o_ref.dtype)

    if not fwd:
        def chunk(ci, carry):
            outputs(ci, 0)
            return carry

        lax.fori_loop(0, n_chunks, chunk, 0, unroll=2)
    elif n_chunks == 1:
        scores(0, 0)
        outputs(0, 0)
    else:
        scores(0, 0)

        def pair(i, carry):
            ci = 2 * i
            outputs(ci, 0)
            scores(ci + 1, 1)
            outputs(ci + 1, 1)
            scores(ci + 2, 0)
            return carry

        lax.fori_loop(0, n_chunks // 2, pair, 0)

    @pl.when(blk == n_blocks - 1)
    def _():
        sT_ref[0, 0] = st_scr[...]


def _ret_scan(q, k, kd, v, lam, s0, *, fwd, row0, n_batch, seq, block_rows, n_heads, gate=None, prev=None):
    dk, dv = RET_QK_DIM, RET_V_DIM
    n_blocks = seq // block_rows
    blk0 = row0 // block_rows
    assert RET_CHUNK <= dk

    def lrow(b, i):
        return b * n_blocks + (i if fwd else (n_blocks - 1 - i))

    kspec = pl.BlockSpec((block_rows, dk), lambda b, h, i: (blk0 + lrow(b, i), h))
    vspec = pl.BlockSpec((block_rows, dv), lambda b, h, i: (blk0 + lrow(b, i), h))
    local_spec = pl.BlockSpec((block_rows, dv), lambda b, h, i: (lrow(b, i), h))
    state_spec = pl.BlockSpec((1, 1, dk, dv), lambda b, h, i: (b, h, 0, 0))
    lam_spec = pl.BlockSpec((1, 2, dk), lambda b, h, i: (h, 0, 0))
    scratch = [pltpu.VMEM((dk, dv), F32), pltpu.VMEM((RET_CHUNK, LANES), F32)]
    if fwd:
        in_specs = [kspec, kspec, kspec, vspec, lam_spec, state_spec, vspec, local_spec]
        args = [q, k, kd, v, lam, s0, gate, prev]
        scratch += [pltpu.VMEM((RET_CHUNK, RET_CHUNK), F32),
                    pltpu.VMEM((2, RET_CHUNK, RET_CHUNK), BF16)]
    else:
        in_specs = [kspec, kspec, vspec, lam_spec, state_spec]
        args = [q, kd, v, lam, s0]
    return pl.pallas_call(
        functools.partial(_ret_kernel, fwd=fwd, n_chunks=block_rows // RET_CHUNK, n_blocks=n_blocks),
        grid=(n_batch, n_heads, n_blocks),
        in_specs=in_specs,
        out_specs=[local_spec, state_spec],
        out_shape=[jax.ShapeDtypeStruct((n_batch * seq, n_heads * dv), BF16),
                   jax.ShapeDtypeStruct((n_batch, n_heads, dk, dv), F32)],
        scratch_shapes=scratch,
        compiler_params=_cparams("parallel", "parallel", "arbitrary"),
        name="retention_fwd" if fwd else "retention_rev",
    )(*args)


def _rope_tables(seq, gw, pad_rows):
    half = LANES // 2
    inv = ROPE_BASE ** (-jnp.arange(half, dtype=F32) / half)
    t = jnp.arange(seq)
    ang_r = (t // gw).astype(F32)[:, None] * inv
    ang_c = (t % gw).astype(F32)[:, None] * inv
    cos = jnp.concatenate([jnp.cos(ang_r)] * 2 + [jnp.cos(ang_c)] * 2, axis=-1)
    sin = jnp.concatenate([-jnp.sin(ang_r), jnp.sin(ang_r), -jnp.sin(ang_c), jnp.sin(ang_c)], axis=-1)
    cos = jnp.concatenate([cos, jnp.ones((pad_rows, cos.shape[1]), F32)], axis=0)
    sin = jnp.concatenate([sin, jnp.zeros((pad_rows, sin.shape[1]), F32)], axis=0)
    return cos, sin


def _even_mixer(h, w_in, rpb, lb, norm_g, *, n_batch, seq, ctx_len, ctx_out, tm):
    n_x = n_batch * seq
    na_w = rpb.shape[0] * NA_HEAD_DIM
    hg_w = lb.shape[1]
    n_heads = hg_w // HG_KEY_DIM
    cuts = [0, 3 * na_w] + [3 * na_w + i * hg_w for i in range(1, 6)]
    seg = lambda i: w_in[:, cuts[i]:cuts[i + 1]]
    w = jnp.concatenate([seg(2), seg(3), seg(1), seg(5), seg(0), seg(4)], axis=1).astype(BF16)
    lb_rows = jnp.stack([jnp.log(lb).reshape(-1), jnp.log1p(-lb).reshape(-1)])
    groups = [(2 * hg_w, "log_forget", F32),
              (2 * hg_w, "silu", BF16),
              (3 * na_w + hg_w, None, BF16)]
    lf, qg, proj = _project(h, w, groups, tm=tm, params=[lb_rows])
    na = _neighbourhood_attention(proj, rpb, n_batch=n_batch, seq=seq, ctx_len=ctx_len, ctx_out=ctx_out)
    zeros = jnp.zeros((n_batch, n_heads, HG_KEY_DIM, HG_KEY_DIM), F32)
    col_i = 3 * na_w // HG_KEY_DIM
    ctx_kw = dict(row0=n_x, n_batch=n_batch, seq=ctx_len, block_rows=ctx_len, n_heads=n_heads, col_i=col_i)
    x_kw = dict(row0=0, n_batch=n_batch, seq=seq, block_rows=min(HG_SCAN_BLOCK, seq), n_heads=n_heads, col_i=col_i)
    oc_f, sc_f = _hg_scan(qg, lf, proj, zeros, rev=False, col_f=0, **ctx_kw)
    gc, sc_b = _hg_scan(qg, lf, proj, zeros, rev=True, col_f=n_heads, prev=oc_f, norm_g=norm_g, **ctx_kw)
    ox_f, _ = _hg_scan(qg, lf, proj, sc_f, rev=False, col_f=0, **x_kw)
    gx, _ = _hg_scan(qg, lf, proj, sc_b, rev=True, col_f=n_heads, prev=ox_f, norm_g=norm_g, **x_kw)
    return [na[0], gx], ([na[1], gc] if ctx_out else None)


def _odd_mixer(h, w_in, log_decay, rope, *, n_batch, seq, ctx_len, ctx_out, tm):
    n_x = n_batch * seq
    n_heads = log_decay.shape[1]
    qk_w = n_heads * RET_QK_DIM
    v_w = n_heads * RET_V_DIM
    lam = jnp.broadcast_to(log_decay.astype(F32).T[:, :, None], (n_heads, 2, RET_QK_DIM))
    lam_rows = lam.transpose(1, 0, 2).reshape(2, qk_w)
    cuts = [0, qk_w, 2 * qk_w, 2 * qk_w + v_w, 2 * qk_w + 2 * v_w]
    seg = lambda i: w_in[:, cuts[i]:cuts[i + 1]]
    w = jnp.concatenate([seg(1), seg(0), seg(3), seg(2)], axis=1).astype(BF16)
    groups = [(qk_w, "ret_k", BF16), (qk_w, "rope", BF16), (v_w, "silu", BF16), (v_w, None, BF16)]
    k, kf, kb, q, g, v = _project(h, w, groups, tm=tm, params=[lam_rows], rope=rope)
    zeros = jnp.zeros((n_batch, n_heads, RET_QK_DIM, RET_V_DIM), F32)
    ctx_kw = dict(row0=n_x, n_batch=n_batch, seq=ctx_len, block_rows=ctx_len, n_heads=n_heads)
    x_kw = dict(row0=0, n_batch=n_batch, seq=seq, block_rows=min(SCAN_BLOCK, seq), n_heads=n_heads)
    oc_b, sc_b = _ret_scan(q, k, kb, v, lam, zeros, fwd=False, **ctx_kw)
    yc, sc_f = _ret_scan(q, k, kf, v, lam, zeros, fwd=True, gate=g, prev=oc_b, **ctx_kw)
    ox_b, _ = _ret_scan(q, k, kb, v, lam, sc_b, fwd=False, **x_kw)
    yx, _ = _ret_scan(q, k, kf, v, lam, sc_f, fwd=True, gate=g, prev=ox_b, **x_kw)
    return [yx], ([yc] if ctx_out else None)


def kernel(x, c, ctx, c_ctx, ada_w, ada_b, ln_g, ln_b, e_w_in, e_w_out, na_rpb, hg_lb_logits, hg_norm_g, ffn_w1, ffn_w3,
           ffn_w2, o_w_in, o_w_out, ret_log_decay, router_w, router_b, moe_w1, moe_w3, moe_w2):
    n_batch, seq, d = x.shape
    ctx_len = ctx.shape[1]
    depth = ada_w.shape[0]
    n_x = n_batch * seq
    n_all = n_x + n_batch * ctx_len
    alpha = (2 * depth) ** 0.25
    row_gcd = math.gcd(math.gcd(n_x, n_all), seq)
    tm_even = _pick_tile(row_gcd, EVEN_PROJ_TILE, unit=RET_CHUNK)
    tm_odd = _pick_tile(row_gcd, ODD_PROJ_TILE, unit=RET_CHUNK)
    dims = dict(n_batch=n_batch, seq=seq, ctx_len=ctx_len)

    lb_cum = jnp.cumsum(jax.nn.softmax(hg_lb_logits.astype(F32), axis=1), axis=1)
    lower_bounds = lb_cum - lb_cum[:, :1]
    cos, sin = _rope_tables(seq, GRID_W, tm_odd)
    tiles_per_seq = seq // tm_odd
    rope = (cos, sin, lambda i: jnp.where(i < n_x // tm_odd, i % tiles_per_seq, tiles_per_seq))

    cond = jnp.concatenate([c, c_ctx[None, :], jnp.zeros((8 - n_batch - 1, d), F32)], axis=0)
    all_mods = _silu_matmul(cond, ada_w, ada_b, tn=d).reshape(depth, 8, 6, d)
    mods = [all_mods[l, :n_batch + 1] for l in range(depth)]

    ffn_w = [w.astype(BF16)[:, None] for w in (ffn_w1, ffn_w3, ffn_w2)]
    moe_w = [moe_w1.astype(BF16), moe_w3.astype(BF16), moe_w2]

    tok = jnp.concatenate([x.reshape(n_x, d), ctx.reshape(n_batch * ctx_len, d)], axis=0)
    h = _modulate(tok, mods[0][:, 0:2], rows_per_batch=seq)
    for layer in range(depth):
        j = layer // 2
        last = layer == depth - 1
        mod = mods[layer]
        n_rows = n_x if last else n_all
        if layer % 2 == 0:
            parts_x, parts_c = _even_mixer(h, e_w_in[j], na_rpb[j], lower_bounds[:, j], hg_norm_g[j], ctx_out=not last,
                                           tm=tm_even, **dims)
            w_out, router = e_w_out[j], None
        else:
            parts_x, parts_c = _odd_mixer(h, o_w_in[j], ret_log_decay[j], rope, ctx_out=not last, tm=tm_odd, **dims)
            rw = jnp.zeros((d, LANES), F32).at[:, :N_EXPERTS].set(router_w[j])
            rb = jnp.zeros((1, LANES), F32).at[0, :N_EXPERTS].set(router_b[j])
            w_out, router = o_w_out[j], (rw, rb)
        lnp = lambda i: jnp.stack([ln_g[layer, i], ln_b[layer, i]])
        res = _mix_out(parts_x, parts_c, w_out.astype(BF16), tok, mod[:, 2:5], lnp(0), router, alpha=alpha,
                       rows_per_batch=seq)
        tok, h = res[0], res[1]
        nxt = mods[layer + 1][:, 0:2] if not last else jnp.stack([jnp.zeros_like(mod[:, 0])] * 2, axis=1)
        mod2 = jnp.concatenate([mod[:, 5:6], nxt], axis=1)
        if layer % 2 == 0:
            ones = jnp.ones((n_rows // ROW_TILE,), jnp.int32)
            res = _grouped_swiglu(h, *ffn_w, j, 0 * ones, ones, tm=ROW_TILE, tf=ffn_w1.shape[2],
                                  ln=dict(x=tok, mod=mod2, lnp=lnp(1), alpha=alpha, rows_per_batch=seq))
        else:
            gates, idx = res[2], res[3]
            slot_token, assign_slot, tile_expert, tile_valid = _route(idx[:, :2], N_EXPERTS, ROW_TILE)
            xs = jnp.take(h, slot_token, axis=0, mode="clip")
            ys = _grouped_swiglu(xs, *moe_w, j, tile_expert, tile_valid, tm=ROW_TILE,
                                 tf=_pick_tile(moe_w1.shape[3], 1792))
            y12 = jnp.take(ys, assign_slot.T.reshape(-1), axis=0, mode="clip")
            res = _residual_ln(tok, y12, mod2, lnp(1), alpha=alpha, rows_per_batch=seq, y_gates=gates)
        tok, h = res[0], res[1]
    return tok[:n_x].reshape(n_batch, seq, d)
```
